```python
import math
import jax, jax.numpy as jnp
from jax import lax
import numpy as np

D_MODEL = 4096
BATCH = 2
SEQ = 8192
DEPTH = 4

HGRN_HEADS = 8
HGRN_KDIM = 128
HGRN_VDIM = 128
HGRN_WIDTH = HGRN_HEADS * HGRN_KDIM
HGRN_CHUNK = 64
DIFF_HEADS = 8
DIFF_HEAD_DIM = 64
DIFF_WIDTH = DIFF_HEADS * 2 * DIFF_HEAD_DIM
Q_BLOCK = 128
N_BUCKETS = 32
MAX_DISTANCE = 128
RET_HEADS = 8
RET_KDIM = 128
RET_VDIM = 256
RET_QK_WIDTH = RET_HEADS * RET_KDIM
RET_V_WIDTH = RET_HEADS * RET_VDIM
RET_CHUNK = 128
ROPE_BASE = 10000.0
GATE_RANK = 256
N_BRANCH = 3
FFN_HIDDEN = ((8 * D_MODEL // 3 + 255) // 256) * 256
RMS_EPS = 1e-6
IN_WIDTHS = (HGRN_WIDTH, HGRN_WIDTH, HGRN_WIDTH, HGRN_WIDTH,
             DIFF_WIDTH, DIFF_WIDTH, DIFF_WIDTH,
             RET_QK_WIDTH, RET_QK_WIDTH, RET_V_WIDTH, RET_V_WIDTH,
             GATE_RANK)
IN_TOTAL = sum(IN_WIDTHS)

kernel_name = "hybrid_hgrn2_diffattn_retention_block"

F32 = jnp.float32


def rms_norm(x, w):
    xf = x.astype(F32)
    y = xf * lax.rsqrt(jnp.mean(xf * xf, axis=-1, keepdims=True) + RMS_EPS)
    return (y * w.astype(F32)).astype(x.dtype)


def t5_bucket(n):
    n = jnp.maximum(n, 0)
    max_exact = N_BUCKETS // 2
    nf = jnp.maximum(n, 1).astype(F32)
    large = max_exact + (jnp.log(nf / max_exact) / math.log(MAX_DISTANCE / max_exact)
                         * (N_BUCKETS - max_exact)).astype(jnp.int32)
    large = jnp.minimum(large, N_BUCKETS - 1)
    return jnp.where(n < max_exact, n, large)


def hgrn2_mixer(q, f, i, g, lb, norm_w):
    B, S, _ = q.shape
    H, C, dk, dv = HGRN_HEADS, HGRN_CHUNK, HGRN_KDIM, HGRN_VDIM
    N = S // C
    lbf = lb.astype(F32)
    qf = jax.nn.silu(q.astype(F32)) * dk ** -0.5
    forget = lbf + (1.0 - lbf) * jax.nn.sigmoid(f.astype(F32))
    kf = 1.0 - forget
    logf = jnp.log(forget)

    def to_chunks(t, d):
        return t.reshape(B, N, C, H, d).transpose(1, 0, 3, 2, 4)

    qc, kc, lfc = to_chunks(qf, dk), to_chunks(kf, dk), to_chunks(logf, dk)
    ic = to_chunks(i.astype(F32), dv)
    causal = jnp.tril(jnp.ones((C, C), bool))[None, None, :, :, None]

    def step(state, inp):
        qb, kb, ib, lf = inp
        b = jnp.cumsum(lf, axis=2)
        pair = b[:, :, :, None, :] - b[:, :, None, :, :]
        decay = jnp.exp(jnp.where(causal, pair, -jnp.inf))
        attn = jnp.einsum('bhtd,bhsd,bhtsd->bhts', qb, kb, decay)
        o = jnp.einsum('bhts,bhse->bhte', attn, ib) + \
            jnp.einsum('bhtd,bhde->bhte', qb * jnp.exp(b), state)
        b_last = b[:, :, -1:, :]
        new_state = jnp.exp(b_last[:, :, 0, :, None]) * state + \
            jnp.einsum('bhsd,bhse->bhde', kb * jnp.exp(b_last - b), ib)
        return new_state, o

    state0 = jnp.zeros((B, H, dk, dv), F32)
    _, o = lax.scan(step, state0, (qc, kc, ic, lfc))
    o = o.transpose(1, 0, 3, 2, 4).reshape(B, S, H, dv)
    o = rms_norm(o, norm_w).reshape(B, S, H * dv) * jax.nn.silu(g.astype(F32))
    return o.astype(q.dtype)


def diff_attention(q, k, v, rel_bias, lam_params, lam_init, norm_w):
    B, S, _ = q.shape
    H, dh = DIFF_HEADS, DIFF_HEAD_DIM
    qf = q.reshape(B, S, H, 2, dh).astype(F32) * dh ** -0.5
    kf = k.reshape(B, S, H, 2, dh).astype(F32)
    vf = v.reshape(B, S, H, 2 * dh).astype(F32)
    lp = lam_params.astype(F32)
    lam = jnp.exp(jnp.sum(lp[0] * lp[1])) - jnp.exp(jnp.sum(lp[2] * lp[3])) + lam_init
    key_pos = jnp.arange(S)
    table = rel_bias.astype(F32)

    def block(qi):
        qb = lax.dynamic_slice_in_dim(qf, qi * Q_BLOCK, Q_BLOCK, axis=1)
        rel = (qi * Q_BLOCK + jnp.arange(Q_BLOCK))[:, None] - key_pos[None, :]
        bias = jnp.transpose(table[t5_bucket(rel)], (2, 0, 1))
        logits = jnp.einsum('bqhmd,bkhmd->bmhqk', qb, kf) + bias[None, None]
        logits = jnp.where(rel >= 0, logits, -jnp.inf)
        p = jax.nn.softmax(logits, axis=-1)
        w = p[:, 0] - lam * p[:, 1]
        return jnp.einsum('bhqk,bkhe->bqhe', w, vf)

    out = lax.map(block, jnp.arange(S // Q_BLOCK))
    out = out.transpose(1, 0, 2, 3, 4).reshape(B, S, H, 2 * dh)
    out = rms_norm(out, norm_w) * (1.0 - lam_init)
    return out.reshape(B, S, H * 2 * dh).astype(q.dtype)


def rotary(x):
    S, d = x.shape[1], x.shape[-1]
    half = d // 2
    inv = 1.0 / (ROPE_BASE ** (jnp.arange(half, dtype=F32) / half))
    ang = jnp.arange(S, dtype=F32)[:, None] * inv[None, :]
    cos, sin = jnp.cos(ang)[None, :, None, :], jnp.sin(ang)[None, :, None, :]
    x1, x2 = x[..., :half], x[..., half:]
    return jnp.concatenate([x1 * cos - x2 * sin, x1 * sin + x2 * cos], axis=-1)


def retention(q, k, v, g, norm_w):
    B, S, _ = q.shape
    H, C, dk, dv = RET_HEADS, RET_CHUNK, RET_KDIM, RET_VDIM
    N = S // C
    qf = rotary(q.reshape(B, S, H, dk).astype(F32))
    kf = rotary(k.reshape(B, S, H, dk).astype(F32)) * dk ** -0.5
    vf = v.reshape(B, S, H, dv).astype(F32)
    log_gamma = jnp.log(1.0 - 2.0 ** (-5.0 - jnp.arange(H, dtype=F32)))
    qc, kc = qf.reshape(B, N, C, H, dk), kf.reshape(B, N, C, H, dk)
    vc = vf.reshape(B, N, C, H, dv)
    idx = jnp.arange(C, dtype=F32)
    diff = idx[:, None] - idx[None, :]
    tril = diff >= 0
    D = jnp.exp(jnp.where(tril[None], diff[None] * log_gamma[:, None, None], -jnp.inf))
    scores = jnp.einsum('bnthd,bnshd->bnhts', qc, kc) * D[None, None]
    o_intra = jnp.einsum('bnhts,bnshe->bnthe', scores, vc)
    zeta = jnp.exp((C - 1 - idx)[:, None] * log_gamma[None, :])
    U = jnp.einsum('bnshd,sh,bnshe->nbhde', kc, zeta, vc)
    gamma_c = jnp.exp(C * log_gamma)[None, :, None, None]

    def step(R, U_n):
        return gamma_c * R + U_n, R

    _, R_prev = lax.scan(step, jnp.zeros((B, H, dk, dv), F32), U)
    xi = jnp.exp((idx + 1)[:, None] * log_gamma[None, :])
    o_inter = jnp.einsum('bnthd,nbhde,th->bnthe', qc, R_prev, xi)
    o = (o_intra + o_inter).reshape(B, S, H, dv)
    o = rms_norm(o, norm_w).reshape(B, S, H * dv) * jax.nn.silu(g.astype(F32))
    return o.astype(q.dtype)


def setup_inputs(seed: int = 0) -> dict:
    key = jax.random.key(seed)
    ks = jax.random.split(key, 24)

    def nrm(k, shape, fan_in):
        return jax.random.normal(k, shape, F32) * fan_in ** -0.5

    def gain(k, shape):
        return 1.0 + 0.02 * jax.random.normal(k, shape, F32)

    return {
        "x": jax.random.normal(ks[0], (BATCH, SEQ, D_MODEL), F32),
        "attn_norm_w": gain(ks[1], (DEPTH, D_MODEL)),
        "w_in": nrm(ks[2], (DEPTH, D_MODEL, IN_TOTAL), D_MODEL),
        "lb_logits": 0.5 * jax.random.normal(ks[3], (DEPTH, HGRN_WIDTH), F32),
        "hgrn_norm_w": gain(ks[4], (DEPTH, HGRN_VDIM)),
        "rel_bias": 0.5 * jax.random.normal(ks[5], (N_BUCKETS, DIFF_HEADS), F32),
        "diff_lambda": 0.1 * jax.random.normal(ks[6], (DEPTH, 4, DIFF_HEAD_DIM), F32),
        "diff_norm_w": gain(ks[7], (DEPTH, 2 * DIFF_HEAD_DIM)),
        "ret_norm_w": gain(ks[8], (DEPTH, RET_VDIM)),
        "w_gate_up": nrm(ks[9], (DEPTH, GATE_RANK, N_BRANCH * D_MODEL), GATE_RANK),
        "b_gate": 0.01 * jax.random.normal(ks[10], (DEPTH, N_BRANCH * D_MODEL), F32),
        "w_br_hgrn": nrm(ks[11], (DEPTH, HGRN_WIDTH, D_MODEL), HGRN_WIDTH),
        "w_br_diff": nrm(ks[12], (DEPTH, DIFF_WIDTH, D_MODEL), DIFF_WIDTH),
        "w_br_ret": nrm(ks[13], (DEPTH, RET_V_WIDTH, D_MODEL), RET_V_WIDTH),
        "w_o": nrm(ks[14], (DEPTH, D_MODEL, D_MODEL), D_MODEL),
        "ffn_norm_w": gain(ks[15], (DEPTH, D_MODEL)),
        "w_ffn_gate": nrm(ks[16], (DEPTH, D_MODEL, FFN_HIDDEN), D_MODEL),
        "w_ffn_up": nrm(ks[17], (DEPTH, D_MODEL, FFN_HIDDEN), D_MODEL),
        "w_ffn_down": nrm(ks[18], (DEPTH, FFN_HIDDEN, D_MODEL), FFN_HIDDEN),
        "final_norm_w": gain(ks[19], (D_MODEL,)),
    }


def reference(x, attn_norm_w, w_in, lb_logits, hgrn_norm_w, rel_bias, diff_lambda, diff_norm_w,
              ret_norm_w, w_gate_up, b_gate, w_br_hgrn, w_br_diff, w_br_ret, w_o, ffn_norm_w,
              w_ffn_gate, w_ffn_up, w_ffn_down, final_norm_w):
    dt = x.dtype
    split_points = [int(p) for p in np.cumsum(IN_WIDTHS)[:-1]]
    lb_sm = jax.nn.softmax(lb_logits.astype(F32), axis=0)
    lower_bounds = jnp.cumsum(lb_sm, axis=0) - lb_sm[0:1]
    for l in range(DEPTH):
        h = rms_norm(x, attn_norm_w[l])
        proj = h @ w_in[l]
        hq, hf, hi, hg, dq, dk, dv, rq, rk, rv, rg, gd = jnp.split(proj, split_points, axis=-1)
        y_h = hgrn2_mixer(hq, hf, hi, hg, lower_bounds[l], hgrn_norm_w[l])
        lam_init = 0.8 - 0.6 * math.exp(-0.3 * l)
        y_d = diff_attention(dq, dk, dv, rel_bias, diff_lambda[l], lam_init, diff_norm_w[l])
        y_r = retention(rq, rk, rv, rg, ret_norm_w[l])
        gates = jax.nn.sigmoid((gd @ w_gate_up[l] + b_gate[l]).astype(F32)).astype(dt)
        g_h, g_d, g_r = jnp.split(gates, N_BRANCH, axis=-1)
        merged = g_h * (y_h @ w_br_hgrn[l]) + g_d * (y_d @ w_br_diff[l]) + g_r * (y_r @ w_br_ret[l])
        x = x + merged @ w_o[l]
        h = rms_norm(x, ffn_norm_w[l])
        x = x + (jax.nn.silu(h @ w_ffn_gate[l]) * (h @ w_ffn_up[l])) @ w_ffn_down[l]
    return rms_norm(x, final_norm_w)
```

```python
import functools
import math

import jax
import jax.numpy as jnp
from jax import lax
from jax.experimental import pallas as pl
from jax.experimental.pallas import tpu as pltpu

F32 = jnp.float32
BF16 = jnp.bfloat16

HGRN_HEADS = 8
HGRN_DIM = 128
DIFF_HEADS = 8
DIFF_HEAD_DIM = 64
RET_HEADS = 8
RET_KDIM = 128
RET_VDIM = 256
N_BUCKETS = 32
MAX_DISTANCE = 128
ROPE_BASE = 10000.0
GATE_RANK = 256
RMS_EPS = 1e-6
HGRN_WIDTH = HGRN_HEADS * HGRN_DIM
DIFF_WIDTH = DIFF_HEADS * 2 * DIFF_HEAD_DIM
RET_QK_WIDTH = RET_HEADS * RET_KDIM
RET_V_WIDTH = RET_HEADS * RET_VDIM

V7X_VMEM_BYTES = 64 * 1024 * 1024
VMEM_LIMIT = V7X_VMEM_BYTES - 8 * 1024 * 1024
LANES = 128

NEG = -1e30
HGRN_CHUNK = 128
HGRN_SUB = 16


def _params(*sem):
    return pltpu.CompilerParams(dimension_semantics=sem, vmem_limit_bytes=VMEM_LIMIT)


def _sigmoid(x):
    return 1.0 / (1.0 + jnp.exp(-x))


def _dot(a, b):
    return jnp.dot(a, b, preferred_element_type=F32)


def _dot_nt(a, b):
    return lax.dot_general(a, b, (((1,), (1,)), ((), ())), preferred_element_type=F32)


def _rms_kernel(x_ref, w_ref, o_ref):
    x = x_ref[...]
    ms = jnp.mean(x * x, axis=-1, keepdims=True)
    o_ref[...] = (x * lax.rsqrt(ms + RMS_EPS) * w_ref[...]).astype(o_ref.dtype)


def rms_norm_rows(x, w, out_dtype, tm=256):
    M, D = x.shape
    tm = min(tm, M)
    return pl.pallas_call(
        _rms_kernel,
        grid=(M // tm,),
        in_specs=[pl.BlockSpec((tm, D), lambda i: (i, 0)),
                  pl.BlockSpec((1, D), lambda i: (0, 0))],
        out_specs=pl.BlockSpec((tm, D), lambda i: (i, 0)),
        out_shape=jax.ShapeDtypeStruct((M, D), out_dtype),
        compiler_params=_params("parallel"),
        name="rmsnorm",
    )(x, w.reshape(1, D))


def _mm_kernel(a_ref, b_ref, o_ref):
    o_ref[...] = _dot(a_ref[...], b_ref[...]).astype(o_ref.dtype)


def matmul_cols(a, b, col_off, n, out_dtype, tm=1024, tn=1024):
    M, K = a.shape
    tm, tn = min(tm, M), min(tn, n)
    assert col_off % tn == 0 and n % tn == 0 and M % tm == 0
    off = col_off // tn
    return pl.pallas_call(
        _mm_kernel,
        grid=(M // tm, n // tn),
        in_specs=[pl.BlockSpec((tm, K), lambda i, j: (i, 0)),
                  pl.BlockSpec((K, tn), lambda i, j: (0, j + off))],
        out_specs=pl.BlockSpec((tm, tn), lambda i, j: (i, j)),
        out_shape=jax.ShapeDtypeStruct((M, n), out_dtype),
        compiler_params=_params("parallel", "arbitrary"),
        name="matmul_cols",
    )(a, b)


def _mm_res_kernel(a_ref, b_ref, x_ref, o_ref):
    o_ref[...] = x_ref[...] + _dot(a_ref[...], b_ref[...])


def matmul_residual(a, b, x, tm=1024, tn=512, a_buffers=2):
    M, K = a.shape
    N = b.shape[1]
    tm, tn = min(tm, M), min(tn, N)
    assert M % tm == 0 and N % tn == 0
    return pl.pallas_call(
        _mm_res_kernel,
        grid=(M // tm, N // tn),
        in_specs=[pl.BlockSpec((tm, K), lambda i, j: (i, 0), pipeline_mode=pl.Buffered(a_buffers)),
                  pl.BlockSpec((K, tn), lambda i, j: (0, j)),
                  pl.BlockSpec((tm, tn), lambda i, j: (i, j))],
        out_specs=pl.BlockSpec((tm, tn), lambda i, j: (i, j)),
        out_shape=jax.ShapeDtypeStruct((M, N), F32),
        input_output_aliases={2: 0},
        compiler_params=_params("parallel", "arbitrary"),
        name="matmul_residual",
    )(a, b, x)


def _swiglu_kernel(a_ref, wg_ref, wu_ref, o_ref):
    a = a_ref[...]
    g = _dot(a, wg_ref[...])
    u = _dot(a, wu_ref[...])
    o_ref[...] = (g * _sigmoid(g) * u).astype(o_ref.dtype)


def swiglu_up(a, wg, wu, tm=1024, tn=256):
    M, K = a.shape
    N = wg.shape[1]
    tm, tn = min(tm, M), min(tn, N)
    assert M % tm == 0 and N % tn == 0
    return pl.pallas_call(
        _swiglu_kernel,
        grid=(M // tm, N // tn),
        in_specs=[pl.BlockSpec((tm, K), lambda i, j: (i, 0)),
                  pl.BlockSpec((K, tn), lambda i, j: (0, j)),
                  pl.BlockSpec((K, tn), lambda i, j: (0, j))],
        out_specs=pl.BlockSpec((tm, tn), lambda i, j: (i, j)),
        out_shape=jax.ShapeDtypeStruct((M, N), BF16),
        compiler_params=_params("parallel", "arbitrary"),
        name="swiglu_up",
    )(a, wg, wu)


def _merge_kernel(yh_ref, yd_ref, yr_ref, gd_ref, wh_ref, wd_ref, wr_ref,
                  wgh_ref, wgd_ref, wgr_ref, bh_ref, bd_ref, br_ref, o_ref):
    gd = gd_ref[...]

    def branch(y_ref, w_ref, wg_ref, b_ref):
        gate = _sigmoid(_dot(gd, wg_ref[...]) + b_ref[...])
        return gate * _dot(y_ref[...], w_ref[...])

    acc = branch(yh_ref, wh_ref, wgh_ref, bh_ref)
    acc = acc + branch(yd_ref, wd_ref, wgd_ref, bd_ref)
    acc = acc + branch(yr_ref, wr_ref, wgr_ref, br_ref)
    o_ref[...] = acc.astype(o_ref.dtype)


def gated_merge(yh, yd, yr, gd, wh, wd, wr, wg, bg, tm=1024, tn=512):
    M = yh.shape[0]
    D = wh.shape[1]
    tm, tn = min(tm, M), min(tn, D)
    assert M % tm == 0 and D % tn == 0
    nb = D // tn
    row = lambda width: pl.BlockSpec((tm, width), lambda i, j: (i, 0))
    wcol = lambda k: pl.BlockSpec((k, tn), lambda i, j: (0, j))
    gcol = lambda br: pl.BlockSpec((GATE_RANK, tn), lambda i, j: (0, j + br * nb))
    bcol = lambda br: pl.BlockSpec((1, tn), lambda i, j: (0, j + br * nb))
    bg2 = bg.reshape(1, -1)
    return pl.pallas_call(
        _merge_kernel,
        grid=(M // tm, nb),
        in_specs=[row(yh.shape[1]), row(yd.shape[1]), row(yr.shape[1]), row(gd.shape[1]),
                  wcol(wh.shape[0]), wcol(wd.shape[0]), wcol(wr.shape[0]),
                  gcol(0), gcol(1), gcol(2), bcol(0), bcol(1), bcol(2)],
        out_specs=pl.BlockSpec((tm, tn), lambda i, j: (i, j)),
        out_shape=jax.ShapeDtypeStruct((M, D), BF16),
        compiler_params=_params("parallel", "arbitrary"),
        name="gated_merge",
    )(yh, yd, yr, gd, wh, wd, wr, wg, wg, wg, bg2, bg2, bg2)


def _rope_table_kernel(inv_ref, cos_ref, sin_ref):
    T = cos_ref.shape[0]
    pos = (lax.broadcasted_iota(jnp.int32, (T, LANES), 0) + pl.program_id(0) * T).astype(F32)
    ang = pos * inv_ref[...]
    lane = lax.broadcasted_iota(jnp.int32, (T, LANES), 1)
    cos_ref[...] = jnp.cos(ang)
    sin_ref[...] = jnp.where(lane < LANES // 2, -jnp.sin(ang), jnp.sin(ang))


def rope_tables(S, T=256):
    T = min(T, S)
    half = RET_KDIM // 2
    inv = 1.0 / (ROPE_BASE ** (jnp.arange(half, dtype=F32) / half))
    inv2 = jnp.concatenate([inv, inv]).reshape(1, LANES)
    return pl.pallas_call(
        _rope_table_kernel,
        grid=(S // T,),
        in_specs=[pl.BlockSpec((1, LANES), lambda i: (0, 0))],
        out_specs=[pl.BlockSpec((T, LANES), lambda i: (i, 0))] * 2,
        out_shape=[jax.ShapeDtypeStruct((S, LANES), F32)] * 2,
        compiler_params=_params("parallel"),
        name="rope_tables",
    )(inv2)


def _bias_tile_kernel(tab_ref, o_ref):
    h = pl.program_id(0)
    T = o_ref.shape[2]
    i = lax.broadcasted_iota(jnp.int32, (T, T), 0)
    j = lax.broadcasted_iota(jnp.int32, (T, T), 1)
    max_exact = N_BUCKETS // 2
    for d in range(2):
        n = jnp.maximum(i - j + d * T, 0)
        nf = jnp.maximum(n, 1).astype(F32)
        large = max_exact + (jnp.log(nf / max_exact) / math.log(MAX_DISTANCE / max_exact)
                             * (N_BUCKETS - max_exact)).astype(jnp.int32)
        large = jnp.minimum(large, N_BUCKETS - 1)
        bucket = jnp.where(n < max_exact, n, large)
        val = jnp.zeros((T, T), F32)
        for b in range(N_BUCKETS):
            val = jnp.where(bucket == b, tab_ref[h, b], val)
        if d == 0:
            val = jnp.where(j > i, NEG, val)
        o_ref[0, d] = val


def bias_tiles(rel_bias, T):
    H = rel_bias.shape[1]
    return pl.pallas_call(
        _bias_tile_kernel,
        grid=(H,),
        in_specs=[pl.BlockSpec(memory_space=pltpu.SMEM)],
        out_specs=pl.BlockSpec((1, 2, T, T), lambda h: (h, 0, 0, 0)),
        out_shape=jax.ShapeDtypeStruct((H, 2, T, T), F32),
        compiler_params=_params("parallel"),
        name="bias_tiles",
    )(rel_bias.T)


def _hgrn_kernel(lbl_ref, nw_ref, q_ref, f_ref, i_ref, g_ref, o_ref, st_ref, *, layer):
    T = q_ref.shape[0]
    C, SUB = HGRN_CHUNK, HGRN_SUB

    @pl.when(pl.program_id(2) == 0)
    def _():
        st_ref[...] = jnp.zeros_like(st_ref)

    lg = lbl_ref[...]
    e = jnp.exp(lg - jnp.max(lg, axis=0, keepdims=True))
    sm = e / jnp.sum(e, axis=0, keepdims=True)
    csum = sm[0:1]
    for r in range(1, layer + 1):
        csum = csum + sm[r:r + 1]
    lb = csum - sm[0:1]

    forget = lb + (1.0 - lb) * _sigmoid(f_ref[...])
    kf = 1.0 - forget
    logf = jnp.log(forget)
    q = q_ref[...]
    qf = q * _sigmoid(q) * (HGRN_DIM ** -0.5)

    r_i = lax.broadcasted_iota(jnp.int32, (T, T), 0)
    c_i = lax.broadcasted_iota(jnp.int32, (T, T), 1)
    shift = C.bit_length() - 1
    same_chunk = (r_i >> shift) == (c_i >> shift)
    tri = jnp.where(c_i <= r_i, jnp.where(same_chunk, 1.0, 0.0), 0.0).astype(BF16)
    hi = logf.astype(BF16)
    rem = logf - hi.astype(F32)
    mid = rem.astype(BF16)
    lo = (rem - mid.astype(F32)).astype(BF16)
    bcum = _dot(tri, hi) + _dot(tri, mid) + _dot(tri, lo)

    iv_all = i_ref[...]
    g_all = g_ref[...]
    nw = nw_ref[...]
    ones = jnp.ones((HGRN_DIM, HGRN_DIM), BF16)
    row_c = lax.broadcasted_iota(jnp.int32, (C, HGRN_DIM), 0)
    row_s = lax.broadcasted_iota(jnp.int32, (SUB, HGRN_DIM), 0)

    state_t = st_ref[...]
    for c in range(T // C):
        rows = slice(c * C, (c + 1) * C)
        b, qc, kc, iv = bcum[rows], qf[rows], kf[rows], iv_all[rows]
        iv_bf = iv.astype(BF16)
        o_state = _dot_nt((qc * jnp.exp(b)).astype(BF16), state_t.astype(BF16))
        parts = []
        for sb in range(C // SUB):
            lo_r = sb * SUB
            sub = slice(lo_r, lo_r + SUB)
            bi, qi = b[sub], qc[sub]
            oi = o_state[sub]
            if sb > 0:
                ref = b[lo_r - 1:lo_r]
                qt = (qi * jnp.exp(bi - ref)).astype(BF16)
                kt = (kc * jnp.exp(jnp.where(row_c < lo_r, ref - b, NEG))).astype(BF16)
                att = _dot_nt(qt, kt)
                oi = oi + _dot(att.astype(BF16), iv_bf)
            ps = []
            for s in range(SUB):
                dec = jnp.exp(jnp.where(row_s >= s, bi - bi[s:s + 1], NEG))
                ps.append((qi * kc[lo_r + s:lo_r + s + 1] * dec).astype(BF16))
            rsum = _dot(jnp.concatenate(ps, axis=0), ones)
            for s in range(SUB):
                oi = oi + rsum[s * SUB:(s + 1) * SUB] * iv[lo_r + s:lo_r + s + 1]
            parts.append(oi)
        o = jnp.concatenate(parts, axis=0)
        b_last = b[C - 1:C]
        khat = (kc * jnp.exp(b_last - b)).astype(BF16)
        state_t = state_t * jnp.exp(b_last) + _dot(iv.T.astype(BF16), khat)
        ms = jnp.mean(o * o, axis=-1, keepdims=True)
        gc = g_all[rows]
        y = o * lax.rsqrt(ms + RMS_EPS) * nw * (gc * _sigmoid(gc))
        o_ref[rows, :] = y.astype(o_ref.dtype)
    st_ref[...] = state_t


def hgrn_mixer(p_h, lb_logits, norm_w, layer, B, S, T=256):
    T = min(T, S)
    nt = S // T
    H, dk = HGRN_HEADS, HGRN_DIM
    col = lambda grp: pl.BlockSpec((T, dk), lambda b, h, t: (b * nt + t, grp * H + h))
    return pl.pallas_call(
        functools.partial(_hgrn_kernel, layer=layer),
        grid=(B, H, nt),
        in_specs=[pl.BlockSpec((lb_logits.shape[0], dk), lambda b, h, t: (0, h)),
                  pl.BlockSpec((1, dk), lambda b, h, t: (0, 0)),
                  col(0), col(1), col(2), col(3)],
        out_specs=pl.BlockSpec((T, dk), lambda b, h, t: (b * nt + t, h)),
        out_shape=jax.ShapeDtypeStruct((B * S, H * dk), BF16),
        scratch_shapes=[pltpu.VMEM((dk, dk), F32)],
        compiler_params=_params("parallel", "parallel", "arbitrary"),
        name="hgrn_mixer",
    )(lb_logits, norm_w.reshape(1, dk), p_h, p_h, p_h, p_h)


def _diff_kernel(lam_ref, nw_ref, q_ref, k_ref, v_ref, bias_ref, o_ref, *, lam_init):
    T = q_ref.shape[0]
    dh = DIFF_HEAD_DIM
    qi = pl.program_id(2)

    q = q_ref[...]
    lane = lax.broadcasted_iota(jnp.int32, q.shape, 1)
    zero = jnp.zeros_like(q)
    scale = jnp.asarray(dh ** -0.5, q.dtype)
    qa = jnp.where(lane < dh, q, zero) * scale
    qb = jnp.where(lane >= dh, q, zero) * scale

    def one_map(qm, k, v, bias, m, l, acc):
        s = _dot_nt(qm, k) + bias
        m_new = jnp.maximum(m, jnp.max(s, axis=-1, keepdims=True))
        alpha = jnp.exp(m - m_new)
        p = jnp.exp(s - m_new)
        l_new = alpha * l + jnp.sum(p, axis=-1, keepdims=True)
        acc_new = alpha * acc + _dot(p.astype(v.dtype), v)
        return m_new, l_new, acc_new

    def tile(kt, carry, bias):
        start = pl.multiple_of(kt * T, T)
        k = k_ref[pl.ds(start, T), :]
        v = v_ref[pl.ds(start, T), :]
        m1, l1, a1, m2, l2, a2 = carry
        m1, l1, a1 = one_map(qa, k, v, bias, m1, l1, a1)
        m2, l2, a2 = one_map(qb, k, v, bias, m2, l2, a2)
        return m1, l1, a1, m2, l2, a2

    m0 = jnp.full((T, 1), NEG, F32)
    l0 = jnp.zeros((T, 1), F32)
    a0 = jnp.zeros((T, 2 * dh), F32)
    carry = (m0, l0, a0, m0, l0, a0)

    far = bias_ref[0, 1, T - 1:T, 0:1]
    carry = lax.fori_loop(0, jnp.maximum(qi - 1, 0),
                          lambda kt, c: tile(kt, c, far), carry)
    carry = lax.cond(qi >= 1, lambda c: tile(qi - 1, c, bias_ref[0, 1]), lambda c: c, carry)
    m1, l1, a1, m2, l2, a2 = tile(qi, carry, bias_ref[0, 0])

    lp = lam_ref[...]
    lam = (jnp.exp(jnp.sum(lp[0:1] * lp[1:2], axis=-1, keepdims=True))
           - jnp.exp(jnp.sum(lp[2:3] * lp[3:4], axis=-1, keepdims=True)) + lam_init)
    out = a1 / l1 - lam * (a2 / l2)
    ms = jnp.mean(out * out, axis=-1, keepdims=True)
    y = out * lax.rsqrt(ms + RMS_EPS) * nw_ref[...] * (1.0 - lam_init)
    o_ref[...] = y.astype(o_ref.dtype)


def diff_attention(p_d, bias, lam_params, norm_w, lam_init, B, S):
    T = bias.shape[2]
    assert T >= MAX_DISTANCE and S % T == 0
    nq = S // T
    H, hw = DIFF_HEADS, 2 * DIFF_HEAD_DIM
    return pl.pallas_call(
        functools.partial(_diff_kernel, lam_init=lam_init),
        grid=(B, H, nq),
        in_specs=[pl.BlockSpec(lam_params.shape, lambda b, h, i: (0, 0)),
                  pl.BlockSpec((1, hw), lambda b, h, i: (0, 0)),
                  pl.BlockSpec((T, hw), lambda b, h, i: (b * nq + i, h)),
                  pl.BlockSpec((S, hw), lambda b, h, i: (b, H + h)),
                  pl.BlockSpec((S, hw), lambda b, h, i: (b, 2 * H + h)),
                  pl.BlockSpec((1, 2, T, T), lambda b, h, i: (h, 0, 0, 0))],
        out_specs=pl.BlockSpec((T, hw), lambda b, h, i: (b * nq + i, h)),
        out_shape=jax.ShapeDtypeStruct((B * S, H * hw), BF16),
        compiler_params=_params("parallel", "parallel", "arbitrary"),
        name="diff_attention",
    )(lam_params, norm_w.reshape(1, hw), p_d, p_d, p_d, bias)


def _ret_kernel(lg_ref, nw_ref, cos_ref, sin_ref, q_ref, k_ref, v_ref, g_ref, o_ref, st_ref):
    C = q_ref.shape[0]
    dk = RET_KDIM

    @pl.when(pl.program_id(2) == 0)
    def _():
        st_ref[...] = jnp.zeros_like(st_ref)

    lg = lg_ref[0]
    lg1 = lg[:, 0:1]
    cosf, sinf = cos_ref[...], sin_ref[...]

    def rot(x):
        return x * cosf + pltpu.roll(x, dk // 2, 1) * sinf

    qr = rot(q_ref[...])
    kr = rot(k_ref[...]) * (dk ** -0.5)
    rowf = lax.broadcasted_iota(jnp.int32, (C, dk), 0).astype(F32)
    xi = jnp.exp((rowf + 1.0) * lg)
    zeta = jnp.exp((C - 1.0 - rowf) * lg)
    r_i = lax.broadcasted_iota(jnp.int32, (C, C), 0)
    c_i = lax.broadcasted_iota(jnp.int32, (C, C), 1)
    decay = jnp.exp(jnp.where(r_i >= c_i, (r_i - c_i).astype(F32) * lg1, NEG))

    v = v_ref[...]
    state = st_ref[...]
    scores = _dot_nt(qr.astype(BF16), kr.astype(BF16)) * decay
    o = _dot(scores.astype(BF16), v) + _dot((qr * xi).astype(BF16), state.astype(BF16))
    st_ref[...] = jnp.exp(C * lg1) * state + _dot((kr * zeta).T.astype(BF16), v)

    ms = jnp.mean(o * o, axis=-1, keepdims=True)
    g = g_ref[...].astype(F32)
    y = o * lax.rsqrt(ms + RMS_EPS) * nw_ref[...] * (g * _sigmoid(g))
    o_ref[...] = y.astype(o_ref.dtype)


def retention_mixer(p_qk, p_vg, cos_t, sin_t, norm_w, B, S, C=256):
    C = min(C, S)
    nt = S // C
    H, dk, dv = RET_HEADS, RET_KDIM, RET_VDIM
    log_gamma = jnp.log(1.0 - 2.0 ** (-5.0 - jnp.arange(H, dtype=F32)))
    lg = jnp.broadcast_to(log_gamma[:, None, None], (H, 1, LANES))
    return pl.pallas_call(
        _ret_kernel,
        grid=(B, H, nt),
        in_specs=[pl.BlockSpec((1, 1, LANES), lambda b, h, t: (h, 0, 0)),
                  pl.BlockSpec((1, dv), lambda b, h, t: (0, 0)),
                  pl.BlockSpec((C, dk), lambda b, h, t: (t, 0)),
                  pl.BlockSpec((C, dk), lambda b, h, t: (t, 0)),
                  pl.BlockSpec((C, dk), lambda b, h, t: (b * nt + t, h)),
                  pl.BlockSpec((C, dk), lambda b, h, t: (b * nt + t, H + h)),
                  pl.BlockSpec((C, dv), lambda b, h, t: (b * nt + t, h)),
                  pl.BlockSpec((C, dv), lambda b, h, t: (b * nt + t, H + h))],
        out_specs=pl.BlockSpec((C, dv), lambda b, h, t: (b * nt + t, h)),
        out_shape=jax.ShapeDtypeStruct((B * S, H * dv), BF16),
        scratch_shapes=[pltpu.VMEM((dk, dv), F32)],
        compiler_params=_params("parallel", "parallel", "arbitrary"),
        name="retention_mixer",
    )(lg, norm_w.reshape(1, dv), cos_t, sin_t, p_qk, p_qk, p_vg, p_vg)


def kernel(x, attn_norm_w, w_in, lb_logits, hgrn_norm_w, rel_bias, diff_lambda, diff_norm_w,
           ret_norm_w, w_gate_up, b_gate, w_br_hgrn, w_br_diff, w_br_ret, w_o, ffn_norm_w,
           w_ffn_gate, w_ffn_up, w_ffn_down, final_norm_w):
    B, S, D = x.shape
    depth = w_in.shape[0]
    M = B * S
    xs = x.reshape(M, D)

    cos_t, sin_t = rope_tables(S)
    bias = bias_tiles(rel_bias, min(256, S))

    off_d = 4 * HGRN_WIDTH
    off_rqk = off_d + 3 * DIFF_WIDTH
    off_rvg = off_rqk + 2 * RET_QK_WIDTH
    off_gd = off_rvg + 2 * RET_V_WIDTH

    for l in range(depth):
        w_in_l = w_in[l].astype(BF16)
        h = rms_norm_rows(xs, attn_norm_w[l], BF16)
        p_h = matmul_cols(h, w_in_l, 0, 4 * HGRN_WIDTH, F32)
        p_d = matmul_cols(h, w_in_l, off_d, 3 * DIFF_WIDTH, BF16)
        p_rqk = matmul_cols(h, w_in_l, off_rqk, 2 * RET_QK_WIDTH, F32)
        p_rvg = matmul_cols(h, w_in_l, off_rvg, 2 * RET_V_WIDTH, BF16)
        gd = matmul_cols(h, w_in_l, off_gd, GATE_RANK, BF16, tn=GATE_RANK)

        y_h = hgrn_mixer(p_h, lb_logits, hgrn_norm_w[l], l, B, S)
        lam_init = 0.8 - 0.6 * math.exp(-0.3 * l)
        y_d = diff_attention(p_d, bias, diff_lambda[l], diff_norm_w[l], lam_init, B, S)
        y_r = retention_mixer(p_rqk, p_rvg, cos_t, sin_t, ret_norm_w[l], B, S)

        merged = gated_merge(y_h, y_d, y_r, gd,
                             w_br_hgrn[l].astype(BF16), w_br_diff[l].astype(BF16),
                             w_br_ret[l].astype(BF16), w_gate_up[l].astype(BF16), b_gate[l])
        xs = matmul_residual(merged, w_o[l].astype(BF16), xs)

        h2 = rms_norm_rows(xs, ffn_norm_w[l], BF16)
        act = swiglu_up(h2, w_ffn_gate[l].astype(BF16), w_ffn_up[l].astype(BF16))
        xs = matmul_residual(act, w_ffn_down[l].astype(BF16), xs, tn=256, a_buffers=1)

    out = rms_norm_rows(xs, final_norm_w, x.dtype)
    return out.reshape(B, S, D)
```

```python
import functools
import math

import jax
import jax.numpy as jnp
from jax import lax
from jax.experimental import pallas as pl
from jax.experimental.pallas import tpu as pltpu

F32 = jnp.float32
BF16 = jnp.bfloat16

HGRN_HEADS = 8
HGRN_DIM = 128
DIFF_HEADS = 8
DIFF_HEAD_DIM = 64
RET_HEADS = 8
RET_KDIM = 128
RET_VDIM = 256
N_BUCKETS = 32
MAX_DISTANCE = 128
ROPE_BASE = 10000.0
GATE_RANK = 256
RMS_EPS = 1e-6
HGRN_WIDTH = HGRN_HEADS * HGRN_DIM
DIFF_WIDTH = DIFF_HEADS * 2 * DIFF_HEAD_DIM
RET_QK_WIDTH = RET_HEADS * RET_KDIM
RET_V_WIDTH = RET_HEADS * RET_VDIM

V7X_VMEM_BYTES = 64 * 1024 * 1024
VMEM_LIMIT = V7X_VMEM_BYTES - 8 * 1024 * 1024
LANES = 128

NEG = -1e30
HGRN_CHUNK = 128
HGRN_SUB = 16
DIFF_UNROLL = 4


def _params(*sem):
    return pltpu.CompilerParams(dimension_semantics=sem, vmem_limit_bytes=VMEM_LIMIT)


def _sigmoid(x):
    return 1.0 / (1.0 + jnp.exp(-x))


def _dot(a, b):
    return jnp.dot(a, b, preferred_element_type=F32)


def _dot_nt(a, b):
    return lax.dot_general(a, b, (((1,), (1,)), ((), ())), preferred_element_type=F32)


def _rms_kernel(x_ref, w_ref, o_ref):
    x = x_ref[...]
    ms = jnp.mean(x * x, axis=-1, keepdims=True)
    o_ref[...] = (x * lax.rsqrt(ms + RMS_EPS) * w_ref[...]).astype(o_ref.dtype)


def rms_norm_rows(x, w, out_dtype, tm=256):
    M, D = x.shape
    tm = min(tm, M)
    return pl.pallas_call(
        _rms_kernel,
        grid=(M // tm,),
        in_specs=[pl.BlockSpec((tm, D), lambda i: (i, 0)),
                  pl.BlockSpec((1, D), lambda i: (0, 0))],
        out_specs=pl.BlockSpec((tm, D), lambda i: (i, 0)),
        out_shape=jax.ShapeDtypeStruct((M, D), out_dtype),
        compiler_params=_params("parallel"),
        name="rmsnorm",
    )(x, w.reshape(1, D))


def _mm_kernel(a_ref, b_ref, o_ref):
    o_ref[...] = _dot(a_ref[...], b_ref[...]).astype(o_ref.dtype)


def matmul_cols(a, b, col_off, n, out_dtype, tm=1024, tn=1024):
    M, K = a.shape
    tm, tn = min(tm, M), min(tn, n)
    assert col_off % tn == 0 and n % tn == 0 and M % tm == 0
    off = col_off // tn
    return pl.pallas_call(
        _mm_kernel,
        grid=(M // tm, n // tn),
        in_specs=[pl.BlockSpec((tm, K), lambda i, j: (i, 0)),
                  pl.BlockSpec((K, tn), lambda i, j: (0, j + off))],
        out_specs=pl.BlockSpec((tm, tn), lambda i, j: (i, j)),
        out_shape=jax.ShapeDtypeStruct((M, n), out_dtype),
        compiler_params=_params("parallel", "arbitrary"),
        name="matmul_cols",
    )(a, b)


def _mm_res_kernel(a_ref, b_ref, x_ref, o_ref):
    o_ref[...] = x_ref[...] + _dot(a_ref[...], b_ref[...])


def matmul_residual(a, b, x, tm=1024, tn=512, a_buffers=2):
    M, K = a.shape
    N = b.shape[1]
    tm, tn = min(tm, M), min(tn, N)
    assert M % tm == 0 and N % tn == 0
    return pl.pallas_call(
        _mm_res_kernel,
        grid=(M // tm, N // tn),
        in_specs=[pl.BlockSpec((tm, K), lambda i, j: (i, 0), pipeline_mode=pl.Buffered(a_buffers)),
                  pl.BlockSpec((K, tn), lambda i, j: (0, j)),
                  pl.BlockSpec((tm, tn), lambda i, j: (i, j))],
        out_specs=pl.BlockSpec((tm, tn), lambda i, j: (i, j)),
        out_shape=jax.ShapeDtypeStruct((M, N), F32),
        input_output_aliases={2: 0},
        compiler_params=_params("parallel", "arbitrary"),
        name="matmul_residual",
    )(a, b, x)


def _swiglu_kernel(a_ref, wg_ref, wu_ref, o_ref):
    a = a_ref[...]
    g = _dot(a, wg_ref[...])
    u = _dot(a, wu_ref[...])
    o_ref[...] = (g * _sigmoid(g) * u).astype(o_ref.dtype)


def swiglu_up(a, wg, wu, tm=2048, tn=256):
    M, K = a.shape
    N = wg.shape[1]
    tm, tn = min(tm, M), min(tn, N)
    assert M % tm == 0 and N % tn == 0
    return pl.pallas_call(
        _swiglu_kernel,
        grid=(M // tm, N // tn),
        in_specs=[pl.BlockSpec((tm, K), lambda i, j: (i, 0)),
                  pl.BlockSpec((K, tn), lambda i, j: (0, j)),
                  pl.BlockSpec((K, tn), lambda i, j: (0, j))],
        out_specs=pl.BlockSpec((tm, tn), lambda i, j: (i, j)),
        out_shape=jax.ShapeDtypeStruct((M, N), BF16),
        compiler_params=_params("parallel", "arbitrary"),
        name="swiglu_up",
    )(a, wg, wu)


def _merge_kernel(yh_ref, yd_ref, yr_ref, gd_ref, wh_ref, wd_ref, wr_ref,
                  wgh_ref, wgd_ref, wgr_ref, bh_ref, bd_ref, br_ref, o_ref):
    gd = gd_ref[...]

    def branch(y_ref, w_ref, wg_ref, b_ref):
        gate = _sigmoid(_dot(gd, wg_ref[...]) + b_ref[...])
        return gate * _dot(y_ref[...], w_ref[...])

    acc = branch(yh_ref, wh_ref, wgh_ref, bh_ref)
    acc = acc + branch(yd_ref, wd_ref, wgd_ref, bd_ref)
    acc = acc + branch(yr_ref, wr_ref, wgr_ref, br_ref)
    o_ref[...] = acc.astype(o_ref.dtype)


def gated_merge(yh, yd, yr, gd, wh, wd, wr, wg, bg, tm=1024, tn=512):
    M = yh.shape[0]
    D = wh.shape[1]
    tm, tn = min(tm, M), min(tn, D)
    assert M % tm == 0 and D % tn == 0
    nb = D // tn
    row = lambda width: pl.BlockSpec((tm, width), lambda i, j: (i, 0))
    wcol = lambda k: pl.BlockSpec((k, tn), lambda i, j: (0, j))
    gcol = lambda br: pl.BlockSpec((GATE_RANK, tn), lambda i, j: (0, j + br * nb))
    bcol = lambda br: pl.BlockSpec((1, tn), lambda i, j: (0, j + br * nb))
    bg2 = bg.reshape(1, -1)
    return pl.pallas_call(
        _merge_kernel,
        grid=(M // tm, nb),
        in_specs=[row(yh.shape[1]), row(yd.shape[1]), row(yr.shape[1]), row(gd.shape[1]),
                  wcol(wh.shape[0]), wcol(wd.shape[0]), wcol(wr.shape[0]),
                  gcol(0), gcol(1), gcol(2), bcol(0), bcol(1), bcol(2)],
        out_specs=pl.BlockSpec((tm, tn), lambda i, j: (i, j)),
        out_shape=jax.ShapeDtypeStruct((M, D), BF16),
        compiler_params=_params("parallel", "arbitrary"),
        name="gated_merge",
    )(yh, yd, yr, gd, wh, wd, wr, wg, wg, wg, bg2, bg2, bg2)


def _rope_table_kernel(inv_ref, cos_ref, sin_ref):
    T = cos_ref.shape[0]
    pos = (lax.broadcasted_iota(jnp.int32, (T, LANES), 0) + pl.program_id(0) * T).astype(F32)
    ang = pos * inv_ref[...]
    lane = lax.broadcasted_iota(jnp.int32, (T, LANES), 1)
    cos_ref[...] = jnp.cos(ang)
    sin_ref[...] = jnp.where(lane < LANES // 2, -jnp.sin(ang), jnp.sin(ang))


def rope_tables(S, T=256):
    T = min(T, S)
    half = RET_KDIM // 2
    inv = 1.0 / (ROPE_BASE ** (jnp.arange(half, dtype=F32) / half))
    inv2 = jnp.concatenate([inv, inv]).reshape(1, LANES)
    return pl.pallas_call(
        _rope_table_kernel,
        grid=(S // T,),
        in_specs=[pl.BlockSpec((1, LANES), lambda i: (0, 0))],
        out_specs=[pl.BlockSpec((T, LANES), lambda i: (i, 0))] * 2,
        out_shape=[jax.ShapeDtypeStruct((S, LANES), F32)] * 2,
        compiler_params=_params("parallel"),
        name="rope_tables",
    )(inv2)


def _bias_tile_kernel(tab_ref, o_ref):
    h = pl.program_id(0)
    T = o_ref.shape[2]
    j = lax.broadcasted_iota(jnp.int32, (T, T), 0)
    i = lax.broadcasted_iota(jnp.int32, (T, T), 1)
    max_exact = N_BUCKETS // 2
    for d in range(2):
        n = jnp.maximum(i - j + d * T, 0)
        nf = jnp.maximum(n, 1).astype(F32)
        large = max_exact + (jnp.log(nf / max_exact) / math.log(MAX_DISTANCE / max_exact)
                             * (N_BUCKETS - max_exact)).astype(jnp.int32)
        large = jnp.minimum(large, N_BUCKETS - 1)
        bucket = jnp.where(n < max_exact, n, large)
        val = jnp.zeros((T, T), F32)
        for b in range(N_BUCKETS):
            val = jnp.where(bucket == b, tab_ref[h, b], val)
        if d == 0:
            val = jnp.where(j > i, NEG, val)
        o_ref[0, d] = val


def bias_tiles(rel_bias, T):
    H = rel_bias.shape[1]
    return pl.pallas_call(
        _bias_tile_kernel,
        grid=(H,),
        in_specs=[pl.BlockSpec(memory_space=pltpu.SMEM)],
        out_specs=pl.BlockSpec((1, 2, T, T), lambda h: (h, 0, 0, 0)),
        out_shape=jax.ShapeDtypeStruct((H, 2, T, T), F32),
        compiler_params=_params("parallel"),
        name="bias_tiles",
    )(rel_bias.T)


def _hgrn_kernel(lbl_ref, nw_ref, q_ref, f_ref, i_ref, g_ref, o_ref, st_ref, *, layer):
    T = q_ref.shape[0]
    C, SUB = HGRN_CHUNK, HGRN_SUB

    @pl.when(pl.program_id(2) == 0)
    def _():
        st_ref[...] = jnp.zeros_like(st_ref)

    lg = lbl_ref[...]
    e = jnp.exp(lg - jnp.max(lg, axis=0, keepdims=True))
    sm = e / jnp.sum(e, axis=0, keepdims=True)
    csum = sm[0:1]
    for r in range(1, layer + 1):
        csum = csum + sm[r:r + 1]
    lb = csum - sm[0:1]

    forget = lb + (1.0 - lb) * _sigmoid(f_ref[...])
    kf = 1.0 - forget
    logf = jnp.log(forget)
    q = q_ref[...]
    qf = q * _sigmoid(q) * (HGRN_DIM ** -0.5)

    r_i = lax.broadcasted_iota(jnp.int32, (T, T), 0)
    c_i = lax.broadcasted_iota(jnp.int32, (T, T), 1)
    shift = C.bit_length() - 1
    same_chunk = (r_i >> shift) == (c_i >> shift)
    tri = jnp.where(c_i <= r_i, jnp.where(same_chunk, 1.0, 0.0), 0.0).astype(BF16)
    hi = logf.astype(BF16)
    rem = logf - hi.astype(F32)
    mid = rem.astype(BF16)
    lo = (rem - mid.astype(F32)).astype(BF16)
    bcum = _dot(tri, hi) + _dot(tri, mid) + _dot(tri, lo)

    iv_all = i_ref[...]
    g_all = g_ref[...]
    nw = nw_ref[...]
    ones = jnp.ones((HGRN_DIM, HGRN_DIM), BF16)
    row_c = lax.broadcasted_iota(jnp.int32, (C, HGRN_DIM), 0)
    row_s = lax.broadcasted_iota(jnp.int32, (SUB, HGRN_DIM), 0)

    state_t = st_ref[...]
    for c in range(T // C):
        rows = slice(c * C, (c + 1) * C)
        b, qc, kc, iv = bcum[rows], qf[rows], kf[rows], iv_all[rows]
        iv_bf = iv.astype(BF16)
        o_state = _dot_nt((qc * jnp.exp(b)).astype(BF16), state_t.astype(BF16))
        parts = []
        for sb in range(C // SUB):
            lo_r = sb * SUB
            sub = slice(lo_r, lo_r + SUB)
            bi, qi = b[sub], qc[sub]
            oi = o_state[sub]
            if sb > 0:
                ref = b[lo_r - 1:lo_r]
                qt = (qi * jnp.exp(bi - ref)).astype(BF16)
                kt = (kc * jnp.exp(jnp.where(row_c < lo_r, ref - b, NEG))).astype(BF16)
                att = _dot_nt(qt, kt)
                oi = oi + _dot(att.astype(BF16), iv_bf)
            ps = []
            for s in range(SUB):
                dec = jnp.exp(jnp.where(row_s >= s, bi - bi[s:s + 1], NEG))
                ps.append((qi * kc[lo_r + s:lo_r + s + 1] * dec).astype(BF16))
            rsum = _dot(jnp.concatenate(ps, axis=0), ones)
            for s in range(SUB):
                oi = oi + rsum[s * SUB:(s + 1) * SUB] * iv[lo_r + s:lo_r + s + 1]
            parts.append(oi)
        o = jnp.concatenate(parts, axis=0)
        b_last = b[C - 1:C]
        khat = (kc * jnp.exp(b_last - b)).astype(BF16)
        state_t = state_t * jnp.exp(b_last) + _dot(iv.T.astype(BF16), khat)
        ms = jnp.mean(o * o, axis=-1, keepdims=True)
        gc = g_all[rows]
        y = o * lax.rsqrt(ms + RMS_EPS) * nw * (gc * _sigmoid(gc))
        o_ref[rows, :] = y.astype(o_ref.dtype)
    st_ref[...] = state_t


def hgrn_mixer(p_h, lb_logits, norm_w, layer, B, S, T=256):
    T = min(T, S)
    nt = S // T
    H, dk = HGRN_HEADS, HGRN_DIM
    col = lambda grp: pl.BlockSpec((T, dk), lambda b, h, t: (b * nt + t, grp * H + h))
    return pl.pallas_call(
        functools.partial(_hgrn_kernel, layer=layer),
        grid=(B, H, nt),
        in_specs=[pl.BlockSpec((lb_logits.shape[0], dk), lambda b, h, t: (0, h)),
                  pl.BlockSpec((1, dk), lambda b, h, t: (0, 0)),
                  col(0), col(1), col(2), col(3)],
        out_specs=pl.BlockSpec((T, dk), lambda b, h, t: (b * nt + t, h)),
        out_shape=jax.ShapeDtypeStruct((B * S, H * dk), BF16),
        scratch_shapes=[pltpu.VMEM((dk, dk), F32)],
        compiler_params=_params("parallel", "parallel", "arbitrary"),
        name="hgrn_mixer",
    )(lb_logits, norm_w.reshape(1, dk), p_h, p_h, p_h, p_h)


def _diff_kernel(lam_ref, nw_ref, q_ref, k_ref, v_ref, bias_ref, o_ref, vt_ref, s_ref, *, lam_init):
    T = q_ref.shape[0]
    S = k_ref.shape[0]
    dh = DIFF_HEAD_DIM
    qi = pl.program_id(2)

    @pl.when(qi == 0)
    def _():
        def body(c, carry):
            start = pl.multiple_of(c * T, T)
            vt_ref[:, pl.ds(start, T)] = v_ref[pl.ds(start, T), :].astype(F32).T.astype(BF16)
            return carry
        lax.fori_loop(0, S // T, body, 0)

    qt = q_ref[...].astype(F32).T * (dh ** -0.5)
    row = lax.broadcasted_iota(jnp.int32, qt.shape, 0)
    q2 = jnp.concatenate([jnp.where(row < dh, qt, 0.0), jnp.where(row >= dh, qt, 0.0)],
                         axis=1).astype(BF16)

    def rows_of(kt):
        return pl.ds(pl.multiple_of(kt * T, T), T)

    def for_tiles(lo, hi, body, carry):
        groups = (hi - lo) // DIFF_UNROLL

        def group(g, c):
            for u in range(DIFF_UNROLL):
                c = body(lo + g * DIFF_UNROLL + u, c)
            return c
        carry = lax.fori_loop(0, groups, group, carry)
        return lax.fori_loop(lo + groups * DIFF_UNROLL, hi, body, carry)

    def score_tile(kt, m, bias):
        s = _dot(k_ref[rows_of(kt), :], q2)
        if bias is not None:
            s = s + jnp.concatenate([bias, bias], axis=1)
        s_ref[kt] = s
        return jnp.maximum(m, jnp.max(s, axis=0, keepdims=True))

    far = bias_ref[0, 1, 0:1, T - 1:T]
    n_far = jnp.maximum(qi - 1, 0)
    m_init = jnp.full((1, 2 * T), NEG, F32)
    m_far = for_tiles(0, n_far, lambda kt, m: score_tile(kt, m, None), m_init) + far
    m_all = lax.cond(qi >= 1, lambda m: score_tile(qi - 1, m, bias_ref[0, 1]), lambda m: m, m_far)
    m_all = score_tile(qi, m_all, bias_ref[0, 0])

    def prob_tile(kt, carry, m_sub):
        l, acc = carry
        p = jnp.exp(s_ref[kt] - m_sub)
        l = l + jnp.sum(p, axis=0, keepdims=True)
        acc = acc + _dot(vt_ref[:, rows_of(kt)], p.astype(BF16))
        return l, acc

    carry = (jnp.zeros((1, 2 * T), F32), jnp.zeros((2 * dh, 2 * T), F32))
    m_far_sub = m_all - far
    carry = for_tiles(0, n_far, lambda kt, c: prob_tile(kt, c, m_far_sub), carry)
    l, acc = for_tiles(n_far, qi + 1, lambda kt, c: prob_tile(kt, c, m_all), carry)

    lp = lam_ref[...]
    lam = (jnp.exp(jnp.sum(lp[0:1] * lp[1:2], axis=-1, keepdims=True))
           - jnp.exp(jnp.sum(lp[2:3] * lp[3:4], axis=-1, keepdims=True)) + lam_init)
    w = acc / l
    out = (w[:, :T] - lam * w[:, T:]).T
    ms = jnp.mean(out * out, axis=-1, keepdims=True)
    y = out * lax.rsqrt(ms + RMS_EPS) * nw_ref[...] * (1.0 - lam_init)
    o_ref[...] = y.astype(o_ref.dtype)


def diff_attention(p_d, bias, lam_params, norm_w, lam_init, B, S):
    T = bias.shape[2]
    assert T >= MAX_DISTANCE and S % T == 0
    nq = S // T
    H, hw = DIFF_HEADS, 2 * DIFF_HEAD_DIM
    return pl.pallas_call(
        functools.partial(_diff_kernel, lam_init=lam_init),
        grid=(B, H, nq),
        in_specs=[pl.BlockSpec(lam_params.shape, lambda b, h, i: (0, 0)),
                  pl.BlockSpec((1, hw), lambda b, h, i: (0, 0)),
                  pl.BlockSpec((T, hw), lambda b, h, i: (b * nq + i, h)),
                  pl.BlockSpec((S, hw), lambda b, h, i: (b, H + h)),
                  pl.BlockSpec((S, hw), lambda b, h, i: (b, 2 * H + h)),
                  pl.BlockSpec((1, 2, T, T), lambda b, h, i: (h, 0, 0, 0))],
        out_specs=pl.BlockSpec((T, hw), lambda b, h, i: (b * nq + i, h)),
        out_shape=jax.ShapeDtypeStruct((B * S, H * hw), BF16),
        scratch_shapes=[pltpu.VMEM((hw, S), BF16), pltpu.VMEM((nq, T, 2 * T), F32)],
        compiler_params=_params("parallel", "parallel", "arbitrary"),
        name="diff_attention",
    )(lam_params, norm_w.reshape(1, hw), p_d, p_d, p_d, bias)


def _ret_kernel(lg_ref, nw_ref, cos_ref, sin_ref, q_ref, k_ref, v_ref, g_ref, o_ref, st_ref):
    C = q_ref.shape[0]
    dk = RET_KDIM

    @pl.when(pl.program_id(2) == 0)
    def _():
        st_ref[...] = jnp.zeros_like(st_ref)

    lg = lg_ref[0]
    lg1 = lg[:, 0:1]
    cosf, sinf = cos_ref[...], sin_ref[...]

    def rot(x):
        return x * cosf + pltpu.roll(x, dk // 2, 1) * sinf

    qr = rot(q_ref[...])
    kr = rot(k_ref[...]) * (dk ** -0.5)
    rowf = lax.broadcasted_iota(jnp.int32, (C, dk), 0).astype(F32)
    xi = jnp.exp((rowf + 1.0) * lg)
    zeta = jnp.exp((C - 1.0 - rowf) * lg)
    r_i = lax.broadcasted_iota(jnp.int32, (C, C), 0)
    c_i = lax.broadcasted_iota(jnp.int32, (C, C), 1)
    decay = jnp.exp(jnp.where(r_i >= c_i, (r_i - c_i).astype(F32) * lg1, NEG))

    v = v_ref[...]
    state = st_ref[...]
    scores = _dot_nt(qr.astype(BF16), kr.astype(BF16)) * decay
    o = _dot(scores.astype(BF16), v) + _dot((qr * xi).astype(BF16), state.astype(BF16))
    st_ref[...] = jnp.exp(C * lg1) * state + _dot((kr * zeta).T.astype(BF16), v)

    ms = jnp.mean(o * o, axis=-1, keepdims=True)
    g = g_ref[...].astype(F32)
    y = o * lax.rsqrt(ms + RMS_EPS) * nw_ref[...] * (g * _sigmoid(g))
    o_ref[...] = y.astype(o_ref.dtype)


def retention_mixer(p_qk, p_vg, cos_t, sin_t, norm_w, B, S, C=256):
    C = min(C, S)
    nt = S // C
    H, dk, dv = RET_HEADS, RET_KDIM, RET_VDIM
    log_gamma = jnp.log(1.0 - 2.0 ** (-5.0 - jnp.arange(H, dtype=F32)))
    lg = jnp.broadcast_to(log_gamma[:, None, None], (H, 1, LANES))
    return pl.pallas_call(
        _ret_kernel,
        grid=(B, H, nt),
        in_specs=[pl.BlockSpec((1, 1, LANES), lambda b, h, t: (h, 0, 0)),
                  pl.BlockSpec((1, dv), lambda b, h, t: (0, 0)),
                  pl.BlockSpec((C, dk), lambda b, h, t: (t, 0)),
                  pl.BlockSpec((C, dk), lambda b, h, t: (t, 0)),
                  pl.BlockSpec((C, dk), lambda b, h, t: (b * nt + t, h)),
                  pl.BlockSpec((C, dk), lambda b, h, t: (b * nt + t, H + h)),
                  pl.BlockSpec((C, dv), lambda b, h, t: (b * nt + t, h)),
                  pl.BlockSpec((C, dv), lambda b, h, t: (b * nt + t, H + h))],
        out_specs=pl.BlockSpec((C, dv), lambda b, h, t: (b * nt + t, h)),
        out_shape=jax.ShapeDtypeStruct((B * S, H * dv), BF16),
        scratch_shapes=[pltpu.VMEM((dk, dv), F32)],
        compiler_params=_params("parallel", "parallel", "arbitrary"),
        name="retention_mixer",
    )(lg, norm_w.reshape(1, dv), cos_t, sin_t, p_qk, p_qk, p_vg, p_vg)


def kernel(x, attn_norm_w, w_in, lb_logits, hgrn_norm_w, rel_bias, diff_lambda, diff_norm_w,
           ret_norm_w, w_gate_up, b_gate, w_br_hgrn, w_br_diff, w_br_ret, w_o, ffn_norm_w,
           w_ffn_gate, w_ffn_up, w_ffn_down, final_norm_w):
    B, S, D = x.shape
    depth = w_in.shape[0]
    M = B * S
    xs = x.reshape(M, D)

    cos_t, sin_t = rope_tables(S)
    bias = bias_tiles(rel_bias, min(256, S))

    off_d = 4 * HGRN_WIDTH
    off_rqk = off_d + 3 * DIFF_WIDTH
    off_rvg = off_rqk + 2 * RET_QK_WIDTH
    off_gd = off_rvg + 2 * RET_V_WIDTH

    for l in range(depth):
        w_in_l = w_in[l].astype(BF16)
        h = rms_norm_rows(xs, attn_norm_w[l], BF16)
        p_h = matmul_cols(h, w_in_l, 0, 4 * HGRN_WIDTH, F32)
        p_d = matmul_cols(h, w_in_l, off_d, 3 * DIFF_WIDTH, BF16)
        p_rqk = matmul_cols(h, w_in_l, off_rqk, 2 * RET_QK_WIDTH, F32)
        p_rvg = matmul_cols(h, w_in_l, off_rvg, 2 * RET_V_WIDTH, BF16)
        gd = matmul_cols(h, w_in_l, off_gd, GATE_RANK, BF16, tn=GATE_RANK)

        y_h = hgrn_mixer(p_h, lb_logits, hgrn_norm_w[l], l, B, S)
        lam_init = 0.8 - 0.6 * math.exp(-0.3 * l)
        y_d = diff_attention(p_d, bias, diff_lambda[l], diff_norm_w[l], lam_init, B, S)
        y_r = retention_mixer(p_rqk, p_rvg, cos_t, sin_t, ret_norm_w[l], B, S)

        merged = gated_merge(y_h, y_d, y_r, gd,
                             w_br_hgrn[l].astype(BF16), w_br_diff[l].astype(BF16),
                             w_br_ret[l].astype(BF16), w_gate_up[l].astype(BF16), b_gate[l])
        xs = matmul_residual(merged, w_o[l].astype(BF16), xs)

        h2 = rms_norm_rows(xs, ffn_norm_w[l], BF16)
        act = swiglu_up(h2, w_ffn_gate[l].astype(BF16), w_ffn_up[l].astype(BF16))
        xs = matmul_residual(act, w_ffn_down[l].astype(BF16), xs, tn=256, a_buffers=1)

    out = rms_norm_rows(xs, final_norm_w, x.dtype)
    return out.reshape(B, S, D)
```

```python
import functools
import math

import jax
import jax.numpy as jnp
from jax import lax
from jax.experimental import pallas as pl
from jax.experimental.pallas import tpu as pltpu

F32 = jnp.float32
BF16 = jnp.bfloat16

HGRN_HEADS = 8
HGRN_DIM = 128
DIFF_HEADS = 8
DIFF_HEAD_DIM = 64
RET_HEADS = 8
RET_KDIM = 128
RET_VDIM = 256
N_BUCKETS = 32
MAX_DISTANCE = 128
ROPE_BASE = 10000.0
GATE_RANK = 256
RMS_EPS = 1e-6
HGRN_WIDTH = HGRN_HEADS * HGRN_DIM
DIFF_WIDTH = DIFF_HEADS * 2 * DIFF_HEAD_DIM
RET_QK_WIDTH = RET_HEADS * RET_KDIM
RET_V_WIDTH = RET_HEADS * RET_VDIM

V7X_VMEM_BYTES = 64 * 1024 * 1024
VMEM_LIMIT = V7X_VMEM_BYTES - 8 * 1024 * 1024
LANES = 128
BF16_TILE_ROWS = 16
LOG2E = math.log2(math.e)

NEG = -1e30
HGRN_CHUNK = 128
HGRN_SUB = 16
DIFF_GROUP = 4


def _params(*sem):
    return pltpu.CompilerParams(dimension_semantics=sem, vmem_limit_bytes=VMEM_LIMIT)


def _sigmoid(x):
    return 1.0 / (1.0 + jnp.exp(-x))


def _dot(a, b):
    return jnp.dot(a, b, preferred_element_type=F32)


def _dot_nt(a, b):
    return lax.dot_general(a, b, (((1,), (1,)), ((), ())), preferred_element_type=F32)


def _rms_kernel(x_ref, w_ref, o_ref):
    x = x_ref[...]
    ms = jnp.mean(x * x, axis=-1, keepdims=True)
    o_ref[...] = (x * lax.rsqrt(ms + RMS_EPS) * w_ref[...]).astype(o_ref.dtype)


def rms_norm_rows(x, w, out_dtype, tm=256):
    M, D = x.shape
    tm = min(tm, M)
    return pl.pallas_call(
        _rms_kernel,
        grid=(M // tm,),
        in_specs=[pl.BlockSpec((tm, D), lambda i: (i, 0)),
                  pl.BlockSpec((1, D), lambda i: (0, 0))],
        out_specs=pl.BlockSpec((tm, D), lambda i: (i, 0)),
        out_shape=jax.ShapeDtypeStruct((M, D), out_dtype),
        compiler_params=_params("parallel"),
        name="rmsnorm",
    )(x, w.reshape(1, D))


def _mm_kernel(a_ref, b_ref, o_ref):
    o_ref[...] = _dot(a_ref[...], b_ref[...]).astype(o_ref.dtype)


def matmul_cols(a, b, col_off, n, out_dtype, tm=1024, tn=1024):
    M, K = a.shape
    tm, tn = min(tm, M), min(tn, n)
    assert col_off % tn == 0 and n % tn == 0 and M % tm == 0
    off = col_off // tn
    return pl.pallas_call(
        _mm_kernel,
        grid=(M // tm, n // tn),
        in_specs=[pl.BlockSpec((tm, K), lambda i, j: (i, 0)),
                  pl.BlockSpec((K, tn), lambda i, j: (0, j + off))],
        out_specs=pl.BlockSpec((tm, tn), lambda i, j: (i, j)),
        out_shape=jax.ShapeDtypeStruct((M, n), out_dtype),
        compiler_params=_params("parallel", "arbitrary"),
        name="matmul_cols",
    )(a, b)


def _mm_res_kernel(a_ref, b_ref, x_ref, o_ref):
    o_ref[...] = x_ref[...] + _dot(a_ref[...], b_ref[...])


def matmul_residual(a, b, x, tm=1024, tn=512, a_buffers=2):
    M, K = a.shape
    N = b.shape[1]
    tm, tn = min(tm, M), min(tn, N)
    assert M % tm == 0 and N % tn == 0
    return pl.pallas_call(
        _mm_res_kernel,
        grid=(M // tm, N // tn),
        in_specs=[pl.BlockSpec((tm, K), lambda i, j: (i, 0), pipeline_mode=pl.Buffered(a_buffers)),
                  pl.BlockSpec((K, tn), lambda i, j: (0, j)),
                  pl.BlockSpec((tm, tn), lambda i, j: (i, j))],
        out_specs=pl.BlockSpec((tm, tn), lambda i, j: (i, j)),
        out_shape=jax.ShapeDtypeStruct((M, N), F32),
        input_output_aliases={2: 0},
        compiler_params=_params("parallel", "arbitrary"),
        name="matmul_residual",
    )(a, b, x)


def _swiglu_kernel(a_ref, wg_ref, wu_ref, o_ref):
    a = a_ref[...]
    g = _dot(a, wg_ref[...])
    u = _dot(a, wu_ref[...])
    o_ref[...] = (g * _sigmoid(g) * u).astype(o_ref.dtype)


def swiglu_up(a, wg, wu, tm=2048, tn=256):
    M, K = a.shape
    N = wg.shape[1]
    tm, tn = min(tm, M), min(tn, N)
    assert M % tm == 0 and N % tn == 0
    return pl.pallas_call(
        _swiglu_kernel,
        grid=(M // tm, N // tn),
        in_specs=[pl.BlockSpec((tm, K), lambda i, j: (i, 0)),
                  pl.BlockSpec((K, tn), lambda i, j: (0, j)),
                  pl.BlockSpec((K, tn), lambda i, j: (0, j))],
        out_specs=pl.BlockSpec((tm, tn), lambda i, j: (i, j)),
        out_shape=jax.ShapeDtypeStruct((M, N), BF16),
        compiler_params=_params("parallel", "arbitrary"),
        name="swiglu_up",
    )(a, wg, wu)


def _merge_kernel(yh_ref, yd_ref, yr_ref, gd_ref, wh_ref, wd_ref, wr_ref,
                  wgh_ref, wgd_ref, wgr_ref, bh_ref, bd_ref, br_ref, o_ref):
    gd = gd_ref[...]

    def branch(y_ref, w_ref, wg_ref, b_ref):
        gate = _sigmoid(_dot(gd, wg_ref[...]) + b_ref[...])
        return gate * _dot(y_ref[...], w_ref[...])

    acc = branch(yh_ref, wh_ref, wgh_ref, bh_ref)
    acc = acc + branch(yd_ref, wd_ref, wgd_ref, bd_ref)
    acc = acc + branch(yr_ref, wr_ref, wgr_ref, br_ref)
    o_ref[...] = acc.astype(o_ref.dtype)


def gated_merge(yh, yd, yr, gd, wh, wd, wr, wg, bg, tm=1024, tn=512):
    M = yh.shape[0]
    D = wh.shape[1]
    tm, tn = min(tm, M), min(tn, D)
    assert M % tm == 0 and D % tn == 0
    nb = D // tn
    row = lambda width: pl.BlockSpec((tm, width), lambda i, j: (i, 0))
    wcol = lambda k: pl.BlockSpec((k, tn), lambda i, j: (0, j))
    gcol = lambda br: pl.BlockSpec((GATE_RANK, tn), lambda i, j: (0, j + br * nb))
    bcol = lambda br: pl.BlockSpec((1, tn), lambda i, j: (0, j + br * nb))
    bg2 = bg.reshape(1, -1)
    return pl.pallas_call(
        _merge_kernel,
        grid=(M // tm, nb),
        in_specs=[row(yh.shape[1]), row(yd.shape[1]), row(yr.shape[1]), row(gd.shape[1]),
                  wcol(wh.shape[0]), wcol(wd.shape[0]), wcol(wr.shape[0]),
                  gcol(0), gcol(1), gcol(2), bcol(0), bcol(1), bcol(2)],
        out_specs=pl.BlockSpec((tm, tn), lambda i, j: (i, j)),
        out_shape=jax.ShapeDtypeStruct((M, D), BF16),
        compiler_params=_params("parallel", "arbitrary"),
        name="gated_merge",
    )(yh, yd, yr, gd, wh, wd, wr, wg, wg, wg, bg2, bg2, bg2)


def _rope_table_kernel(inv_ref, cos_ref, sin_ref):
    T = cos_ref.shape[0]
    pos = (lax.broadcasted_iota(jnp.int32, (T, LANES), 0) + pl.program_id(0) * T).astype(F32)
    ang = pos * inv_ref[...]
    lane = lax.broadcasted_iota(jnp.int32, (T, LANES), 1)
    cos_ref[...] = jnp.cos(ang)
    sin_ref[...] = jnp.where(lane < LANES // 2, -jnp.sin(ang), jnp.sin(ang))


def rope_tables(S, T=256):
    T = min(T, S)
    half = RET_KDIM // 2
    inv = 1.0 / (ROPE_BASE ** (jnp.arange(half, dtype=F32) / half))
    inv2 = jnp.concatenate([inv, inv]).reshape(1, LANES)
    return pl.pallas_call(
        _rope_table_kernel,
        grid=(S // T,),
        in_specs=[pl.BlockSpec((1, LANES), lambda i: (0, 0))],
        out_specs=[pl.BlockSpec((T, LANES), lambda i: (i, 0))] * 2,
        out_shape=[jax.ShapeDtypeStruct((S, LANES), F32)] * 2,
        compiler_params=_params("parallel"),
        name="rope_tables",
    )(inv2)


def _bias_tile_kernel(tab_ref, o_ref):
    h = pl.program_id(0)
    T = o_ref.shape[2]
    j = lax.broadcasted_iota(jnp.int32, (T, T), 0)
    i = lax.broadcasted_iota(jnp.int32, (T, T), 1)
    max_exact = N_BUCKETS // 2
    o_ref[0, 3] = jnp.full((T, T), NEG, F32)
    for d in range(3):
        n = jnp.maximum(i - j + d * T, 0)
        nf = jnp.maximum(n, 1).astype(F32)
        large = max_exact + (jnp.log(nf / max_exact) / math.log(MAX_DISTANCE / max_exact)
                             * (N_BUCKETS - max_exact)).astype(jnp.int32)
        large = jnp.minimum(large, N_BUCKETS - 1)
        bucket = jnp.where(n < max_exact, n, large)
        val = jnp.zeros((T, T), F32)
        for b in range(N_BUCKETS):
            val = jnp.where(bucket == b, tab_ref[h, b], val)
        val = (val - tab_ref[h, N_BUCKETS - 1]) * LOG2E
        if d == 0:
            val = jnp.where(j > i, NEG, val)
        o_ref[0, d] = val


def bias_tiles(rel_bias, T):
    H = rel_bias.shape[1]
    return pl.pallas_call(
        _bias_tile_kernel,
        grid=(H,),
        in_specs=[pl.BlockSpec(memory_space=pltpu.SMEM)],
        out_specs=pl.BlockSpec((1, 4, T, T), lambda h: (h, 0, 0, 0)),
        out_shape=jax.ShapeDtypeStruct((H, 4, T, T), F32),
        compiler_params=_params("parallel"),
        name="bias_tiles",
    )(rel_bias.T)


def _hgrn_kernel(lbl_ref, nw_ref, q_ref, f_ref, i_ref, g_ref, o_ref,
                 st_ref, p_ref, r_ref, ol_ref, *, layer):
    T = q_ref.shape[0]
    C, SUB = HGRN_CHUNK, HGRN_SUB

    @pl.when(pl.program_id(2) == 0)
    def _():
        st_ref[...] = jnp.zeros_like(st_ref)

    lg = lbl_ref[...]
    e = jnp.exp(lg - jnp.max(lg, axis=0, keepdims=True))
    sm = e / jnp.sum(e, axis=0, keepdims=True)
    csum = sm[0:1]
    for r in range(1, layer + 1):
        csum = csum + sm[r:r + 1]
    lb = csum - sm[0:1]

    forget = lb + (1.0 - lb) * _sigmoid(f_ref[...])
    kf = 1.0 - forget
    logf = jnp.log(forget)
    q = q_ref[...]
    qf = q * _sigmoid(q) * (HGRN_DIM ** -0.5)

    r_i = lax.broadcasted_iota(jnp.int32, (T, T), 0)
    c_i = lax.broadcasted_iota(jnp.int32, (T, T), 1)
    shift = C.bit_length() - 1
    same_chunk = (r_i >> shift) == (c_i >> shift)
    tri = jnp.where(c_i <= r_i, jnp.where(same_chunk, 1.0, 0.0), 0.0).astype(BF16)
    hi = logf.astype(BF16)
    rem = logf - hi.astype(F32)
    mid = rem.astype(BF16)
    lo = (rem - mid.astype(F32)).astype(BF16)
    bcum = _dot(tri, hi) + _dot(tri, mid) + _dot(tri, lo)

    iv_all = i_ref[...]
    g_all = g_ref[...]
    nw = nw_ref[...]
    ones = jnp.ones((HGRN_DIM, HGRN_DIM), BF16)
    row_c = lax.broadcasted_iota(jnp.int32, (C, HGRN_DIM), 0)
    row_s = lax.broadcasted_iota(jnp.int32, (SUB, HGRN_DIM), 0)

    n_sub = T // SUB
    for sb in range(n_sub):
        sub = slice(sb * SUB, (sb + 1) * SUB)
        bi, qi = bcum[sub], qf[sub]
        for s in range(SUB):
            r = sb * SUB + s
            dec = jnp.exp(jnp.where(row_s >= s, bi - bcum[r:r + 1], NEG))
            p_ref[r * SUB:(r + 1) * SUB, :] = (qi * kf[r:r + 1] * dec).astype(BF16)
    r_ref[...] = _dot(p_ref[...], ones)

    n_blk = C // SUB
    for c in range(T // C):
        c0 = c * C
        b, kc = bcum[c0:c0 + C], kf[c0:c0 + C]
        qts, kts = [], []
        for j in range(1, n_blk):
            lo_r = j * SUB
            ref = b[lo_r - 1:lo_r]
            qts.append((qf[c0 + lo_r:c0 + lo_r + SUB] * jnp.exp(b[lo_r:lo_r + SUB] - ref)).astype(BF16))
            kts.append((kc * jnp.exp(jnp.where(row_c < lo_r, ref - b, NEG))).astype(BF16))
        att_all = _dot_nt(jnp.concatenate(qts, axis=0), jnp.concatenate(kts, axis=0))
        att = jnp.concatenate([att_all[j * SUB:(j + 1) * SUB, j * C:(j + 1) * C]
                               for j in range(n_blk - 1)], axis=0)
        off = _dot(att.astype(BF16), iv_all[c0:c0 + C].astype(BF16))
        for j in range(n_blk):
            lo_r = c0 + j * SUB
            oi = r_ref[lo_r * SUB:(lo_r + 1) * SUB, :] * iv_all[lo_r:lo_r + 1]
            for s in range(1, SUB):
                r = lo_r + s
                oi = oi + r_ref[r * SUB:(r + 1) * SUB, :] * iv_all[r:r + 1]
            if j > 0:
                oi = oi + off[(j - 1) * SUB:j * SUB]
            ol_ref[lo_r:lo_r + SUB, :] = oi

    state_t = st_ref[...]
    for c in range(T // C):
        rows = slice(c * C, (c + 1) * C)
        b, qc, kc, iv = bcum[rows], qf[rows], kf[rows], iv_all[rows]
        o = ol_ref[rows, :] + _dot_nt((qc * jnp.exp(b)).astype(BF16), state_t.astype(BF16))
        b_last = b[C - 1:C]
        khat = (kc * jnp.exp(b_last - b)).astype(BF16)
        state_t = state_t * jnp.exp(b_last) + _dot(iv.T.astype(BF16), khat)
        ms = jnp.mean(o * o, axis=-1, keepdims=True)
        gc = g_all[rows]
        y = o * lax.rsqrt(ms + RMS_EPS) * nw * (gc * _sigmoid(gc))
        o_ref[rows, :] = y.astype(o_ref.dtype)
    st_ref[...] = state_t


def hgrn_mixer(p_h, lb_logits, norm_w, layer, B, S, T=256):
    T = min(T, S)
    nt = S // T
    H, dk = HGRN_HEADS, HGRN_DIM
    col = lambda grp: pl.BlockSpec((T, dk), lambda b, h, t: (b * nt + t, grp * H + h))
    return pl.pallas_call(
        functools.partial(_hgrn_kernel, layer=layer),
        grid=(B, H, nt),
        in_specs=[pl.BlockSpec((lb_logits.shape[0], dk), lambda b, h, t: (0, h)),
                  pl.BlockSpec((1, dk), lambda b, h, t: (0, 0)),
                  col(0), col(1), col(2), col(3)],
        out_specs=pl.BlockSpec((T, dk), lambda b, h, t: (b * nt + t, h)),
        out_shape=jax.ShapeDtypeStruct((B * S, H * dk), BF16),
        scratch_shapes=[pltpu.VMEM((dk, dk), F32),
                        pltpu.VMEM((T * HGRN_SUB, dk), BF16), pltpu.VMEM((T * HGRN_SUB, dk), F32),
                        pltpu.VMEM((T, dk), F32)],
        compiler_params=_params("parallel", "parallel", "arbitrary"),
        name="hgrn_mixer",
    )(lb_logits, norm_w.reshape(1, dk), p_h, p_h, p_h, p_h)


def _diff_kernel(lam_ref, nw_ref, q_ref, k_ref, v_ref, bias_ref, o_ref,
                 vt_ref, s0_ref, s1_ref, gm_ref, m_ref, acc_ref, *, lam_init):
    T = q_ref.shape[0]
    S = k_ref.shape[0]
    dh = DIFF_HEAD_DIM
    G = DIFF_GROUP
    n_groups = S // (G * T)
    qi = pl.program_id(2)
    ng = qi // G + 1

    @pl.when(qi == 0)
    def _():
        def body(c, carry):
            start = pl.multiple_of(c * T, T)
            vt_ref[0:2 * dh, pl.ds(start, T)] = v_ref[pl.ds(start, T), :].astype(F32).T.astype(BF16)
            return carry
        lax.fori_loop(0, S // T, body, 0)
        pad_row = lax.broadcasted_iota(jnp.int32, (BF16_TILE_ROWS, S), 0)
        vt_ref[2 * dh:, :] = jnp.where(pad_row == 0, 1.0, 0.0).astype(BF16)

    qt = q_ref[...].astype(F32).T * (dh ** -0.5 * LOG2E)
    row = lax.broadcasted_iota(jnp.int32, qt.shape, 0)
    q2 = jnp.concatenate([jnp.where(row < dh, qt, 0.0), jnp.where(row >= dh, qt, 0.0)],
                         axis=1).astype(BF16)

    slots = (s0_ref, s1_ref)

    def scores_group(g, near):
        slot = slots[g % 2]
        gm = None
        for u in range(G):
            kt = g * G + u
            s = _dot(k_ref[kt * T:(kt + 1) * T, :], q2)
            if near:
                d = qi - kt
                bias = bias_ref[0, jnp.where(d < 0, 3, jnp.minimum(d, 2))]
                s = s + jnp.concatenate([bias, bias], axis=1)
            slot[u] = s
            cm = jnp.max(s, axis=0, keepdims=True)
            gm = cm if gm is None else jnp.maximum(gm, cm)
        gm_ref[g % 2] = gm

    def softmax_group(g):
        slot = slots[g % 2]
        m_old = m_ref[...]
        m_new = jnp.maximum(m_old, gm_ref[g % 2])
        alpha = jnp.exp2(m_old - m_new)
        ps = [jnp.exp2(slot[u] - m_new).astype(BF16) for u in range(G)]
        pv = _dot(vt_ref[:, g * G * T:(g + 1) * G * T], jnp.concatenate(ps, axis=0))
        m_ref[...] = m_new
        acc_ref[...] = alpha * acc_ref[...] + pv

    m_ref[...] = jnp.full(m_ref.shape, NEG, F32)
    acc_ref[...] = jnp.zeros(acc_ref.shape, F32)
    n_far_groups = jnp.maximum(qi - 1, 0) // G

    def stage(cond, g_scores, g_softmax):
        for near in (False, True):
            is_near = g_scores >= n_far_groups
            @pl.when(jnp.logical_and(cond, is_near if near else jnp.logical_not(is_near)))
            def _():
                scores_group(g_scores, near)
                if g_softmax is not None:
                    softmax_group(g_softmax)

    stage(True, 0, None)
    for g in range(n_groups):
        if g + 1 < n_groups:
            stage(g < ng - 1, g + 1, g)

        @pl.when(g == ng - 1)
        def _():
            softmax_group(g)

    lp = lam_ref[...]
    lam = (jnp.exp(jnp.sum(lp[0:1] * lp[1:2], axis=-1, keepdims=True))
           - jnp.exp(jnp.sum(lp[2:3] * lp[3:4], axis=-1, keepdims=True)) + lam_init)
    acc = acc_ref[...]
    w = acc[:2 * dh] / acc[2 * dh:2 * dh + 1]
    out = (w[:, :T] - lam * w[:, T:]).T
    ms = jnp.mean(out * out, axis=-1, keepdims=True)
    y = out * lax.rsqrt(ms + RMS_EPS) * nw_ref[...] * (1.0 - lam_init)
    o_ref[...] = y.astype(o_ref.dtype)


def diff_attention(p_d, bias, lam_params, norm_w, lam_init, B, S):
    T = bias.shape[2]
    G = DIFF_GROUP
    assert T >= MAX_DISTANCE and S % (G * T) == 0
    nq = S // T
    H, hw = DIFF_HEADS, 2 * DIFF_HEAD_DIM
    return pl.pallas_call(
        functools.partial(_diff_kernel, lam_init=lam_init),
        grid=(B, H, nq),
        in_specs=[pl.BlockSpec(lam_params.shape, lambda b, h, i: (0, 0)),
                  pl.BlockSpec((1, hw), lambda b, h, i: (0, 0)),
                  pl.BlockSpec((T, hw), lambda b, h, i: (b * nq + i, h)),
                  pl.BlockSpec((S, hw), lambda b, h, i: (b, H + h)),
                  pl.BlockSpec((S, hw), lambda b, h, i: (b, 2 * H + h)),
                  pl.BlockSpec((1, 4, T, T), lambda b, h, i: (h, 0, 0, 0))],
        out_specs=pl.BlockSpec((T, hw), lambda b, h, i: (b * nq + i, h)),
        out_shape=jax.ShapeDtypeStruct((B * S, H * hw), BF16),
        scratch_shapes=[pltpu.VMEM((hw + BF16_TILE_ROWS, S), BF16),
                        pltpu.VMEM((G, T, 2 * T), F32), pltpu.VMEM((G, T, 2 * T), F32),
                        pltpu.VMEM((2, 1, 2 * T), F32),
                        pltpu.VMEM((1, 2 * T), F32),
                        pltpu.VMEM((hw + BF16_TILE_ROWS, 2 * T), F32)],
        compiler_params=_params("parallel", "parallel", "arbitrary"),
        name="diff_attention",
    )(lam_params, norm_w.reshape(1, hw), p_d, p_d, p_d, bias)


def _ret_kernel(lg_ref, nw_ref, cos_ref, sin_ref, q_ref, k_ref, v_ref, g_ref, o_ref, st_ref):
    C = q_ref.shape[0]
    dk = RET_KDIM

    @pl.when(pl.program_id(2) == 0)
    def _():
        st_ref[...] = jnp.zeros_like(st_ref)

    lg = lg_ref[0]
    lg1 = lg[:, 0:1]
    cosf, sinf = cos_ref[...], sin_ref[...]

    def rot(x):
        return x * cosf + pltpu.roll(x, dk // 2, 1) * sinf

    qr = rot(q_ref[...])
    kr = rot(k_ref[...]) * (dk ** -0.5)
    rowf = lax.broadcasted_iota(jnp.int32, (C, dk), 0).astype(F32)
    xi = jnp.exp((rowf + 1.0) * lg)
    zeta = jnp.exp((C - 1.0 - rowf) * lg)
    r_i = lax.broadcasted_iota(jnp.int32, (C, C), 0)
    c_i = lax.broadcasted_iota(jnp.int32, (C, C), 1)
    decay = jnp.exp(jnp.where(r_i >= c_i, (r_i - c_i).astype(F32) * lg1, NEG))

    v = v_ref[...]
    state = st_ref[...]
    scores = _dot_nt(qr.astype(BF16), kr.astype(BF16)) * decay
    o = _dot(scores.astype(BF16), v) + _dot((qr * xi).astype(BF16), state.astype(BF16))
    st_ref[...] = jnp.exp(C * lg1) * state + _dot((kr * zeta).T.astype(BF16), v)

    ms = jnp.mean(o * o, axis=-1, keepdims=True)
    g = g_ref[...].astype(F32)
    y = o * lax.rsqrt(ms + RMS_EPS) * nw_ref[...] * (g * _sigmoid(g))
    o_ref[...] = y.astype(o_ref.dtype)


def retention_mixer(p_qk, p_vg, cos_t, sin_t, norm_w, B, S, C=256):
    C = min(C, S)
    nt = S // C
    H, dk, dv = RET_HEADS, RET_KDIM, RET_VDIM
    log_gamma = jnp.log(1.0 - 2.0 ** (-5.0 - jnp.arange(H, dtype=F32)))
    lg = jnp.broadcast_to(log_gamma[:, None, None], (H, 1, LANES))
    return pl.pallas_call(
        _ret_kernel,
        grid=(B, H, nt),
        in_specs=[pl.BlockSpec((1, 1, LANES), lambda b, h, t: (h, 0, 0)),
                  pl.BlockSpec((1, dv), lambda b, h, t: (0, 0)),
                  pl.BlockSpec((C, dk), lambda b, h, t: (t, 0)),
                  pl.BlockSpec((C, dk), lambda b, h, t: (t, 0)),
                  pl.BlockSpec((C, dk), lambda b, h, t: (b * nt + t, h)),
                  pl.BlockSpec((C, dk), lambda b, h, t: (b * nt + t, H + h)),
                  pl.BlockSpec((C, dv), lambda b, h, t: (b * nt + t, h)),
                  pl.BlockSpec((C, dv), lambda b, h, t: (b * nt + t, H + h))],
        out_specs=pl.BlockSpec((C, dv), lambda b, h, t: (b * nt + t, h)),
        out_shape=jax.ShapeDtypeStruct((B * S, H * dv), BF16),
        scratch_shapes=[pltpu.VMEM((dk, dv), F32)],
        compiler_params=_params("parallel", "parallel", "arbitrary"),
        name="retention_mixer",
    )(lg, norm_w.reshape(1, dv), cos_t, sin_t, p_qk, p_qk, p_vg, p_vg)


def kernel(x, attn_norm_w, w_in, lb_logits, hgrn_norm_w, rel_bias, diff_lambda, diff_norm_w,
           ret_norm_w, w_gate_up, b_gate, w_br_hgrn, w_br_diff, w_br_ret, w_o, ffn_norm_w,
           w_ffn_gate, w_ffn_up, w_ffn_down, final_norm_w):
    B, S, D = x.shape
    depth = w_in.shape[0]
    M = B * S
    xs = x.reshape(M, D)

    cos_t, sin_t = rope_tables(S)
    bias = bias_tiles(rel_bias, min(256, S))

    off_d = 4 * HGRN_WIDTH
    off_rqk = off_d + 3 * DIFF_WIDTH
    off_rvg = off_rqk + 2 * RET_QK_WIDTH
    off_gd = off_rvg + 2 * RET_V_WIDTH

    for l in range(depth):
        w_in_l = w_in[l].astype(BF16)
        h = rms_norm_rows(xs, attn_norm_w[l], BF16)
        p_h = matmul_cols(h, w_in_l, 0, 4 * HGRN_WIDTH, F32)
        p_d = matmul_cols(h, w_in_l, off_d, 3 * DIFF_WIDTH, BF16)
        p_rqk = matmul_cols(h, w_in_l, off_rqk, 2 * RET_QK_WIDTH, F32)
        p_rvg = matmul_cols(h, w_in_l, off_rvg, 2 * RET_V_WIDTH, BF16)
        gd = matmul_cols(h, w_in_l, off_gd, GATE_RANK, BF16, tn=GATE_RANK)

        y_h = hgrn_mixer(p_h, lb_logits, hgrn_norm_w[l], l, B, S)
        lam_init = 0.8 - 0.6 * math.exp(-0.3 * l)
        y_d = diff_attention(p_d, bias, diff_lambda[l], diff_norm_w[l], lam_init, B, S)
        y_r = retention_mixer(p_rqk, p_rvg, cos_t, sin_t, ret_norm_w[l], B, S)

        merged = gated_merge(y_h, y_d, y_r, gd,
                             w_br_hgrn[l].astype(BF16), w_br_diff[l].astype(BF16),
                             w_br_ret[l].astype(BF16), w_gate_up[l].astype(BF16), b_gate[l])
        xs = matmul_residual(merged, w_o[l].astype(BF16), xs)

        h2 = rms_norm_rows(xs, ffn_norm_w[l], BF16)
        act = swiglu_up(h2, w_ffn_gate[l].astype(BF16), w_ffn_up[l].astype(BF16))
        xs = matmul_residual(act, w_ffn_down[l].astype(BF16), xs, tn=256, a_buffers=1)

    out = rms_norm_rows(xs, final_norm_w, x.dtype)
    return out.reshape(B, S, D)
```

```python
import functools
import math

import jax
import jax.numpy as jnp
from jax import lax
from jax.experimental import pallas as pl
from jax.experimental.pallas import tpu as pltpu

F32 = jnp.float32
BF16 = jnp.bfloat16

HGRN_HEADS = 8
HGRN_DIM = 128
DIFF_HEADS = 8
DIFF_HEAD_DIM = 64
RET_HEADS = 8
RET_KDIM = 128
RET_VDIM = 256
N_BUCKETS = 32
MAX_DISTANCE = 128
ROPE_BASE = 10000.0
GATE_RANK = 256
RMS_EPS = 1e-6
HGRN_WIDTH = HGRN_HEADS * HGRN_DIM
DIFF_WIDTH = DIFF_HEADS * 2 * DIFF_HEAD_DIM
RET_QK_WIDTH = RET_HEADS * RET_KDIM
RET_V_WIDTH = RET_HEADS * RET_VDIM

V7X_VMEM_BYTES = 64 * 1024 * 1024
VMEM_LIMIT = V7X_VMEM_BYTES - 8 * 1024 * 1024
LANES = 128
BF16_TILE_ROWS = 16
F32_TILE_ROWS = 8
LOG2E = math.log2(math.e)

NEG = -1e30
HGRN_CHUNK = 128
HGRN_SUB = 16
DIFF_GROUP = 4
DIFF_QTILES = 2


def _params(*sem):
    return pltpu.CompilerParams(dimension_semantics=sem, vmem_limit_bytes=VMEM_LIMIT)


def _sigmoid(x):
    return 1.0 / (1.0 + jnp.exp(-x))


def _dot(a, b):
    return jnp.dot(a, b, preferred_element_type=F32)


def _dot_nt(a, b):
    return lax.dot_general(a, b, (((1,), (1,)), ((), ())), preferred_element_type=F32)


def _rms_kernel(x_ref, w_ref, o_ref):
    x = x_ref[...]
    ms = jnp.mean(x * x, axis=-1, keepdims=True)
    o_ref[...] = (x * lax.rsqrt(ms + RMS_EPS) * w_ref[...]).astype(o_ref.dtype)


def rms_norm_rows(x, w, out_dtype, tm=256):
    M, D = x.shape
    tm = min(tm, M)
    return pl.pallas_call(
        _rms_kernel,
        grid=(M // tm,),
        in_specs=[pl.BlockSpec((tm, D), lambda i: (i, 0)),
                  pl.BlockSpec((1, D), lambda i: (0, 0))],
        out_specs=pl.BlockSpec((tm, D), lambda i: (i, 0)),
        out_shape=jax.ShapeDtypeStruct((M, D), out_dtype),
        compiler_params=_params("parallel"),
        name="rmsnorm",
    )(x, w.reshape(1, D))


def _mm_kernel(a_ref, b_ref, o_ref):
    o_ref[...] = _dot(a_ref[...], b_ref[...]).astype(o_ref.dtype)


def matmul_cols(a, b, col_off, n, out_dtype, tm=1024, tn=1024):
    M, K = a.shape
    tm, tn = min(tm, M), min(tn, n)
    assert col_off % tn == 0 and n % tn == 0 and M % tm == 0
    off = col_off // tn
    return pl.pallas_call(
        _mm_kernel,
        grid=(M // tm, n // tn),
        in_specs=[pl.BlockSpec((tm, K), lambda i, j: (i, 0)),
                  pl.BlockSpec((K, tn), lambda i, j: (0, j + off))],
        out_specs=pl.BlockSpec((tm, tn), lambda i, j: (i, j)),
        out_shape=jax.ShapeDtypeStruct((M, n), out_dtype),
        compiler_params=_params("parallel", "arbitrary"),
        name="matmul_cols",
    )(a, b)


def _mm_res_kernel(a_ref, b_ref, x_ref, o_ref):
    o_ref[...] = x_ref[...] + _dot(a_ref[...], b_ref[...])


def matmul_residual(a, b, x, tm=1024, tn=512, a_buffers=2):
    M, K = a.shape
    N = b.shape[1]
    tm, tn = min(tm, M), min(tn, N)
    assert M % tm == 0 and N % tn == 0
    return pl.pallas_call(
        _mm_res_kernel,
        grid=(M // tm, N // tn),
        in_specs=[pl.BlockSpec((tm, K), lambda i, j: (i, 0), pipeline_mode=pl.Buffered(a_buffers)),
                  pl.BlockSpec((K, tn), lambda i, j: (0, j)),
                  pl.BlockSpec((tm, tn), lambda i, j: (i, j))],
        out_specs=pl.BlockSpec((tm, tn), lambda i, j: (i, j)),
        out_shape=jax.ShapeDtypeStruct((M, N), F32),
        input_output_aliases={2: 0},
        compiler_params=_params("parallel", "arbitrary"),
        name="matmul_residual",
    )(a, b, x)


def _swiglu_kernel(a_ref, wg_ref, wu_ref, o_ref):
    a = a_ref[...]
    g = _dot(a, wg_ref[...].astype(BF16))
    u = _dot(a, wu_ref[...].astype(BF16))
    o_ref[...] = (g * _sigmoid(g) * u).astype(o_ref.dtype)


def swiglu_up(a, wg, wu, layer, tm=2048, tn=256):
    M, K = a.shape
    N = wg.shape[2]
    tm, tn = min(tm, M), min(tn, N)
    assert M % tm == 0 and N % tn == 0
    return pl.pallas_call(
        _swiglu_kernel,
        grid=(M // tm, N // tn),
        in_specs=[pl.BlockSpec((tm, K), lambda i, j: (i, 0), pipeline_mode=pl.Buffered(1)),
                  pl.BlockSpec((None, K, tn), lambda i, j: (layer, 0, j)),
                  pl.BlockSpec((None, K, tn), lambda i, j: (layer, 0, j))],
        out_specs=pl.BlockSpec((tm, tn), lambda i, j: (i, j)),
        out_shape=jax.ShapeDtypeStruct((M, N), BF16),
        compiler_params=_params("parallel", "arbitrary"),
        name="swiglu_up",
    )(a, wg, wu)


def _merge_kernel(yh_ref, yd_ref, yr_ref, gd_ref, wh_ref, wd_ref, wr_ref,
                  wgh_ref, wgd_ref, wgr_ref, bh_ref, bd_ref, br_ref, o_ref):
    gd = gd_ref[...]

    def branch(y_ref, w_ref, wg_ref, b_ref):
        gate = _sigmoid(_dot(gd, wg_ref[...]) + b_ref[...])
        return gate * _dot(y_ref[...], w_ref[...])

    acc = branch(yh_ref, wh_ref, wgh_ref, bh_ref)
    acc = acc + branch(yd_ref, wd_ref, wgd_ref, bd_ref)
    acc = acc + branch(yr_ref, wr_ref, wgr_ref, br_ref)
    o_ref[...] = acc.astype(o_ref.dtype)


def gated_merge(yh, yd, yr, gd, wh, wd, wr, wg, bg, tm=1024, tn=512):
    M = yh.shape[0]
    D = wh.shape[1]
    tm, tn = min(tm, M), min(tn, D)
    assert M % tm == 0 and D % tn == 0
    nb = D // tn
    row = lambda width: pl.BlockSpec((tm, width), lambda i, j: (i, 0))
    wcol = lambda k: pl.BlockSpec((k, tn), lambda i, j: (0, j))
    gcol = lambda br: pl.BlockSpec((GATE_RANK, tn), lambda i, j: (0, j + br * nb))
    bcol = lambda br: pl.BlockSpec((1, tn), lambda i, j: (0, j + br * nb))
    bg2 = bg.reshape(1, -1)
    return pl.pallas_call(
        _merge_kernel,
        grid=(M // tm, nb),
        in_specs=[row(yh.shape[1]), row(yd.shape[1]), row(yr.shape[1]), row(gd.shape[1]),
                  wcol(wh.shape[0]), wcol(wd.shape[0]), wcol(wr.shape[0]),
                  gcol(0), gcol(1), gcol(2), bcol(0), bcol(1), bcol(2)],
        out_specs=pl.BlockSpec((tm, tn), lambda i, j: (i, j)),
        out_shape=jax.ShapeDtypeStruct((M, D), BF16),
        compiler_params=_params("parallel", "arbitrary"),
        name="gated_merge",
    )(yh, yd, yr, gd, wh, wd, wr, wg, wg, wg, bg2, bg2, bg2)


def _rope_table_kernel(inv_ref, cos_ref, sin_ref):
    T = cos_ref.shape[0]
    pos = (lax.broadcasted_iota(jnp.int32, (T, LANES), 0) + pl.program_id(0) * T).astype(F32)
    ang = pos * inv_ref[...]
    lane = lax.broadcasted_iota(jnp.int32, (T, LANES), 1)
    cos_ref[...] = jnp.cos(ang)
    sin_ref[...] = jnp.where(lane < LANES // 2, -jnp.sin(ang), jnp.sin(ang))


def rope_tables(S, T=256):
    T = min(T, S)
    half = RET_KDIM // 2
    inv = 1.0 / (ROPE_BASE ** (jnp.arange(half, dtype=F32) / half))
    inv2 = jnp.concatenate([inv, inv]).reshape(1, LANES)
    return pl.pallas_call(
        _rope_table_kernel,
        grid=(S // T,),
        in_specs=[pl.BlockSpec((1, LANES), lambda i: (0, 0))],
        out_specs=[pl.BlockSpec((T, LANES), lambda i: (i, 0))] * 2,
        out_shape=[jax.ShapeDtypeStruct((S, LANES), F32)] * 2,
        compiler_params=_params("parallel"),
        name="rope_tables",
    )(inv2)


def _bias_tile_kernel(tab_ref, o_ref):
    h = pl.program_id(0)
    T = o_ref.shape[2]
    j = lax.broadcasted_iota(jnp.int32, (T, T), 0)
    i = lax.broadcasted_iota(jnp.int32, (T, T), 1)
    max_exact = N_BUCKETS // 2
    o_ref[0, 3] = jnp.full((T, T), NEG, F32)
    for d in range(3):
        n = jnp.maximum(i - j + d * T, 0)
        nf = jnp.maximum(n, 1).astype(F32)
        large = max_exact + (jnp.log(nf / max_exact) / math.log(MAX_DISTANCE / max_exact)
                             * (N_BUCKETS - max_exact)).astype(jnp.int32)
        large = jnp.minimum(large, N_BUCKETS - 1)
        bucket = jnp.where(n < max_exact, n, large)
        val = jnp.zeros((T, T), F32)
        for b in range(N_BUCKETS):
            val = jnp.where(bucket == b, tab_ref[h, b], val)
        val = (val - tab_ref[h, N_BUCKETS - 1]) * LOG2E
        if d == 0:
            val = jnp.where(j > i, NEG, val)
        o_ref[0, d] = val


def bias_tiles(rel_bias, T):
    H = rel_bias.shape[1]
    return pl.pallas_call(
        _bias_tile_kernel,
        grid=(H,),
        in_specs=[pl.BlockSpec(memory_space=pltpu.SMEM)],
        out_specs=pl.BlockSpec((1, 4, T, T), lambda h: (h, 0, 0, 0)),
        out_shape=jax.ShapeDtypeStruct((H, 4, T, T), F32),
        compiler_params=_params("parallel"),
        name="bias_tiles",
    )(rel_bias.T)


def _hgrn_kernel(lbl_ref, nw_ref, q_ref, f_ref, i_ref, g_ref, o_ref,
                 st_ref, p_ref, r_ref, ol_ref, *, layer):
    T = q_ref.shape[0]
    C, SUB = HGRN_CHUNK, HGRN_SUB

    @pl.when(pl.program_id(2) == 0)
    def _():
        st_ref[...] = jnp.zeros_like(st_ref)

    lg = lbl_ref[...]
    e = jnp.exp(lg - jnp.max(lg, axis=0, keepdims=True))
    sm = e / jnp.sum(e, axis=0, keepdims=True)
    csum = sm[0:1]
    for r in range(1, layer + 1):
        csum = csum + sm[r:r + 1]
    lb = csum - sm[0:1]

    forget = lb + (1.0 - lb) * _sigmoid(f_ref[...])
    kf = 1.0 - forget
    logf = jnp.log(forget)
    q = q_ref[...]
    qf = q * _sigmoid(q) * (HGRN_DIM ** -0.5)

    r_i = lax.broadcasted_iota(jnp.int32, (T, T), 0)
    c_i = lax.broadcasted_iota(jnp.int32, (T, T), 1)
    shift = C.bit_length() - 1
    same_chunk = (r_i >> shift) == (c_i >> shift)
    tri = jnp.where(c_i <= r_i, jnp.where(same_chunk, 1.0, 0.0), 0.0).astype(BF16)
    hi = logf.astype(BF16)
    rem = logf - hi.astype(F32)
    mid = rem.astype(BF16)
    lo = (rem - mid.astype(F32)).astype(BF16)
    bcum = _dot(tri, hi) + _dot(tri, mid) + _dot(tri, lo)

    iv_all = i_ref[...]
    g_all = g_ref[...]
    nw = nw_ref[...]
    ones = jnp.ones((HGRN_DIM, HGRN_DIM), BF16)
    row_c = lax.broadcasted_iota(jnp.int32, (C, HGRN_DIM), 0)
    row_s = lax.broadcasted_iota(jnp.int32, (SUB, HGRN_DIM), 0)

    n_sub = T // SUB
    for sb in range(n_sub):
        sub = slice(sb * SUB, (sb + 1) * SUB)
        bi, qi = bcum[sub], qf[sub]
        for s in range(SUB):
            r = sb * SUB + s
            top = (s // F32_TILE_ROWS) * F32_TILE_ROWS
            dec = jnp.exp(jnp.where(row_s[top:] >= s, bi[top:] - bcum[r:r + 1], NEG))
            p = qi[top:] * kf[r:r + 1] * dec
            if top:
                p = jnp.concatenate([jnp.zeros((top, HGRN_DIM), F32), p], axis=0)
            p_ref[r * SUB:(r + 1) * SUB, :] = p.astype(BF16)
    r_ref[...] = _dot(p_ref[...], ones)

    n_blk = C // SUB
    for c in range(T // C):
        c0 = c * C
        b, kc = bcum[c0:c0 + C], kf[c0:c0 + C]
        qts, kts = [], []
        for j in range(1, n_blk):
            lo_r = j * SUB
            ref = b[lo_r - 1:lo_r]
            qts.append((qf[c0 + lo_r:c0 + lo_r + SUB] * jnp.exp(b[lo_r:lo_r + SUB] - ref)).astype(BF16))
            kts.append((kc * jnp.exp(jnp.where(row_c < lo_r, ref - b, NEG))).astype(BF16))
        att_all = _dot_nt(jnp.concatenate(qts, axis=0), jnp.concatenate(kts, axis=0))
        att = jnp.concatenate([att_all[j * SUB:(j + 1) * SUB, j * C:(j + 1) * C]
                               for j in range(n_blk - 1)], axis=0)
        off = _dot(att.astype(BF16), iv_all[c0:c0 + C].astype(BF16))
        for j in range(n_blk):
            lo_r = c0 + j * SUB
            oi = r_ref[lo_r * SUB:(lo_r + 1) * SUB, :] * iv_all[lo_r:lo_r + 1]
            for s in range(1, SUB):
                r = lo_r + s
                top = (s // F32_TILE_ROWS) * F32_TILE_ROWS
                term = r_ref[r * SUB + top:(r + 1) * SUB, :] * iv_all[r:r + 1]
                oi = oi + term if top == 0 else jnp.concatenate([oi[:top], oi[top:] + term], axis=0)
            if j > 0:
                oi = oi + off[(j - 1) * SUB:j * SUB]
            ol_ref[lo_r:lo_r + SUB, :] = oi

    state_t = st_ref[...]
    for c in range(T // C):
        rows = slice(c * C, (c + 1) * C)
        b, qc, kc, iv = bcum[rows], qf[rows], kf[rows], iv_all[rows]
        o = ol_ref[rows, :] + _dot_nt((qc * jnp.exp(b)).astype(BF16), state_t.astype(BF16))
        b_last = b[C - 1:C]
        khat = (kc * jnp.exp(b_last - b)).astype(BF16)
        state_t = state_t * jnp.exp(b_last) + _dot(iv.T.astype(BF16), khat)
        ms = jnp.mean(o * o, axis=-1, keepdims=True)
        gc = g_all[rows]
        y = o * lax.rsqrt(ms + RMS_EPS) * nw * (gc * _sigmoid(gc))
        o_ref[rows, :] = y.astype(o_ref.dtype)
    st_ref[...] = state_t


def hgrn_mixer(p_h, lb_logits, norm_w, layer, B, S, T=256):
    T = min(T, S)
    nt = S // T
    H, dk = HGRN_HEADS, HGRN_DIM
    col = lambda grp: pl.BlockSpec((T, dk), lambda b, h, t: (b * nt + t, grp * H + h))
    return pl.pallas_call(
        functools.partial(_hgrn_kernel, layer=layer),
        grid=(B, H, nt),
        in_specs=[pl.BlockSpec((lb_logits.shape[0], dk), lambda b, h, t: (0, h)),
                  pl.BlockSpec((1, dk), lambda b, h, t: (0, 0)),
                  col(0), col(1), col(2), col(3)],
        out_specs=pl.BlockSpec((T, dk), lambda b, h, t: (b * nt + t, h)),
        out_shape=jax.ShapeDtypeStruct((B * S, H * dk), BF16),
        scratch_shapes=[pltpu.VMEM((dk, dk), F32),
                        pltpu.VMEM((T * HGRN_SUB, dk), BF16), pltpu.VMEM((T * HGRN_SUB, dk), F32),
                        pltpu.VMEM((T, dk), F32)],
        compiler_params=_params("parallel", "parallel", "arbitrary"),
        name="hgrn_mixer",
    )(lb_logits, norm_w.reshape(1, dk), p_h, p_h, p_h, p_h)


def _diff_kernel(lam_ref, nw_ref, q_ref, k_ref, v_ref, bias_ref, o_ref,
                 vt_ref, s0_ref, s1_ref, gm_ref, m_ref, acc_ref, *, lam_init):
    QT = DIFF_QTILES
    T = q_ref.shape[0] // QT
    S = k_ref.shape[0]
    dh = DIFF_HEAD_DIM
    G = DIFF_GROUP
    n_groups = S // (G * T)
    qi = pl.program_id(2) * QT
    ng = (qi + QT - 1) // G + 1

    @pl.when(qi == 0)
    def _():
        def body(c, carry):
            start = pl.multiple_of(c * T, T)
            vt_ref[0:2 * dh, pl.ds(start, T)] = v_ref[pl.ds(start, T), :].astype(F32).T.astype(BF16)
            return carry
        lax.fori_loop(0, S // T, body, 0)
        pad_row = lax.broadcasted_iota(jnp.int32, (BF16_TILE_ROWS, S), 0)
        vt_ref[2 * dh:, :] = jnp.where(pad_row == 0, 1.0, 0.0).astype(BF16)

    row = lax.broadcasted_iota(jnp.int32, (2 * dh, T), 0)
    cols = []
    for t in range(QT):
        qt = q_ref[t * T:(t + 1) * T, :].astype(F32).T * (dh ** -0.5 * LOG2E)
        cols += [jnp.where(row < dh, qt, 0.0), jnp.where(row >= dh, qt, 0.0)]
    q2 = jnp.concatenate(cols, axis=1).astype(BF16)

    slots = (s0_ref, s1_ref)

    def scores_group(g, near):
        slot = slots[g % 2]
        gm = None
        for u in range(G):
            kt = g * G + u
            s = _dot(k_ref[kt * T:(kt + 1) * T, :], q2)
            if near:
                tiles = []
                for t in range(QT):
                    d = qi + t - kt
                    tiles += [bias_ref[0, jnp.where(d < 0, 3, jnp.minimum(d, 2))]] * 2
                s = s + jnp.concatenate(tiles, axis=1)
            slot[u] = s
            cm = jnp.max(s, axis=0, keepdims=True)
            gm = cm if gm is None else jnp.maximum(gm, cm)
        gm_ref[g % 2] = gm

    def softmax_group(g):
        slot = slots[g % 2]
        m_old = m_ref[...]
        m_new = jnp.maximum(m_old, gm_ref[g % 2])
        alpha = jnp.exp2(m_old - m_new)
        ps = [jnp.exp2(slot[u] - m_new).astype(BF16) for u in range(G)]
        pv = _dot(vt_ref[:, g * G * T:(g + 1) * G * T], jnp.concatenate(ps, axis=0))
        m_ref[...] = m_new
        acc_ref[...] = alpha * acc_ref[...] + pv

    m_ref[...] = jnp.full(m_ref.shape, NEG, F32)
    acc_ref[...] = jnp.zeros(acc_ref.shape, F32)
    n_far_groups = jnp.maximum(qi - 1, 0) // G

    def stage(cond, g_scores, g_softmax):
        for near in (False, True):
            is_near = g_scores >= n_far_groups
            @pl.when(jnp.logical_and(cond, is_near if near else jnp.logical_not(is_near)))
            def _():
                scores_group(g_scores, near)
                if g_softmax is not None:
                    softmax_group(g_softmax)

    stage(True, 0, None)
    for g in range(n_groups):
        if g + 1 < n_groups:
            stage(g < ng - 1, g + 1, g)

        @pl.when(g == ng - 1)
        def _():
            softmax_group(g)

    lp = lam_ref[...]
    lam = (jnp.exp(jnp.sum(lp[0:1] * lp[1:2], axis=-1, keepdims=True))
           - jnp.exp(jnp.sum(lp[2:3] * lp[3:4], axis=-1, keepdims=True)) + lam_init)
    acc = acc_ref[...]
    w = acc[:2 * dh] / acc[2 * dh:2 * dh + 1]
    for t in range(QT):
        c0 = 2 * t * T
        out = (w[:, c0:c0 + T] - lam * w[:, c0 + T:c0 + 2 * T]).T
        ms = jnp.mean(out * out, axis=-1, keepdims=True)
        y = out * lax.rsqrt(ms + RMS_EPS) * nw_ref[...] * (1.0 - lam_init)
        o_ref[t * T:(t + 1) * T, :] = y.astype(o_ref.dtype)


def diff_attention(p_d, bias, lam_params, norm_w, lam_init, B, S):
    T = bias.shape[2]
    G = DIFF_GROUP
    QT = DIFF_QTILES
    assert T >= MAX_DISTANCE and S % (G * T) == 0 and G % QT == 0
    nq = S // (QT * T)
    lanes = 2 * QT * T
    H, hw = DIFF_HEADS, 2 * DIFF_HEAD_DIM
    return pl.pallas_call(
        functools.partial(_diff_kernel, lam_init=lam_init),
        grid=(B, H, nq),
        in_specs=[pl.BlockSpec(lam_params.shape, lambda b, h, i: (0, 0)),
                  pl.BlockSpec((1, hw), lambda b, h, i: (0, 0)),
                  pl.BlockSpec((QT * T, hw), lambda b, h, i: (b * nq + i, h)),
                  pl.BlockSpec((S, hw), lambda b, h, i: (b, H + h)),
                  pl.BlockSpec((S, hw), lambda b, h, i: (b, 2 * H + h)),
                  pl.BlockSpec((1, 4, T, T), lambda b, h, i: (h, 0, 0, 0))],
        out_specs=pl.BlockSpec((QT * T, hw), lambda b, h, i: (b * nq + i, h)),
        out_shape=jax.ShapeDtypeStruct((B * S, H * hw), BF16),
        scratch_shapes=[pltpu.VMEM((hw + BF16_TILE_ROWS, S), BF16),
                        pltpu.VMEM((G, T, lanes), F32), pltpu.VMEM((G, T, lanes), F32),
                        pltpu.VMEM((2, 1, lanes), F32),
                        pltpu.VMEM((1, lanes), F32),
                        pltpu.VMEM((hw + BF16_TILE_ROWS, lanes), F32)],
        compiler_params=_params("parallel", "parallel", "arbitrary"),
        name="diff_attention",
    )(lam_params, norm_w.reshape(1, hw), p_d, p_d, p_d, bias)


def _ret_kernel(lg_ref, nw_ref, cos_ref, sin_ref, q_ref, k_ref, v_ref, g_ref, o_ref, st_ref):
    C = q_ref.shape[0]
    dk = RET_KDIM

    @pl.when(pl.program_id(2) == 0)
    def _():
        st_ref[...] = jnp.zeros_like(st_ref)

    lg = lg_ref[0]
    lg1 = lg[:, 0:1]
    cosf, sinf = cos_ref[...], sin_ref[...]

    def rot(x):
        return x * cosf + pltpu.roll(x, dk // 2, 1) * sinf

    qr = rot(q_ref[...])
    kr = rot(k_ref[...]) * (dk ** -0.5)
    rowf = lax.broadcasted_iota(jnp.int32, (C, dk), 0).astype(F32)
    xi = jnp.exp((rowf + 1.0) * lg)
    zeta = jnp.exp((C - 1.0 - rowf) * lg)
    r_i = lax.broadcasted_iota(jnp.int32, (C, C), 0)
    c_i = lax.broadcasted_iota(jnp.int32, (C, C), 1)
    decay = jnp.exp(jnp.where(r_i >= c_i, (r_i - c_i).astype(F32) * lg1, NEG))

    v = v_ref[...]
    state = st_ref[...]
    scores = _dot_nt(qr.astype(BF16), kr.astype(BF16)) * decay
    o = _dot(scores.astype(BF16), v) + _dot((qr * xi).astype(BF16), state.astype(BF16))
    st_ref[...] = jnp.exp(C * lg1) * state + _dot((kr * zeta).T.astype(BF16), v)

    ms = jnp.mean(o * o, axis=-1, keepdims=True)
    g = g_ref[...].astype(F32)
    y = o * lax.rsqrt(ms + RMS_EPS) * nw_ref[...] * (g * _sigmoid(g))
    o_ref[...] = y.astype(o_ref.dtype)


def retention_mixer(p_qk, p_vg, cos_t, sin_t, norm_w, B, S, C=256):
    C = min(C, S)
    nt = S // C
    H, dk, dv = RET_HEADS, RET_KDIM, RET_VDIM
    log_gamma = jnp.log(1.0 - 2.0 ** (-5.0 - jnp.arange(H, dtype=F32)))
    lg = jnp.broadcast_to(log_gamma[:, None, None], (H, 1, LANES))
    return pl.pallas_call(
        _ret_kernel,
        grid=(B, H, nt),
        in_specs=[pl.BlockSpec((1, 1, LANES), lambda b, h, t: (h, 0, 0)),
                  pl.BlockSpec((1, dv), lambda b, h, t: (0, 0)),
                  pl.BlockSpec((C, dk), lambda b, h, t: (t, 0)),
                  pl.BlockSpec((C, dk), lambda b, h, t: (t, 0)),
                  pl.BlockSpec((C, dk), lambda b, h, t: (b * nt + t, h)),
                  pl.BlockSpec((C, dk), lambda b, h, t: (b * nt + t, H + h)),
                  pl.BlockSpec((C, dv), lambda b, h, t: (b * nt + t, h)),
                  pl.BlockSpec((C, dv), lambda b, h, t: (b * nt + t, H + h))],
        out_specs=pl.BlockSpec((C, dv), lambda b, h, t: (b * nt + t, h)),
        out_shape=jax.ShapeDtypeStruct((B * S, H * dv), BF16),
        scratch_shapes=[pltpu.VMEM((dk, dv), F32)],
        compiler_params=_params("parallel", "parallel", "arbitrary"),
        name="retention_mixer",
    )(lg, norm_w.reshape(1, dv), cos_t, sin_t, p_qk, p_qk, p_vg, p_vg)


def kernel(x, attn_norm_w, w_in, lb_logits, hgrn_norm_w, rel_bias, diff_lambda, diff_norm_w,
           ret_norm_w, w_gate_up, b_gate, w_br_hgrn, w_br_diff, w_br_ret, w_o, ffn_norm_w,
           w_ffn_gate, w_ffn_up, w_ffn_down, final_norm_w):
    B, S, D = x.shape
    depth = w_in.shape[0]
    M = B * S
    xs = x.reshape(M, D)

    cos_t, sin_t = rope_tables(S)
    bias = bias_tiles(rel_bias, min(256, S))

    off_d = 4 * HGRN_WIDTH
    off_rqk = off_d + 3 * DIFF_WIDTH
    off_rvg = off_rqk + 2 * RET_QK_WIDTH
    off_gd = off_rvg + 2 * RET_V_WIDTH

    for l in range(depth):
        w_in_l = w_in[l].astype(BF16)
        h = rms_norm_rows(xs, attn_norm_w[l], BF16)
        p_h = matmul_cols(h, w_in_l, 0, 4 * HGRN_WIDTH, F32)
        p_d = matmul_cols(h, w_in_l, off_d, 3 * DIFF_WIDTH, BF16)
        p_rqk = matmul_cols(h, w_in_l, off_rqk, 2 * RET_QK_WIDTH, F32)
        p_rvg = matmul_cols(h, w_in_l, off_rvg, 2 * RET_V_WIDTH, BF16)
        gd = matmul_cols(h, w_in_l, off_gd, GATE_RANK, BF16, tn=GATE_RANK)

        y_h = hgrn_mixer(p_h, lb_logits, hgrn_norm_w[l], l, B, S)
        lam_init = 0.8 - 0.6 * math.exp(-0.3 * l)
        y_d = diff_attention(p_d, bias, diff_lambda[l], diff_norm_w[l], lam_init, B, S)
        y_r = retention_mixer(p_rqk, p_rvg, cos_t, sin_t, ret_norm_w[l], B, S)

        merged = gated_merge(y_h, y_d, y_r, gd,
                             w_br_hgrn[l].astype(BF16), w_br_diff[l].astype(BF16),
                             w_br_ret[l].astype(BF16), w_gate_up[l].astype(BF16), b_gate[l])
        xs = matmul_residual(merged, w_o[l].astype(BF16), xs)

        h2 = rms_norm_rows(xs, ffn_norm_w[l], BF16)
        act = swiglu_up(h2, w_ffn_gate, w_ffn_up, l)
        xs = matmul_residual(act, w_ffn_down[l].astype(BF16), xs, tn=256, a_buffers=1)

    out = rms_norm_rows(xs, final_norm_w, x.dtype)
    return out.reshape(B, S, D)
```

```python
import functools
import math

import jax
import jax.numpy as jnp
from jax import lax
from jax.experimental import pallas as pl
from jax.experimental.pallas import tpu as pltpu

F32 = jnp.float32
BF16 = jnp.bfloat16

HGRN_HEADS = 8
HGRN_DIM = 128
DIFF_HEADS = 8
DIFF_HEAD_DIM = 64
RET_HEADS = 8
RET_KDIM = 128
RET_VDIM = 256
N_BUCKETS = 32
MAX_DISTANCE = 128
ROPE_BASE = 10000.0
GATE_RANK = 256
RMS_EPS = 1e-6
HGRN_WIDTH = HGRN_HEADS * HGRN_DIM
DIFF_WIDTH = DIFF_HEADS * 2 * DIFF_HEAD_DIM
RET_QK_WIDTH = RET_HEADS * RET_KDIM
RET_V_WIDTH = RET_HEADS * RET_VDIM

V7X_VMEM_BYTES = 64 * 1024 * 1024
VMEM_LIMIT = V7X_VMEM_BYTES - 8 * 1024 * 1024
LANES = 128
BF16_TILE_ROWS = 16
F32_TILE_ROWS = 8
LOG2E = math.log2(math.e)

NEG = -1e30
HGRN_CHUNK = 128
HGRN_SUB = 16
DIFF_GROUP = 4
DIFF_QTILES = 2


def _params(*sem):
    return pltpu.CompilerParams(dimension_semantics=sem, vmem_limit_bytes=VMEM_LIMIT)


def _sigmoid(x):
    return 1.0 / (1.0 + jnp.exp(-x))


def _dot(a, b):
    return jnp.dot(a, b, preferred_element_type=F32)


def _dot_nt(a, b):
    return lax.dot_general(a, b, (((1,), (1,)), ((), ())), preferred_element_type=F32)


def _rms_kernel(x_ref, w_ref, o_ref):
    x = x_ref[...]
    ms = jnp.mean(x * x, axis=-1, keepdims=True)
    o_ref[...] = (x * lax.rsqrt(ms + RMS_EPS) * w_ref[...]).astype(o_ref.dtype)


def rms_norm_rows(x, w, out_dtype, tm=256):
    M, D = x.shape
    tm = min(tm, M)
    return pl.pallas_call(
        _rms_kernel,
        grid=(M // tm,),
        in_specs=[pl.BlockSpec((tm, D), lambda i: (i, 0)),
                  pl.BlockSpec((1, D), lambda i: (0, 0))],
        out_specs=pl.BlockSpec((tm, D), lambda i: (i, 0)),
        out_shape=jax.ShapeDtypeStruct((M, D), out_dtype),
        compiler_params=_params("parallel"),
        name="rmsnorm",
    )(x, w.reshape(1, D))


def _lane_partial_sumsq(x):
    sq = x * x
    part = sq[:, 0:LANES]
    for k in range(1, x.shape[1] // LANES):
        part = part + sq[:, k * LANES:(k + 1) * LANES]
    return part


def _row_scale(ss, d_model):
    ms = jnp.sum(ss, axis=-1, keepdims=True) / d_model
    return jnp.broadcast_to(lax.rsqrt(ms + RMS_EPS), ss.shape)


def _prenorm_kernel(x_ref, w_ref, xw_ref, ss_ref):
    x = x_ref[...]
    xw_ref[...] = (x * w_ref[...]).astype(xw_ref.dtype)
    ss_ref[...] = _lane_partial_sumsq(x)


def prenorm(x, w, tm=256):
    M, D = x.shape
    tm = min(tm, M)
    return pl.pallas_call(
        _prenorm_kernel,
        grid=(M // tm,),
        in_specs=[pl.BlockSpec((tm, D), lambda i: (i, 0)),
                  pl.BlockSpec((1, D), lambda i: (0, 0))],
        out_specs=[pl.BlockSpec((tm, D), lambda i: (i, 0)),
                   pl.BlockSpec((tm, LANES), lambda i: (i, 0))],
        out_shape=[jax.ShapeDtypeStruct((M, D), BF16), jax.ShapeDtypeStruct((M, LANES), F32)],
        compiler_params=_params("parallel"),
        name="prenorm",
    )(x, w.reshape(1, D))


def _mm_kernel(a_ref, ss_ref, b_ref, o_ref, r_ref):
    @pl.when(pl.program_id(1) == 0)
    def _():
        r_ref[...] = _row_scale(ss_ref[...], a_ref.shape[1])
    scale = jnp.tile(r_ref[...], (1, o_ref.shape[1] // LANES))
    o_ref[...] = (_dot(a_ref[...], b_ref[...]) * scale).astype(o_ref.dtype)


def matmul_cols(a, ss, b, col_off, n, out_dtype, tm=1024, tn=1024):
    M, K = a.shape
    tm, tn = min(tm, M), min(tn, n)
    assert col_off % tn == 0 and n % tn == 0 and M % tm == 0
    off = col_off // tn
    return pl.pallas_call(
        _mm_kernel,
        grid=(M // tm, n // tn),
        in_specs=[pl.BlockSpec((tm, K), lambda i, j: (i, 0)),
                  pl.BlockSpec((tm, LANES), lambda i, j: (i, 0)),
                  pl.BlockSpec((K, tn), lambda i, j: (0, j + off))],
        out_specs=pl.BlockSpec((tm, tn), lambda i, j: (i, j)),
        out_shape=jax.ShapeDtypeStruct((M, n), out_dtype),
        scratch_shapes=[pltpu.VMEM((tm, LANES), F32)],
        compiler_params=_params("parallel", "arbitrary"),
        name="matmul_cols",
    )(a, ss, b)


def _mm_res_kernel(a_ref, b_ref, x_ref, o_ref):
    o_ref[...] = x_ref[...] + _dot(a_ref[...], b_ref[...])


def _mm_res_norm_kernel(a_ref, b_ref, x_ref, w_ref, o_ref, xw_ref, ss_ref):
    y = x_ref[...] + _dot(a_ref[...], b_ref[...])
    o_ref[...] = y
    xw_ref[...] = (y * w_ref[...]).astype(xw_ref.dtype)
    part = _lane_partial_sumsq(y)

    @pl.when(pl.program_id(1) == 0)
    def _():
        ss_ref[...] = part

    @pl.when(pl.program_id(1) > 0)
    def _():
        ss_ref[...] = ss_ref[...] + part


def matmul_residual(a, b, x, next_norm_w=None, in_place=True, tm=1024, tn=512, a_buffers=2):
    M, K = a.shape
    N = b.shape[1]
    tm, tn = min(tm, M), min(tn, N)
    assert M % tm == 0 and N % tn == 0
    tile = pl.BlockSpec((tm, tn), lambda i, j: (i, j))
    in_specs = [pl.BlockSpec((tm, K), lambda i, j: (i, 0), pipeline_mode=pl.Buffered(a_buffers)),
                pl.BlockSpec((K, tn), lambda i, j: (0, j)),
                tile]
    aliases = {2: 0} if in_place else {}
    if next_norm_w is None:
        return pl.pallas_call(
            _mm_res_kernel,
            grid=(M // tm, N // tn),
            in_specs=in_specs,
            out_specs=tile,
            out_shape=jax.ShapeDtypeStruct((M, N), F32),
            input_output_aliases=aliases,
            compiler_params=_params("parallel", "arbitrary"),
            name="matmul_residual",
        )(a, b, x)
    return pl.pallas_call(
        _mm_res_norm_kernel,
        grid=(M // tm, N // tn),
        in_specs=in_specs + [pl.BlockSpec((1, tn), lambda i, j: (0, j))],
        out_specs=[tile, tile, pl.BlockSpec((tm, LANES), lambda i, j: (i, 0))],
        out_shape=[jax.ShapeDtypeStruct((M, N), F32), jax.ShapeDtypeStruct((M, N), BF16),
                   jax.ShapeDtypeStruct((M, LANES), F32)],
        input_output_aliases=aliases,
        compiler_params=_params("parallel", "arbitrary"),
        name="matmul_residual_norm",
    )(a, b, x, next_norm_w.reshape(1, N))


def _swiglu_kernel(a_ref, ss_ref, wg_ref, wu_ref, o_ref, r_ref):
    @pl.when(pl.program_id(1) == 0)
    def _():
        r_ref[...] = _row_scale(ss_ref[...], a_ref.shape[1])
    scale = jnp.tile(r_ref[...], (1, o_ref.shape[1] // LANES))
    a = a_ref[...]
    g = _dot(a, wg_ref[...].astype(BF16)) * scale
    u = _dot(a, wu_ref[...].astype(BF16)) * scale
    o_ref[...] = (g * _sigmoid(g) * u).astype(o_ref.dtype)


def swiglu_up(a, ss, wg, wu, layer, tm=2048, tn=256):
    M, K = a.shape
    N = wg.shape[2]
    tm, tn = min(tm, M), min(tn, N)
    assert M % tm == 0 and N % tn == 0
    return pl.pallas_call(
        _swiglu_kernel,
        grid=(M // tm, N // tn),
        in_specs=[pl.BlockSpec((tm, K), lambda i, j: (i, 0), pipeline_mode=pl.Buffered(1)),
                  pl.BlockSpec((tm, LANES), lambda i, j: (i, 0)),
                  pl.BlockSpec((None, K, tn), lambda i, j: (layer, 0, j)),
                  pl.BlockSpec((None, K, tn), lambda i, j: (layer, 0, j))],
        out_specs=pl.BlockSpec((tm, tn), lambda i, j: (i, j)),
        out_shape=jax.ShapeDtypeStruct((M, N), BF16),
        scratch_shapes=[pltpu.VMEM((tm, LANES), F32)],
        compiler_params=_params("parallel", "arbitrary"),
        name="swiglu_up",
    )(a, ss, wg, wu)


def _merge_kernel(yh_ref, yd_ref, yr_ref, gd_ref, wh_ref, wd_ref, wr_ref,
                  wgh_ref, wgd_ref, wgr_ref, bh_ref, bd_ref, br_ref, o_ref):
    gd = gd_ref[...]

    def branch(y_ref, w_ref, wg_ref, b_ref):
        gate = _sigmoid(_dot(gd, wg_ref[...]) + b_ref[...])
        return gate * _dot(y_ref[...], w_ref[...])

    acc = branch(yh_ref, wh_ref, wgh_ref, bh_ref)
    acc = acc + branch(yd_ref, wd_ref, wgd_ref, bd_ref)
    acc = acc + branch(yr_ref, wr_ref, wgr_ref, br_ref)
    o_ref[...] = acc.astype(o_ref.dtype)


def gated_merge(yh, yd, yr, gd, wh, wd, wr, wg, bg, tm=1024, tn=512):
    M = yh.shape[0]
    D = wh.shape[1]
    tm, tn = min(tm, M), min(tn, D)
    assert M % tm == 0 and D % tn == 0
    nb = D // tn
    row = lambda width: pl.BlockSpec((tm, width), lambda i, j: (i, 0))
    wcol = lambda k: pl.BlockSpec((k, tn), lambda i, j: (0, j))
    gcol = lambda br: pl.BlockSpec((GATE_RANK, tn), lambda i, j: (0, j + br * nb))
    bcol = lambda br: pl.BlockSpec((1, tn), lambda i, j: (0, j + br * nb))
    bg2 = bg.reshape(1, -1)
    return pl.pallas_call(
        _merge_kernel,
        grid=(M // tm, nb),
        in_specs=[row(yh.shape[1]), row(yd.shape[1]), row(yr.shape[1]), row(gd.shape[1]),
                  wcol(wh.shape[0]), wcol(wd.shape[0]), wcol(wr.shape[0]),
                  gcol(0), gcol(1), gcol(2), bcol(0), bcol(1), bcol(2)],
        out_specs=pl.BlockSpec((tm, tn), lambda i, j: (i, j)),
        out_shape=jax.ShapeDtypeStruct((M, D), BF16),
        compiler_params=_params("parallel", "arbitrary"),
        name="gated_merge",
    )(yh, yd, yr, gd, wh, wd, wr, wg, wg, wg, bg2, bg2, bg2)


def _rope_table_kernel(inv_ref, cos_ref, sin_ref):
    T = cos_ref.shape[0]
    pos = (lax.broadcasted_iota(jnp.int32, (T, LANES), 0) + pl.program_id(0) * T).astype(F32)
    ang = pos * inv_ref[...]
    lane = lax.broadcasted_iota(jnp.int32, (T, LANES), 1)
    cos_ref[...] = jnp.cos(ang)
    sin_ref[...] = jnp.where(lane < LANES // 2, -jnp.sin(ang), jnp.sin(ang))


def rope_tables(S, T=256):
    T = min(T, S)
    half = RET_KDIM // 2
    inv = 1.0 / (ROPE_BASE ** (jnp.arange(half, dtype=F32) / half))
    inv2 = jnp.concatenate([inv, inv]).reshape(1, LANES)
    return pl.pallas_call(
        _rope_table_kernel,
        grid=(S // T,),
        in_specs=[pl.BlockSpec((1, LANES), lambda i: (0, 0))],
        out_specs=[pl.BlockSpec((T, LANES), lambda i: (i, 0))] * 2,
        out_shape=[jax.ShapeDtypeStruct((S, LANES), F32)] * 2,
        compiler_params=_params("parallel"),
        name="rope_tables",
    )(inv2)


def _bias_tile_kernel(tab_ref, o_ref):
    h = pl.program_id(0)
    T = o_ref.shape[2]
    j = lax.broadcasted_iota(jnp.int32, (T, T), 0)
    i = lax.broadcasted_iota(jnp.int32, (T, T), 1)
    max_exact = N_BUCKETS // 2
    o_ref[0, 3] = jnp.full((T, T), NEG, F32)
    for d in range(3):
        n = jnp.maximum(i - j + d * T, 0)
        nf = jnp.maximum(n, 1).astype(F32)
        large = max_exact + (jnp.log(nf / max_exact) / math.log(MAX_DISTANCE / max_exact)
                             * (N_BUCKETS - max_exact)).astype(jnp.int32)
        large = jnp.minimum(large, N_BUCKETS - 1)
        bucket = jnp.where(n < max_exact, n, large)
        val = jnp.zeros((T, T), F32)
        for b in range(N_BUCKETS):
            val = jnp.where(bucket == b, tab_ref[h, b], val)
        val = (val - tab_ref[h, N_BUCKETS - 1]) * LOG2E
        if d == 0:
            val = jnp.where(j > i, NEG, val)
        o_ref[0, d] = val


def bias_tiles(rel_bias, T):
    H = rel_bias.shape[1]
    return pl.pallas_call(
        _bias_tile_kernel,
        grid=(H,),
        in_specs=[pl.BlockSpec(memory_space=pltpu.SMEM)],
        out_specs=pl.BlockSpec((1, 4, T, T), lambda h: (h, 0, 0, 0)),
        out_shape=jax.ShapeDtypeStruct((H, 4, T, T), F32),
        compiler_params=_params("parallel"),
        name="bias_tiles",
    )(rel_bias.T)


def _hgrn_kernel(lbl_ref, nw_ref, q_ref, f_ref, i_ref, g_ref, o_ref,
                 st_ref, p_ref, r_ref, ol_ref, *, layer):
    T = q_ref.shape[0]
    C, SUB = HGRN_CHUNK, HGRN_SUB

    @pl.when(pl.program_id(2) == 0)
    def _():
        st_ref[...] = jnp.zeros_like(st_ref)

    lg = lbl_ref[...]
    e = jnp.exp(lg - jnp.max(lg, axis=0, keepdims=True))
    sm = e / jnp.sum(e, axis=0, keepdims=True)
    csum = sm[0:1]
    for r in range(1, layer + 1):
        csum = csum + sm[r:r + 1]
    lb = csum - sm[0:1]

    forget = lb + (1.0 - lb) * _sigmoid(f_ref[...])
    kf = 1.0 - forget
    logf = jnp.log(forget)
    q = q_ref[...]
    qf = q * _sigmoid(q) * (HGRN_DIM ** -0.5)

    r_i = lax.broadcasted_iota(jnp.int32, (T, T), 0)
    c_i = lax.broadcasted_iota(jnp.int32, (T, T), 1)
    shift = C.bit_length() - 1
    same_chunk = (r_i >> shift) == (c_i >> shift)
    tri = jnp.where(c_i <= r_i, jnp.where(same_chunk, 1.0, 0.0), 0.0).astype(BF16)
    hi = logf.astype(BF16)
    rem = logf - hi.astype(F32)
    mid = rem.astype(BF16)
    lo = (rem - mid.astype(F32)).astype(BF16)
    bcum = _dot(tri, hi) + _dot(tri, mid) + _dot(tri, lo)

    iv_all = i_ref[...]
    g_all = g_ref[...]
    nw = nw_ref[...]
    ones = jnp.ones((HGRN_DIM, HGRN_DIM), BF16)
    row_c = lax.broadcasted_iota(jnp.int32, (C, HGRN_DIM), 0)
    row_s = lax.broadcasted_iota(jnp.int32, (SUB, HGRN_DIM), 0)

    n_sub = T // SUB
    for sb in range(n_sub):
        sub = slice(sb * SUB, (sb + 1) * SUB)
        bi, qi = bcum[sub], qf[sub]
        for s in range(SUB):
            r = sb * SUB + s
            top = (s // F32_TILE_ROWS) * F32_TILE_ROWS
            dec = jnp.exp(jnp.where(row_s[top:] >= s, bi[top:] - bcum[r:r + 1], NEG))
            p = qi[top:] * kf[r:r + 1] * dec
            if top:
                p = jnp.concatenate([jnp.zeros((top, HGRN_DIM), F32), p], axis=0)
            p_ref[r * SUB:(r + 1) * SUB, :] = p.astype(BF16)
    r_ref[...] = _dot(p_ref[...], ones)

    n_blk = C // SUB
    for c in range(T // C):
        c0 = c * C
        b, kc = bcum[c0:c0 + C], kf[c0:c0 + C]
        qts, kts = [], []
        for j in range(1, n_blk):
            lo_r = j * SUB
            ref = b[lo_r - 1:lo_r]
            qts.append((qf[c0 + lo_r:c0 + lo_r + SUB] * jnp.exp(b[lo_r:lo_r + SUB] - ref)).astype(BF16))
            kts.append((kc * jnp.exp(jnp.where(row_c < lo_r, ref - b, NEG))).astype(BF16))
        att_all = _dot_nt(jnp.concatenate(qts, axis=0), jnp.concatenate(kts, axis=0))
        att = jnp.concatenate([att_all[j * SUB:(j + 1) * SUB, j * C:(j + 1) * C]
                               for j in range(n_blk - 1)], axis=0)
        off = _dot(att.astype(BF16), iv_all[c0:c0 + C].astype(BF16))
        for j in range(n_blk):
            lo_r = c0 + j * SUB
            oi = r_ref[lo_r * SUB:(lo_r + 1) * SUB, :] * iv_all[lo_r:lo_r + 1]
            for s in range(1, SUB):
                r = lo_r + s
                top = (s // F32_TILE_ROWS) * F32_TILE_ROWS
                term = r_ref[r * SUB + top:(r + 1) * SUB, :] * iv_all[r:r + 1]
                oi = oi + term if top == 0 else jnp.concatenate([oi[:top], oi[top:] + term], axis=0)
            if j > 0:
                oi = oi + off[(j - 1) * SUB:j * SUB]
            ol_ref[lo_r:lo_r + SUB, :] = oi

    state_t = st_ref[...]
    for c in range(T // C):
        rows = slice(c * C, (c + 1) * C)
        b, qc, kc, iv = bcum[rows], qf[rows], kf[rows], iv_all[rows]
        o = ol_ref[rows, :] + _dot_nt((qc * jnp.exp(b)).astype(BF16), state_t.astype(BF16))
        b_last = b[C - 1:C]
        khat = (kc * jnp.exp(b_last - b)).astype(BF16)
        state_t = state_t * jnp.exp(b_last) + _dot(iv.T.astype(BF16), khat)
        ms = jnp.mean(o * o, axis=-1, keepdims=True)
        gc = g_all[rows]
        y = o * lax.rsqrt(ms + RMS_EPS) * nw * (gc * _sigmoid(gc))
        o_ref[rows, :] = y.astype(o_ref.dtype)
    st_ref[...] = state_t


def hgrn_mixer(p_h, lb_logits, norm_w, layer, B, S, T=256):
    T = min(T, S)
    nt = S // T
    H, dk = HGRN_HEADS, HGRN_DIM
    col = lambda grp: pl.BlockSpec((T, dk), lambda b, h, t: (b * nt + t, grp * H + h))
    return pl.pallas_call(
        functools.partial(_hgrn_kernel, layer=layer),
        grid=(B, H, nt),
        in_specs=[pl.BlockSpec((lb_logits.shape[0], dk), lambda b, h, t: (0, h)),
                  pl.BlockSpec((1, dk), lambda b, h, t: (0, 0)),
                  col(0), col(1), col(2), col(3)],
        out_specs=pl.BlockSpec((T, dk), lambda b, h, t: (b * nt + t, h)),
        out_shape=jax.ShapeDtypeStruct((B * S, H * dk), BF16),
        scratch_shapes=[pltpu.VMEM((dk, dk), F32),
                        pltpu.VMEM((T * HGRN_SUB, dk), BF16), pltpu.VMEM((T * HGRN_SUB, dk), F32),
                        pltpu.VMEM((T, dk), F32)],
        compiler_params=_params("parallel", "parallel", "arbitrary"),
        name="hgrn_mixer",
    )(lb_logits, norm_w.reshape(1, dk), p_h, p_h, p_h, p_h)


def _diff_kernel(lam_ref, nw_ref, q_ref, k_ref, v_ref, bias_ref, o_ref,
                 vt_ref, s0_ref, s1_ref, gm_ref, m_ref, acc_ref, *, lam_init):
    QT = DIFF_QTILES
    T = q_ref.shape[0] // QT
    S = k_ref.shape[0]
    dh = DIFF_HEAD_DIM
    G = DIFF_GROUP
    n_groups = S // (G * T)
    qi = pl.program_id(2) * QT
    ng = (qi + QT - 1) // G + 1

    @pl.when(qi == 0)
    def _():
        def body(c, carry):
            start = pl.multiple_of(c * T, T)
            vt_ref[0:2 * dh, pl.ds(start, T)] = v_ref[pl.ds(start, T), :].astype(F32).T.astype(BF16)
            return carry
        lax.fori_loop(0, S // T, body, 0)
        pad_row = lax.broadcasted_iota(jnp.int32, (BF16_TILE_ROWS, S), 0)
        vt_ref[2 * dh:, :] = jnp.where(pad_row == 0, 1.0, 0.0).astype(BF16)

    row = lax.broadcasted_iota(jnp.int32, (2 * dh, T), 0)
    cols = []
    for t in range(QT):
        qt = q_ref[t * T:(t + 1) * T, :].astype(F32).T * (dh ** -0.5 * LOG2E)
        cols += [jnp.where(row < dh, qt, 0.0), jnp.where(row >= dh, qt, 0.0)]
    q2 = jnp.concatenate(cols, axis=1).astype(BF16)

    slots = (s0_ref, s1_ref)

    def scores_group(g, near):
        slot = slots[g % 2]
        gm = None
        for u in range(G):
            kt = g * G + u
            s = _dot(k_ref[kt * T:(kt + 1) * T, :], q2)
            if near:
                tiles = []
                for t in range(QT):
                    d = qi + t - kt
                    tiles += [bias_ref[0, jnp.where(d < 0, 3, jnp.minimum(d, 2))]] * 2
                s = s + jnp.concatenate(tiles, axis=1)
            slot[u] = s
            cm = jnp.max(s, axis=0, keepdims=True)
            gm = cm if gm is None else jnp.maximum(gm, cm)
        gm_ref[g % 2] = gm

    def softmax_group(g):
        slot = slots[g % 2]
        m_old = m_ref[...]
        m_new = jnp.maximum(m_old, gm_ref[g % 2])
        alpha = jnp.exp2(m_old - m_new)
        ps = [jnp.exp2(slot[u] - m_new).astype(BF16) for u in range(G)]
        pv = _dot(vt_ref[:, g * G * T:(g + 1) * G * T], jnp.concatenate(ps, axis=0))
        m_ref[...] = m_new
        acc_ref[...] = alpha * acc_ref[...] + pv

    m_ref[...] = jnp.full(m_ref.shape, NEG, F32)
    acc_ref[...] = jnp.zeros(acc_ref.shape, F32)
    n_far_groups = jnp.maximum(qi - 1, 0) // G

    def stage(cond, g_scores, g_softmax):
        for near in (False, True):
            is_near = g_scores >= n_far_groups
            @pl.when(jnp.logical_and(cond, is_near if near else jnp.logical_not(is_near)))
            def _():
                scores_group(g_scores, near)
                if g_softmax is not None:
                    softmax_group(g_softmax)

    stage(True, 0, None)
    for g in range(n_groups):
        if g + 1 < n_groups:
            stage(g < ng - 1, g + 1, g)

        @pl.when(g == ng - 1)
        def _():
            softmax_group(g)

    lp = lam_ref[...]
    lam = (jnp.exp(jnp.sum(lp[0:1] * lp[1:2], axis=-1, keepdims=True))
           - jnp.exp(jnp.sum(lp[2:3] * lp[3:4], axis=-1, keepdims=True)) + lam_init)
    acc = acc_ref[...]
    w = acc[:2 * dh] / acc[2 * dh:2 * dh + 1]
    for t in range(QT):
        c0 = 2 * t * T
        out = (w[:, c0:c0 + T] - lam * w[:, c0 + T:c0 + 2 * T]).T
        ms = jnp.mean(out * out, axis=-1, keepdims=True)
        y = out * lax.rsqrt(ms + RMS_EPS) * nw_ref[...] * (1.0 - lam_init)
        o_ref[t * T:(t + 1) * T, :] = y.astype(o_ref.dtype)


def diff_attention(p_d, bias, lam_params, norm_w, lam_init, B, S):
    T = bias.shape[2]
    G = DIFF_GROUP
    QT = DIFF_QTILES
    assert T >= MAX_DISTANCE and S % (G * T) == 0 and G % QT == 0
    nq = S // (QT * T)
    lanes = 2 * QT * T
    H, hw = DIFF_HEADS, 2 * DIFF_HEAD_DIM
    return pl.pallas_call(
        functools.partial(_diff_kernel, lam_init=lam_init),
        grid=(B, H, nq),
        in_specs=[pl.BlockSpec(lam_params.shape, lambda b, h, i: (0, 0)),
                  pl.BlockSpec((1, hw), lambda b, h, i: (0, 0)),
                  pl.BlockSpec((QT * T, hw), lambda b, h, i: (b * nq + i, h)),
                  pl.BlockSpec((S, hw), lambda b, h, i: (b, H + h)),
                  pl.BlockSpec((S, hw), lambda b, h, i: (b, 2 * H + h)),
                  pl.BlockSpec((1, 4, T, T), lambda b, h, i: (h, 0, 0, 0))],
        out_specs=pl.BlockSpec((QT * T, hw), lambda b, h, i: (b * nq + i, h)),
        out_shape=jax.ShapeDtypeStruct((B * S, H * hw), BF16),
        scratch_shapes=[pltpu.VMEM((hw + BF16_TILE_ROWS, S), BF16),
                        pltpu.VMEM((G, T, lanes), F32), pltpu.VMEM((G, T, lanes), F32),
                        pltpu.VMEM((2, 1, lanes), F32),
                        pltpu.VMEM((1, lanes), F32),
                        pltpu.VMEM((hw + BF16_TILE_ROWS, lanes), F32)],
        compiler_params=_params("parallel", "parallel", "arbitrary"),
        name="diff_attention",
    )(lam_params, norm_w.reshape(1, hw), p_d, p_d, p_d, bias)


def _ret_kernel(lg_ref, nw_ref, cos_ref, sin_ref, q_ref, k_ref, v_ref, g_ref, o_ref, st_ref):
    C = q_ref.shape[0]
    dk = RET_KDIM

    @pl.when(pl.program_id(2) == 0)
    def _():
        st_ref[...] = jnp.zeros_like(st_ref)

    lg = lg_ref[0]
    lg1 = lg[:, 0:1]
    cosf, sinf = cos_ref[...], sin_ref[...]

    def rot(x):
        return x * cosf + pltpu.roll(x, dk // 2, 1) * sinf

    qr = rot(q_ref[...])
    kr = rot(k_ref[...]) * (dk ** -0.5)
    rowf = lax.broadcasted_iota(jnp.int32, (C, dk), 0).astype(F32)
    xi = jnp.exp((rowf + 1.0) * lg)
    zeta = jnp.exp((C - 1.0 - rowf) * lg)
    r_i = lax.broadcasted_iota(jnp.int32, (C, C), 0)
    c_i = lax.broadcasted_iota(jnp.int32, (C, C), 1)
    decay = jnp.exp(jnp.where(r_i >= c_i, (r_i - c_i).astype(F32) * lg1, NEG))

    v = v_ref[...]
    state = st_ref[...]
    scores = _dot_nt(qr.astype(BF16), kr.astype(BF16)) * decay
    o = _dot(scores.astype(BF16), v) + _dot((qr * xi).astype(BF16), state.astype(BF16))
    st_ref[...] = jnp.exp(C * lg1) * state + _dot((kr * zeta).T.astype(BF16), v)

    ms = jnp.mean(o * o, axis=-1, keepdims=True)
    g = g_ref[...].astype(F32)
    y = o * lax.rsqrt(ms + RMS_EPS) * nw_ref[...] * (g * _sigmoid(g))
    o_ref[...] = y.astype(o_ref.dtype)


def retention_mixer(p_qk, p_vg, cos_t, sin_t, norm_w, B, S, C=256):
    C = min(C, S)
    nt = S // C
    H, dk, dv = RET_HEADS, RET_KDIM, RET_VDIM
    log_gamma = jnp.log(1.0 - 2.0 ** (-5.0 - jnp.arange(H, dtype=F32)))
    lg = jnp.broadcast_to(log_gamma[:, None, None], (H, 1, LANES))
    return pl.pallas_call(
        _ret_kernel,
        grid=(B, H, nt),
        in_specs=[pl.BlockSpec((1, 1, LANES), lambda b, h, t: (h, 0, 0)),
                  pl.BlockSpec((1, dv), lambda b, h, t: (0, 0)),
                  pl.BlockSpec((C, dk), lambda b, h, t: (t, 0)),
                  pl.BlockSpec((C, dk), lambda b, h, t: (t, 0)),
                  pl.BlockSpec((C, dk), lambda b, h, t: (b * nt + t, h)),
                  pl.BlockSpec((C, dk), lambda b, h, t: (b * nt + t, H + h)),
                  pl.BlockSpec((C, dv), lambda b, h, t: (b * nt + t, h)),
                  pl.BlockSpec((C, dv), lambda b, h, t: (b * nt + t, H + h))],
        out_specs=pl.BlockSpec((C, dv), lambda b, h, t: (b * nt + t, h)),
        out_shape=jax.ShapeDtypeStruct((B * S, H * dv), BF16),
        scratch_shapes=[pltpu.VMEM((dk, dv), F32)],
        compiler_params=_params("parallel", "parallel", "arbitrary"),
        name="retention_mixer",
    )(lg, norm_w.reshape(1, dv), cos_t, sin_t, p_qk, p_qk, p_vg, p_vg)


def kernel(x, attn_norm_w, w_in, lb_logits, hgrn_norm_w, rel_bias, diff_lambda, diff_norm_w,
           ret_norm_w, w_gate_up, b_gate, w_br_hgrn, w_br_diff, w_br_ret, w_o, ffn_norm_w,
           w_ffn_gate, w_ffn_up, w_ffn_down, final_norm_w):
    B, S, D = x.shape
    depth = w_in.shape[0]
    M = B * S
    xs = x.reshape(M, D)

    cos_t, sin_t = rope_tables(S)
    bias = bias_tiles(rel_bias, min(256, S))

    off_d = 4 * HGRN_WIDTH
    off_rqk = off_d + 3 * DIFF_WIDTH
    off_rvg = off_rqk + 2 * RET_QK_WIDTH
    off_gd = off_rvg + 2 * RET_V_WIDTH

    h, ss = prenorm(xs, attn_norm_w[0])
    for l in range(depth):
        w_in_l = w_in[l].astype(BF16)
        p_h = matmul_cols(h, ss, w_in_l, 0, 4 * HGRN_WIDTH, F32)
        p_d = matmul_cols(h, ss, w_in_l, off_d, 3 * DIFF_WIDTH, BF16)
        p_rqk = matmul_cols(h, ss, w_in_l, off_rqk, 2 * RET_QK_WIDTH, F32)
        p_rvg = matmul_cols(h, ss, w_in_l, off_rvg, 2 * RET_V_WIDTH, BF16)
        gd = matmul_cols(h, ss, w_in_l, off_gd, GATE_RANK, BF16, tn=GATE_RANK)

        y_h = hgrn_mixer(p_h, lb_logits, hgrn_norm_w[l], l, B, S)
        lam_init = 0.8 - 0.6 * math.exp(-0.3 * l)
        y_d = diff_attention(p_d, bias, diff_lambda[l], diff_norm_w[l], lam_init, B, S)
        y_r = retention_mixer(p_rqk, p_rvg, cos_t, sin_t, ret_norm_w[l], B, S)

        merged = gated_merge(y_h, y_d, y_r, gd,
                             w_br_hgrn[l].astype(BF16), w_br_diff[l].astype(BF16),
                             w_br_ret[l].astype(BF16), w_gate_up[l].astype(BF16), b_gate[l])
        xs, h2, ss2 = matmul_residual(merged, w_o[l].astype(BF16), xs, ffn_norm_w[l], in_place=l > 0)
        act = swiglu_up(h2, ss2, w_ffn_gate, w_ffn_up, l)
        w_down = w_ffn_down[l].astype(BF16)
        if l + 1 < depth:
            xs, h, ss = matmul_residual(act, w_down, xs, attn_norm_w[l + 1], tn=256, a_buffers=1)
        else:
            xs = matmul_residual(act, w_down, xs, tn=256, a_buffers=1)

    out = rms_norm_rows(xs, final_norm_w, x.dtype)
    return out.reshape(B, S, D)
```

```python
import functools
import math

import jax
import jax.numpy as jnp
from jax import lax
from jax.experimental import pallas as pl
from jax.experimental.pallas import tpu as pltpu

F32 = jnp.float32
BF16 = jnp.bfloat16

HGRN_HEADS = 8
HGRN_DIM = 128
DIFF_HEADS = 8
DIFF_HEAD_DIM = 64
RET_HEADS = 8
RET_KDIM = 128
RET_VDIM = 256
N_BUCKETS = 32
MAX_DISTANCE = 128
ROPE_BASE = 10000.0
GATE_RANK = 256
RMS_EPS = 1e-6
HGRN_WIDTH = HGRN_HEADS * HGRN_DIM
DIFF_WIDTH = DIFF_HEADS * 2 * DIFF_HEAD_DIM
RET_QK_WIDTH = RET_HEADS * RET_KDIM
RET_V_WIDTH = RET_HEADS * RET_VDIM

V7X_VMEM_BYTES = 64 * 1024 * 1024
VMEM_LIMIT = V7X_VMEM_BYTES - 8 * 1024 * 1024
LANES = 128
BF16_TILE_ROWS = 16
F32_TILE_ROWS = 8
LOG2E = math.log2(math.e)

NEG = -1e30
HGRN_CHUNK = 128
HGRN_SUB = 16
DIFF_GROUP = 4
DIFF_QTILES = 2
RET_STEP_HEADS = 4
HGRN_STEP_HEADS = 4


def _params(*sem):
    return pltpu.CompilerParams(dimension_semantics=sem, vmem_limit_bytes=VMEM_LIMIT)


def _sigmoid(x):
    return 1.0 / (1.0 + jnp.exp(-x))


def _dot(a, b):
    return jnp.dot(a, b, preferred_element_type=F32)


def _dot_nt(a, b):
    return lax.dot_general(a, b, (((1,), (1,)), ((), ())), preferred_element_type=F32)


def _rms_kernel(x_ref, w_ref, o_ref):
    x = x_ref[...]
    ms = jnp.mean(x * x, axis=-1, keepdims=True)
    o_ref[...] = (x * lax.rsqrt(ms + RMS_EPS) * w_ref[...]).astype(o_ref.dtype)


def rms_norm_rows(x, w, out_dtype, tm=256):
    M, D = x.shape
    tm = min(tm, M)
    return pl.pallas_call(
        _rms_kernel,
        grid=(M // tm,),
        in_specs=[pl.BlockSpec((tm, D), lambda i: (i, 0)),
                  pl.BlockSpec((1, D), lambda i: (0, 0))],
        out_specs=pl.BlockSpec((tm, D), lambda i: (i, 0)),
        out_shape=jax.ShapeDtypeStruct((M, D), out_dtype),
        compiler_params=_params("parallel"),
        name="rmsnorm",
    )(x, w.reshape(1, D))


def _cast_kernel(w_ref, o_ref):
    o_ref[...] = w_ref[...].astype(o_ref.dtype)


def layer_weight_bf16(w, layer, block_bytes=8 * 1024 * 1024):
    _, R, C = w.shape
    rb = R
    while rb * C * 4 > block_bytes and rb % (2 * BF16_TILE_ROWS) == 0:
        rb //= 2
    assert R % rb == 0
    return pl.pallas_call(
        _cast_kernel,
        grid=(R // rb,),
        in_specs=[pl.BlockSpec((None, rb, C), lambda i: (layer, i, 0))],
        out_specs=pl.BlockSpec((rb, C), lambda i: (i, 0)),
        out_shape=jax.ShapeDtypeStruct((R, C), BF16),
        compiler_params=_params("parallel"),
        name="weight_bf16",
    )(w)


def _lane_partial_sumsq(x):
    sq = x * x
    part = sq[:, 0:LANES]
    for k in range(1, x.shape[1] // LANES):
        part = part + sq[:, k * LANES:(k + 1) * LANES]
    return part


def _row_scale(ss, d_model):
    ms = jnp.sum(ss, axis=-1, keepdims=True) / d_model
    return jnp.broadcast_to(lax.rsqrt(ms + RMS_EPS), ss.shape)


def _prenorm_kernel(x_ref, w_ref, xw_ref, ss_ref):
    x = x_ref[...]
    xw_ref[...] = (x * w_ref[...]).astype(xw_ref.dtype)
    ss_ref[...] = _lane_partial_sumsq(x)


def prenorm(x, w, tm=256):
    M, D = x.shape
    tm = min(tm, M)
    return pl.pallas_call(
        _prenorm_kernel,
        grid=(M // tm,),
        in_specs=[pl.BlockSpec((tm, D), lambda i: (i, 0)),
                  pl.BlockSpec((1, D), lambda i: (0, 0))],
        out_specs=[pl.BlockSpec((tm, D), lambda i: (i, 0)),
                   pl.BlockSpec((tm, LANES), lambda i: (i, 0))],
        out_shape=[jax.ShapeDtypeStruct((M, D), BF16), jax.ShapeDtypeStruct((M, LANES), F32)],
        compiler_params=_params("parallel"),
        name="prenorm",
    )(x, w.reshape(1, D))


def _mm_kernel(a_ref, ss_ref, b_ref, o_ref, r_ref):
    @pl.when(pl.program_id(1) == 0)
    def _():
        r_ref[...] = _row_scale(ss_ref[...], a_ref.shape[1])
    scale = jnp.tile(r_ref[...], (1, o_ref.shape[1] // LANES))
    o_ref[...] = (_dot(a_ref[...], b_ref[...]) * scale).astype(o_ref.dtype)


def matmul_cols(a, ss, b, col_off, n, out_dtype, tm=1024, tn=1024):
    M, K = a.shape
    tm, tn = min(tm, M), min(tn, n)
    assert col_off % tn == 0 and n % tn == 0 and M % tm == 0
    off = col_off // tn
    return pl.pallas_call(
        _mm_kernel,
        grid=(M // tm, n // tn),
        in_specs=[pl.BlockSpec((tm, K), lambda i, j: (i, 0)),
                  pl.BlockSpec((tm, LANES), lambda i, j: (i, 0)),
                  pl.BlockSpec((K, tn), lambda i, j: (0, j + off))],
        out_specs=pl.BlockSpec((tm, tn), lambda i, j: (i, j)),
        out_shape=jax.ShapeDtypeStruct((M, n), out_dtype),
        scratch_shapes=[pltpu.VMEM((tm, LANES), F32)],
        compiler_params=_params("parallel", "arbitrary"),
        name="matmul_cols",
    )(a, ss, b)


def _mm_res_kernel(a_ref, b_ref, x_ref, o_ref):
    o_ref[...] = x_ref[...] + _dot(a_ref[...], b_ref[...])


def _mm_res_norm_kernel(a_ref, b_ref, x_ref, w_ref, o_ref, xw_ref, ss_ref):
    y = x_ref[...] + _dot(a_ref[...], b_ref[...])
    o_ref[...] = y
    xw_ref[...] = (y * w_ref[...]).astype(xw_ref.dtype)
    part = _lane_partial_sumsq(y)

    @pl.when(pl.program_id(1) == 0)
    def _():
        ss_ref[...] = part

    @pl.when(pl.program_id(1) > 0)
    def _():
        ss_ref[...] = ss_ref[...] + part


def matmul_residual(a, b, x, next_norm_w=None, in_place=True, tm=1024, tn=512, a_buffers=2):
    M, K = a.shape
    N = b.shape[1]
    tm, tn = min(tm, M), min(tn, N)
    assert M % tm == 0 and N % tn == 0
    tile = pl.BlockSpec((tm, tn), lambda i, j: (i, j))
    in_specs = [pl.BlockSpec((tm, K), lambda i, j: (i, 0), pipeline_mode=pl.Buffered(a_buffers)),
                pl.BlockSpec((K, tn), lambda i, j: (0, j)),
                tile]
    aliases = {2: 0} if in_place else {}
    if next_norm_w is None:
        return pl.pallas_call(
            _mm_res_kernel,
            grid=(M // tm, N // tn),
            in_specs=in_specs,
            out_specs=tile,
            out_shape=jax.ShapeDtypeStruct((M, N), F32),
            input_output_aliases=aliases,
            compiler_params=_params("parallel", "arbitrary"),
            name="matmul_residual",
        )(a, b, x)
    return pl.pallas_call(
        _mm_res_norm_kernel,
        grid=(M // tm, N // tn),
        in_specs=in_specs + [pl.BlockSpec((1, tn), lambda i, j: (0, j))],
        out_specs=[tile, tile, pl.BlockSpec((tm, LANES), lambda i, j: (i, 0))],
        out_shape=[jax.ShapeDtypeStruct((M, N), F32), jax.ShapeDtypeStruct((M, N), BF16),
                   jax.ShapeDtypeStruct((M, LANES), F32)],
        input_output_aliases=aliases,
        compiler_params=_params("parallel", "arbitrary"),
        name="matmul_residual_norm",
    )(a, b, x, next_norm_w.reshape(1, N))


def _swiglu_kernel(a_ref, ss_ref, wg_ref, wu_ref, o_ref, r_ref):
    @pl.when(pl.program_id(1) == 0)
    def _():
        r_ref[...] = _row_scale(ss_ref[...], a_ref.shape[1])
    scale = jnp.tile(r_ref[...], (1, o_ref.shape[1] // LANES))
    a = a_ref[...]
    g = _dot(a, wg_ref[...].astype(BF16)) * scale
    u = _dot(a, wu_ref[...].astype(BF16)) * scale
    o_ref[...] = (g * _sigmoid(g) * u).astype(o_ref.dtype)


def swiglu_up(a, ss, wg, wu, layer, tm=2048, tn=256):
    M, K = a.shape
    N = wg.shape[2]
    tm, tn = min(tm, M), min(tn, N)
    assert M % tm == 0 and N % tn == 0
    return pl.pallas_call(
        _swiglu_kernel,
        grid=(M // tm, N // tn),
        in_specs=[pl.BlockSpec((tm, K), lambda i, j: (i, 0), pipeline_mode=pl.Buffered(1)),
                  pl.BlockSpec((tm, LANES), lambda i, j: (i, 0)),
                  pl.BlockSpec((None, K, tn), lambda i, j: (layer, 0, j)),
                  pl.BlockSpec((None, K, tn), lambda i, j: (layer, 0, j))],
        out_specs=pl.BlockSpec((tm, tn), lambda i, j: (i, j)),
        out_shape=jax.ShapeDtypeStruct((M, N), BF16),
        scratch_shapes=[pltpu.VMEM((tm, LANES), F32)],
        compiler_params=_params("parallel", "arbitrary"),
        name="swiglu_up",
    )(a, ss, wg, wu)


def _merge_kernel(yh_ref, yd_ref, yr_ref, gd_ref, wh_ref, wd_ref, wr_ref,
                  wgh_ref, wgd_ref, wgr_ref, bh_ref, bd_ref, br_ref, o_ref):
    gd = gd_ref[...]

    def branch(y_ref, w_ref, wg_ref, b_ref):
        gate = _sigmoid(_dot(gd, wg_ref[...]) + b_ref[...])
        return gate * _dot(y_ref[...], w_ref[...])

    acc = branch(yh_ref, wh_ref, wgh_ref, bh_ref)
    acc = acc + branch(yd_ref, wd_ref, wgd_ref, bd_ref)
    acc = acc + branch(yr_ref, wr_ref, wgr_ref, br_ref)
    o_ref[...] = acc.astype(o_ref.dtype)


def gated_merge(yh, yd, yr, gd, wh, wd, wr, wg, bg, tm=1024, tn=512):
    M = yh.shape[0]
    D = wh.shape[1]
    tm, tn = min(tm, M), min(tn, D)
    assert M % tm == 0 and D % tn == 0
    nb = D // tn
    row = lambda width: pl.BlockSpec((tm, width), lambda i, j: (i, 0))
    wcol = lambda k: pl.BlockSpec((k, tn), lambda i, j: (0, j))
    gcol = lambda br: pl.BlockSpec((GATE_RANK, tn), lambda i, j: (0, j + br * nb))
    bcol = lambda br: pl.BlockSpec((1, tn), lambda i, j: (0, j + br * nb))
    bg2 = bg.reshape(1, -1)
    return pl.pallas_call(
        _merge_kernel,
        grid=(M // tm, nb),
        in_specs=[row(yh.shape[1]), row(yd.shape[1]), row(yr.shape[1]), row(gd.shape[1]),
                  wcol(wh.shape[0]), wcol(wd.shape[0]), wcol(wr.shape[0]),
                  gcol(0), gcol(1), gcol(2), bcol(0), bcol(1), bcol(2)],
        out_specs=pl.BlockSpec((tm, tn), lambda i, j: (i, j)),
        out_shape=jax.ShapeDtypeStruct((M, D), BF16),
        compiler_params=_params("parallel", "arbitrary"),
        name="gated_merge",
    )(yh, yd, yr, gd, wh, wd, wr, wg, wg, wg, bg2, bg2, bg2)


def _rope_table_kernel(inv_ref, cos_ref, sin_ref):
    T = cos_ref.shape[0]
    pos = (lax.broadcasted_iota(jnp.int32, (T, LANES), 0) + pl.program_id(0) * T).astype(F32)
    ang = pos * inv_ref[...]
    lane = lax.broadcasted_iota(jnp.int32, (T, LANES), 1)
    cos_ref[...] = jnp.cos(ang)
    sin_ref[...] = jnp.where(lane < LANES // 2, -jnp.sin(ang), jnp.sin(ang))


def rope_tables(S, T=256):
    T = min(T, S)
    half = RET_KDIM // 2
    inv = 1.0 / (ROPE_BASE ** (jnp.arange(half, dtype=F32) / half))
    inv2 = jnp.concatenate([inv, inv]).reshape(1, LANES)
    return pl.pallas_call(
        _rope_table_kernel,
        grid=(S // T,),
        in_specs=[pl.BlockSpec((1, LANES), lambda i: (0, 0))],
        out_specs=[pl.BlockSpec((T, LANES), lambda i: (i, 0))] * 2,
        out_shape=[jax.ShapeDtypeStruct((S, LANES), F32)] * 2,
        compiler_params=_params("parallel"),
        name="rope_tables",
    )(inv2)


def _bias_tile_kernel(tab_ref, o_ref):
    h = pl.program_id(0)
    T = o_ref.shape[2]
    j = lax.broadcasted_iota(jnp.int32, (T, T), 0)
    i = lax.broadcasted_iota(jnp.int32, (T, T), 1)
    max_exact = N_BUCKETS // 2
    o_ref[0, 3] = jnp.full((T, T), NEG, F32)
    for d in range(3):
        n = jnp.maximum(i - j + d * T, 0)
        nf = jnp.maximum(n, 1).astype(F32)
        large = max_exact + (jnp.log(nf / max_exact) / math.log(MAX_DISTANCE / max_exact)
                             * (N_BUCKETS - max_exact)).astype(jnp.int32)
        large = jnp.minimum(large, N_BUCKETS - 1)
        bucket = jnp.where(n < max_exact, n, large)
        val = jnp.zeros((T, T), F32)
        for b in range(N_BUCKETS):
            val = jnp.where(bucket == b, tab_ref[h, b], val)
        val = (val - tab_ref[h, N_BUCKETS - 1]) * LOG2E
        if d == 0:
            val = jnp.where(j > i, NEG, val)
        o_ref[0, d] = val


def bias_tiles(rel_bias, T):
    H = rel_bias.shape[1]
    return pl.pallas_call(
        _bias_tile_kernel,
        grid=(H,),
        in_specs=[pl.BlockSpec(memory_space=pltpu.SMEM)],
        out_specs=pl.BlockSpec((1, 4, T, T), lambda h: (h, 0, 0, 0)),
        out_shape=jax.ShapeDtypeStruct((H, 4, T, T), F32),
        compiler_params=_params("parallel"),
        name="bias_tiles",
    )(rel_bias.T)


def _hgrn_kernel(lbl_ref, nw_ref, q_ref, f_ref, i_ref, g_ref, o_ref,
                 st_ref, p_ref, r_ref, ol_ref, *, layer):
    @pl.when(pl.program_id(2) == 0)
    def _():
        st_ref[...] = jnp.zeros_like(st_ref)

    for hh in range(HGRN_STEP_HEADS):
        lanes = slice(hh * HGRN_DIM, (hh + 1) * HGRN_DIM)
        _hgrn_head(lbl_ref[:, lanes], nw_ref[...], q_ref[:, lanes], f_ref[:, lanes], i_ref[:, lanes],
                   g_ref[:, lanes], o_ref.at[:, lanes], st_ref.at[hh], p_ref.at[hh], r_ref.at[hh],
                   ol_ref.at[hh], layer)


def _hgrn_head(lg, nw, q, f, iv_all, g_all, o_ref, st_ref, p_ref, r_ref, ol_ref, layer):
    T = q.shape[0]
    C, SUB = HGRN_CHUNK, HGRN_SUB

    e = jnp.exp(lg - jnp.max(lg, axis=0, keepdims=True))
    sm = e / jnp.sum(e, axis=0, keepdims=True)
    csum = sm[0:1]
    for r in range(1, layer + 1):
        csum = csum + sm[r:r + 1]
    lb = csum - sm[0:1]

    forget = lb + (1.0 - lb) * _sigmoid(f)
    kf = 1.0 - forget
    logf = jnp.log(forget)
    qf = q * _sigmoid(q) * (HGRN_DIM ** -0.5)

    r_i = lax.broadcasted_iota(jnp.int32, (T, T), 0)
    c_i = lax.broadcasted_iota(jnp.int32, (T, T), 1)
    shift = C.bit_length() - 1
    same_chunk = (r_i >> shift) == (c_i >> shift)
    tri = jnp.where(c_i <= r_i, jnp.where(same_chunk, 1.0, 0.0), 0.0).astype(BF16)
    hi = logf.astype(BF16)
    rem = logf - hi.astype(F32)
    mid = rem.astype(BF16)
    lo = (rem - mid.astype(F32)).astype(BF16)
    bcum = (_dot(tri, hi) + _dot(tri, mid) + _dot(tri, lo)) * LOG2E

    ones = jnp.ones((HGRN_DIM, HGRN_DIM), BF16)
    row_c = lax.broadcasted_iota(jnp.int32, (C, HGRN_DIM), 0)
    row_s = lax.broadcasted_iota(jnp.int32, (SUB, HGRN_DIM), 0)

    n_sub = T // SUB
    causal_add = [jnp.where(row_s[(s // F32_TILE_ROWS) * F32_TILE_ROWS:] >= s, 0.0, NEG)
                  for s in range(SUB)]
    for sb in range(n_sub):
        sub = slice(sb * SUB, (sb + 1) * SUB)
        bi, qi = bcum[sub], qf[sub]
        for s in range(SUB):
            r = sb * SUB + s
            top = (s // F32_TILE_ROWS) * F32_TILE_ROWS
            dec = jnp.exp2(bi[top:] - bcum[r:r + 1] + causal_add[s])
            p = qi[top:] * kf[r:r + 1] * dec
            if top:
                p = jnp.concatenate([jnp.zeros((top, HGRN_DIM), F32), p], axis=0)
            p_ref[r * SUB:(r + 1) * SUB, :] = p.astype(BF16)
    r_ref[...] = _dot(p_ref[...], ones)

    n_blk = C // SUB
    for c in range(T // C):
        c0 = c * C
        b, kc = bcum[c0:c0 + C], kf[c0:c0 + C]
        qts, kts = [], []
        for j in range(1, n_blk):
            lo_r = j * SUB
            ref = b[lo_r - 1:lo_r]
            qts.append((qf[c0 + lo_r:c0 + lo_r + SUB] * jnp.exp2(b[lo_r:lo_r + SUB] - ref)).astype(BF16))
            kts.append((kc * jnp.exp2(jnp.where(row_c < lo_r, ref - b, NEG))).astype(BF16))
        att_all = _dot_nt(jnp.concatenate(qts, axis=0), jnp.concatenate(kts, axis=0))
        att = jnp.concatenate([att_all[j * SUB:(j + 1) * SUB, j * C:(j + 1) * C]
                               for j in range(n_blk - 1)], axis=0)
        off = _dot(att.astype(BF16), iv_all[c0:c0 + C].astype(BF16))
        for j in range(n_blk):
            lo_r = c0 + j * SUB
            oi = r_ref[lo_r * SUB:(lo_r + 1) * SUB, :] * iv_all[lo_r:lo_r + 1]
            for s in range(1, SUB):
                r = lo_r + s
                top = (s // F32_TILE_ROWS) * F32_TILE_ROWS
                term = r_ref[r * SUB + top:(r + 1) * SUB, :] * iv_all[r:r + 1]
                oi = oi + term if top == 0 else jnp.concatenate([oi[:top], oi[top:] + term], axis=0)
            if j > 0:
                oi = oi + off[(j - 1) * SUB:j * SUB]
            ol_ref[lo_r:lo_r + SUB, :] = oi

    state_t = st_ref[...]
    for c in range(T // C):
        rows = slice(c * C, (c + 1) * C)
        b, qc, kc, iv = bcum[rows], qf[rows], kf[rows], iv_all[rows]
        o = ol_ref[rows, :] + _dot_nt((qc * jnp.exp2(b)).astype(BF16), state_t.astype(BF16))
        b_last = b[C - 1:C]
        khat = (kc * jnp.exp2(b_last - b)).astype(BF16)
        state_t = state_t * jnp.exp2(b_last) + _dot(iv.T.astype(BF16), khat)
        ms = jnp.mean(o * o, axis=-1, keepdims=True)
        gc = g_all[rows]
        y = o * lax.rsqrt(ms + RMS_EPS) * nw * (gc * _sigmoid(gc))
        o_ref[rows, :] = y.astype(o_ref.dtype)
    st_ref[...] = state_t


def hgrn_mixer(p_h, lb_logits, norm_w, layer, B, S, T=256):
    T = min(T, S)
    nt = S // T
    H, dk, NH = HGRN_HEADS, HGRN_DIM, HGRN_STEP_HEADS
    HG = H // NH
    col = lambda grp: pl.BlockSpec((T, NH * dk), lambda b, h, t: (b * nt + t, grp * HG + h))
    return pl.pallas_call(
        functools.partial(_hgrn_kernel, layer=layer),
        grid=(B, HG, nt),
        in_specs=[pl.BlockSpec((lb_logits.shape[0], NH * dk), lambda b, h, t: (0, h)),
                  pl.BlockSpec((1, dk), lambda b, h, t: (0, 0)),
                  col(0), col(1), col(2), col(3)],
        out_specs=pl.BlockSpec((T, NH * dk), lambda b, h, t: (b * nt + t, h)),
        out_shape=jax.ShapeDtypeStruct((B * S, H * dk), BF16),
        scratch_shapes=[pltpu.VMEM((NH, dk, dk), F32),
                        pltpu.VMEM((NH, T * HGRN_SUB, dk), BF16),
                        pltpu.VMEM((NH, T * HGRN_SUB, dk), F32),
                        pltpu.VMEM((NH, T, dk), F32)],
        compiler_params=_params("parallel", "parallel", "arbitrary"),
        name="hgrn_mixer",
    )(lb_logits, norm_w.reshape(1, dk), p_h, p_h, p_h, p_h)


def _diff_kernel(lam_ref, nw_ref, q_ref, k_ref, v_ref, bias_ref, o_ref,
                 vt_ref, s0_ref, s1_ref, p0_ref, p1_ref, gm_ref, alpha_ref, m_ref, acc_ref, *, lam_init):
    QT = DIFF_QTILES
    T = q_ref.shape[0] // QT
    S = k_ref.shape[0]
    dh = DIFF_HEAD_DIM
    G = DIFF_GROUP
    n_groups = S // (G * T)
    qi = pl.program_id(2) * QT
    ng = (qi + QT - 1) // G + 1

    @pl.when(qi == 0)
    def _():
        def body(c, carry):
            start = pl.multiple_of(c * T, T)
            vt_ref[0:2 * dh, pl.ds(start, T)] = v_ref[pl.ds(start, T), :].astype(F32).T.astype(BF16)
            return carry
        lax.fori_loop(0, S // T, body, 0)
        pad_row = lax.broadcasted_iota(jnp.int32, (BF16_TILE_ROWS, S), 0)
        vt_ref[2 * dh:, :] = jnp.where(pad_row == 0, 1.0, 0.0).astype(BF16)

    row = lax.broadcasted_iota(jnp.int32, (2 * dh, T), 0)
    cols = []
    for t in range(QT):
        qt = q_ref[t * T:(t + 1) * T, :].astype(F32).T * (dh ** -0.5 * LOG2E)
        cols += [jnp.where(row < dh, qt, 0.0), jnp.where(row >= dh, qt, 0.0)]
    q2 = jnp.concatenate(cols, axis=1).astype(BF16)

    slots = (s0_ref, s1_ref)

    p_slots = (p0_ref, p1_ref)

    def values_group(g):
        pv = _dot(vt_ref[:, g * G * T:(g + 1) * G * T], p_slots[g % 2][...])
        acc_ref[...] = alpha_ref[g % 2] * acc_ref[...] + pv

    def block(g_scores, near, g_values, g_softmax):
        if g_values is not None:
            values_group(g_values)
        if g_softmax is not None:
            sm_slot, p_slot = slots[g_softmax % 2], p_slots[g_softmax % 2]
            m_old = m_ref[...]
            m_new = jnp.maximum(m_old, gm_ref[g_softmax % 2])
            alpha_ref[g_softmax % 2] = jnp.exp2(m_old - m_new)
        gm = None
        for u in range(G):
            if g_scores is not None:
                kt = g_scores * G + u
                s = _dot(k_ref[kt * T:(kt + 1) * T, :], q2)
                if near:
                    tiles = []
                    for t in range(QT):
                        d = qi + t - kt
                        tiles += [bias_ref[0, jnp.where(d < 0, 3, jnp.minimum(d, 2))]] * 2
                    s = s + jnp.concatenate(tiles, axis=1)
                slots[g_scores % 2][u] = s
                cm = jnp.max(s, axis=0, keepdims=True)
                gm = cm if gm is None else jnp.maximum(gm, cm)
            if g_softmax is not None:
                p_slot[u * T:(u + 1) * T, :] = jnp.exp2(sm_slot[u] - m_new).astype(BF16)
        if g_scores is not None:
            gm_ref[g_scores % 2] = gm
        if g_softmax is not None:
            m_ref[...] = m_new

    m_ref[...] = jnp.full(m_ref.shape, NEG, F32)
    acc_ref[...] = jnp.zeros(acc_ref.shape, F32)
    n_far_groups = jnp.maximum(qi - 1, 0) // G

    def stage(cond, g):
        for near in (False, True):
            is_near = g >= n_far_groups
            @pl.when(jnp.logical_and(cond, is_near if near else jnp.logical_not(is_near)))
            def _():
                block(g, near, None, g - 1 if g >= 1 else None)
                if g >= 1:
                    values_group(g - 1)

    stage(True, 0)
    for g in range(n_groups):
        if g + 1 < n_groups:
            stage(g < ng - 1, g + 1)

        @pl.when(g == ng - 1)
        def _():
            block(None, False, None, g)
            values_group(g)

    lp = lam_ref[...]
    lam = (jnp.exp(jnp.sum(lp[0:1] * lp[1:2], axis=-1, keepdims=True))
           - jnp.exp(jnp.sum(lp[2:3] * lp[3:4], axis=-1, keepdims=True)) + lam_init)
    acc = acc_ref[...]
    w = acc[:2 * dh] / acc[2 * dh:2 * dh + 1]
    for t in range(QT):
        c0 = 2 * t * T
        out = (w[:, c0:c0 + T] - lam * w[:, c0 + T:c0 + 2 * T]).T
        ms = jnp.mean(out * out, axis=-1, keepdims=True)
        y = out * lax.rsqrt(ms + RMS_EPS) * nw_ref[...] * (1.0 - lam_init)
        o_ref[t * T:(t + 1) * T, :] = y.astype(o_ref.dtype)


def diff_attention(p_d, bias, lam_params, norm_w, lam_init, B, S):
    T = bias.shape[2]
    G = DIFF_GROUP
    QT = DIFF_QTILES
    assert T >= MAX_DISTANCE and S % (G * T) == 0 and G % QT == 0
    nq = S // (QT * T)
    lanes = 2 * QT * T
    H, hw = DIFF_HEADS, 2 * DIFF_HEAD_DIM
    return pl.pallas_call(
        functools.partial(_diff_kernel, lam_init=lam_init),
        grid=(B, H, nq),
        in_specs=[pl.BlockSpec(lam_params.shape, lambda b, h, i: (0, 0)),
                  pl.BlockSpec((1, hw), lambda b, h, i: (0, 0)),
                  pl.BlockSpec((QT * T, hw), lambda b, h, i: (b * nq + i, h)),
                  pl.BlockSpec((S, hw), lambda b, h, i: (b, H + h)),
                  pl.BlockSpec((S, hw), lambda b, h, i: (b, 2 * H + h)),
                  pl.BlockSpec((1, 4, T, T), lambda b, h, i: (h, 0, 0, 0))],
        out_specs=pl.BlockSpec((QT * T, hw), lambda b, h, i: (b * nq + i, h)),
        out_shape=jax.ShapeDtypeStruct((B * S, H * hw), BF16),
        scratch_shapes=[pltpu.VMEM((hw + BF16_TILE_ROWS, S), BF16),
                        pltpu.VMEM((G, T, lanes), F32), pltpu.VMEM((G, T, lanes), F32),
                        pltpu.VMEM((G * T, lanes), BF16), pltpu.VMEM((G * T, lanes), BF16),
                        pltpu.VMEM((2, 1, lanes), F32), pltpu.VMEM((2, 1, lanes), F32),
                        pltpu.VMEM((1, lanes), F32),
                        pltpu.VMEM((hw + BF16_TILE_ROWS, lanes), F32)],
        compiler_params=_params("parallel", "parallel", "arbitrary"),
        name="diff_attention",
    )(lam_params, norm_w.reshape(1, hw), p_d, p_d, p_d, bias)


def _ret_kernel(lg_ref, nw_ref, cos_ref, sin_ref, q_ref, k_ref, v_ref, g_ref, o_ref,
                st_ref, dec_ref, xi_ref, zeta_ref):
    C = q_ref.shape[0]
    dk, dv = RET_KDIM, RET_VDIM

    @pl.when(pl.program_id(2) == 0)
    def _():
        st_ref[...] = jnp.zeros_like(st_ref)
        rowf = lax.broadcasted_iota(jnp.int32, (C, dk), 0).astype(F32)
        r_i = lax.broadcasted_iota(jnp.int32, (C, C), 0)
        c_i = lax.broadcasted_iota(jnp.int32, (C, C), 1)
        for hh in range(RET_STEP_HEADS):
            lg = lg_ref[hh]
            xi_ref[hh] = jnp.exp((rowf + 1.0) * lg)
            zeta_ref[hh] = jnp.exp((C - 1.0 - rowf) * lg)
            dec_ref[hh] = jnp.exp(jnp.where(r_i >= c_i, (r_i - c_i).astype(F32) * lg[:, 0:1], NEG))

    cosf, sinf = cos_ref[...], sin_ref[...]

    def rot(x):
        return x * cosf + pltpu.roll(x, dk // 2, 1) * sinf

    for hh in range(RET_STEP_HEADS):
        qr = rot(q_ref[:, hh * dk:(hh + 1) * dk])
        kr = rot(k_ref[:, hh * dk:(hh + 1) * dk]) * (dk ** -0.5)
        v = v_ref[:, hh * dv:(hh + 1) * dv]
        state = st_ref[hh]
        scores = _dot_nt(qr.astype(BF16), kr.astype(BF16)) * dec_ref[hh]
        o = _dot(scores.astype(BF16), v) + _dot((qr * xi_ref[hh]).astype(BF16), state.astype(BF16))
        gamma_c = jnp.exp(C * lg_ref[hh][:, 0:1])
        st_ref[hh] = gamma_c * state + _dot((kr * zeta_ref[hh]).T.astype(BF16), v)

        ms = jnp.mean(o * o, axis=-1, keepdims=True)
        g = g_ref[:, hh * dv:(hh + 1) * dv].astype(F32)
        y = o * lax.rsqrt(ms + RMS_EPS) * nw_ref[...] * (g * _sigmoid(g))
        o_ref[:, hh * dv:(hh + 1) * dv] = y.astype(o_ref.dtype)


def retention_mixer(p_qk, p_vg, cos_t, sin_t, norm_w, B, S, C=256):
    C = min(C, S)
    nt = S // C
    H, dk, dv, NH = RET_HEADS, RET_KDIM, RET_VDIM, RET_STEP_HEADS
    HG = H // NH
    log_gamma = jnp.log(1.0 - 2.0 ** (-5.0 - jnp.arange(H, dtype=F32)))
    lg = jnp.broadcast_to(log_gamma[:, None, None], (H, 1, LANES))
    return pl.pallas_call(
        _ret_kernel,
        grid=(B, HG, nt),
        in_specs=[pl.BlockSpec((NH, 1, LANES), lambda b, h, t: (h, 0, 0)),
                  pl.BlockSpec((1, dv), lambda b, h, t: (0, 0)),
                  pl.BlockSpec((C, dk), lambda b, h, t: (t, 0)),
                  pl.BlockSpec((C, dk), lambda b, h, t: (t, 0)),
                  pl.BlockSpec((C, NH * dk), lambda b, h, t: (b * nt + t, h)),
                  pl.BlockSpec((C, NH * dk), lambda b, h, t: (b * nt + t, HG + h)),
                  pl.BlockSpec((C, NH * dv), lambda b, h, t: (b * nt + t, h)),
                  pl.BlockSpec((C, NH * dv), lambda b, h, t: (b * nt + t, HG + h))],
        out_specs=pl.BlockSpec((C, NH * dv), lambda b, h, t: (b * nt + t, h)),
        out_shape=jax.ShapeDtypeStruct((B * S, H * dv), BF16),
        scratch_shapes=[pltpu.VMEM((NH, dk, dv), F32), pltpu.VMEM((NH, C, C), F32),
                        pltpu.VMEM((NH, C, dk), F32), pltpu.VMEM((NH, C, dk), F32)],
        compiler_params=_params("parallel", "parallel", "arbitrary"),
        name="retention_mixer",
    )(lg, norm_w.reshape(1, dv), cos_t, sin_t, p_qk, p_qk, p_vg, p_vg)


def kernel(x, attn_norm_w, w_in, lb_logits, hgrn_norm_w, rel_bias, diff_lambda, diff_norm_w,
           ret_norm_w, w_gate_up, b_gate, w_br_hgrn, w_br_diff, w_br_ret, w_o, ffn_norm_w,
           w_ffn_gate, w_ffn_up, w_ffn_down, final_norm_w):
    B, S, D = x.shape
    depth = w_in.shape[0]
    M = B * S
    xs = x.reshape(M, D)

    cos_t, sin_t = rope_tables(S)
    bias = bias_tiles(rel_bias, min(256, S))

    off_d = 4 * HGRN_WIDTH
    off_rqk = off_d + 3 * DIFF_WIDTH
    off_rvg = off_rqk + 2 * RET_QK_WIDTH
    off_gd = off_rvg + 2 * RET_V_WIDTH

    h, ss = prenorm(xs, attn_norm_w[0])
    for l in range(depth):
        w_in_l = layer_weight_bf16(w_in, l)
        p_h = matmul_cols(h, ss, w_in_l, 0, 4 * HGRN_WIDTH, F32)
        p_d = matmul_cols(h, ss, w_in_l, off_d, 3 * DIFF_WIDTH, BF16)
        p_rqk = matmul_cols(h, ss, w_in_l, off_rqk, 2 * RET_QK_WIDTH, F32)
        p_rvg = matmul_cols(h, ss, w_in_l, off_rvg, 2 * RET_V_WIDTH, BF16)
        gd = matmul_cols(h, ss, w_in_l, off_gd, GATE_RANK, BF16, tn=GATE_RANK)

        y_h = hgrn_mixer(p_h, lb_logits, hgrn_norm_w[l], l, B, S)
        lam_init = 0.8 - 0.6 * math.exp(-0.3 * l)
        y_d = diff_attention(p_d, bias, diff_lambda[l], diff_norm_w[l], lam_init, B, S)
        y_r = retention_mixer(p_rqk, p_rvg, cos_t, sin_t, ret_norm_w[l], B, S)

        merged = gated_merge(y_h, y_d, y_r, gd,
                             layer_weight_bf16(w_br_hgrn, l), layer_weight_bf16(w_br_diff, l),
                             layer_weight_bf16(w_br_ret, l), layer_weight_bf16(w_gate_up, l), b_gate[l])
        xs, h2, ss2 = matmul_residual(merged, layer_weight_bf16(w_o, l), xs, ffn_norm_w[l],
                                      in_place=l > 0)
        act = swiglu_up(h2, ss2, w_ffn_gate, w_ffn_up, l)
        w_down = layer_weight_bf16(w_ffn_down, l)
        if l + 1 < depth:
            xs, h, ss = matmul_residual(act, w_down, xs, attn_norm_w[l + 1], tn=256, a_buffers=1)
        else:
            xs = matmul_residual(act, w_down, xs, tn=256, a_buffers=1)

    out = rms_norm_rows(xs, final_norm_w, x.dtype)
    return out.reshape(B, S, D)
```

```python
import functools
import math

import jax
import jax.numpy as jnp
from jax import lax
from jax.experimental import pallas as pl
from jax.experimental.pallas import tpu as pltpu

F32 = jnp.float32
BF16 = jnp.bfloat16

HGRN_HEADS = 8
HGRN_DIM = 128
DIFF_HEADS = 8
DIFF_HEAD_DIM = 64
RET_HEADS = 8
RET_KDIM = 128
RET_VDIM = 256
N_BUCKETS = 32
MAX_DISTANCE = 128
ROPE_BASE = 10000.0
GATE_RANK = 256
RMS_EPS = 1e-6
HGRN_WIDTH = HGRN_HEADS * HGRN_DIM
DIFF_WIDTH = DIFF_HEADS * 2 * DIFF_HEAD_DIM
RET_QK_WIDTH = RET_HEADS * RET_KDIM
RET_V_WIDTH = RET_HEADS * RET_VDIM

V7X_VMEM_BYTES = 64 * 1024 * 1024
VMEM_LIMIT = V7X_VMEM_BYTES - 8 * 1024 * 1024
LANES = 128
BF16_TILE_ROWS = 16
F32_TILE_ROWS = 8
LOG2E = math.log2(math.e)

NEG = -1e30
HGRN_CHUNK = 128
HGRN_SUB = 16
DIFF_GROUP = 4
DIFF_QTILES = 2
RET_STEP_HEADS = 4
MM_SUB_ROWS = 512
HGRN_STEP_HEADS = 4


def _params(*sem):
    return pltpu.CompilerParams(dimension_semantics=sem, vmem_limit_bytes=VMEM_LIMIT)


def _sigmoid(x):
    return 1.0 / (1.0 + jnp.exp(-x))


def _dot(a, b):
    return jnp.dot(a, b, preferred_element_type=F32)


def _dot_nt(a, b):
    return lax.dot_general(a, b, (((1,), (1,)), ((), ())), preferred_element_type=F32)


def _rms_kernel(x_ref, w_ref, o_ref):
    x = x_ref[...]
    ms = jnp.mean(x * x, axis=-1, keepdims=True)
    o_ref[...] = (x * lax.rsqrt(ms + RMS_EPS) * w_ref[...]).astype(o_ref.dtype)


def rms_norm_rows(x, w, out_dtype, tm=256):
    M, D = x.shape
    tm = min(tm, M)
    return pl.pallas_call(
        _rms_kernel,
        grid=(M // tm,),
        in_specs=[pl.BlockSpec((tm, D), lambda i: (i, 0)),
                  pl.BlockSpec((1, D), lambda i: (0, 0))],
        out_specs=pl.BlockSpec((tm, D), lambda i: (i, 0)),
        out_shape=jax.ShapeDtypeStruct((M, D), out_dtype),
        compiler_params=_params("parallel"),
        name="rmsnorm",
    )(x, w.reshape(1, D))


def _cast_kernel(w_ref, o_ref):
    o_ref[...] = w_ref[...].astype(o_ref.dtype)


def layer_weight_bf16(w, layer, block_bytes=8 * 1024 * 1024):
    _, R, C = w.shape
    rb = R
    while rb * C * 4 > block_bytes and rb % (2 * BF16_TILE_ROWS) == 0:
        rb //= 2
    assert R % rb == 0
    return pl.pallas_call(
        _cast_kernel,
        grid=(R // rb,),
        in_specs=[pl.BlockSpec((None, rb, C), lambda i: (layer, i, 0))],
        out_specs=pl.BlockSpec((rb, C), lambda i: (i, 0)),
        out_shape=jax.ShapeDtypeStruct((R, C), BF16),
        compiler_params=_params("parallel"),
        name="weight_bf16",
    )(w)


def _lane_partial_sumsq(x):
    sq = x * x
    part = sq[:, 0:LANES]
    for k in range(1, x.shape[1] // LANES):
        part = part + sq[:, k * LANES:(k + 1) * LANES]
    return part


def _row_scale(ss, d_model):
    ms = jnp.sum(ss, axis=-1, keepdims=True) / d_model
    return jnp.broadcast_to(lax.rsqrt(ms + RMS_EPS), ss.shape)


def _prenorm_kernel(x_ref, w_ref, xw_ref, ss_ref):
    x = x_ref[...]
    xw_ref[...] = (x * w_ref[...]).astype(xw_ref.dtype)
    ss_ref[...] = _lane_partial_sumsq(x)


def prenorm(x, w, tm=256):
    M, D = x.shape
    tm = min(tm, M)
    return pl.pallas_call(
        _prenorm_kernel,
        grid=(M // tm,),
        in_specs=[pl.BlockSpec((tm, D), lambda i: (i, 0)),
                  pl.BlockSpec((1, D), lambda i: (0, 0))],
        out_specs=[pl.BlockSpec((tm, D), lambda i: (i, 0)),
                   pl.BlockSpec((tm, LANES), lambda i: (i, 0))],
        out_shape=[jax.ShapeDtypeStruct((M, D), BF16), jax.ShapeDtypeStruct((M, LANES), F32)],
        compiler_params=_params("parallel"),
        name="prenorm",
    )(x, w.reshape(1, D))


def _row_blocks(tm):
    sub = min(tm, MM_SUB_ROWS)
    return [slice(r, r + sub) for r in range(0, tm, sub)]


def _mm_kernel(a_ref, ss_ref, b_ref, o_ref, r_ref):
    @pl.when(pl.program_id(1) == 0)
    def _():
        r_ref[...] = _row_scale(ss_ref[...], a_ref.shape[1])
    scale = jnp.tile(r_ref[...], (1, o_ref.shape[1] // LANES))
    o_ref[...] = (_dot(a_ref[...], b_ref[...]) * scale).astype(o_ref.dtype)


def matmul_cols(a, ss, b, col_off, n, out_dtype, tm=1024, tn=1024):
    M, K = a.shape
    tm, tn = min(tm, M), min(tn, n)
    assert col_off % tn == 0 and n % tn == 0 and M % tm == 0
    off = col_off // tn
    return pl.pallas_call(
        _mm_kernel,
        grid=(M // tm, n // tn),
        in_specs=[pl.BlockSpec((tm, K), lambda i, j: (i, 0)),
                  pl.BlockSpec((tm, LANES), lambda i, j: (i, 0)),
                  pl.BlockSpec((K, tn), lambda i, j: (0, j + off))],
        out_specs=pl.BlockSpec((tm, tn), lambda i, j: (i, j)),
        out_shape=jax.ShapeDtypeStruct((M, n), out_dtype),
        scratch_shapes=[pltpu.VMEM((tm, LANES), F32)],
        compiler_params=_params("parallel", "arbitrary"),
        name="matmul_cols",
    )(a, ss, b)


def _mm_res_kernel(a_ref, b_ref, x_ref, o_ref):
    o_ref[...] = x_ref[...] + _dot(a_ref[...], b_ref[...])


def _mm_res_norm_kernel(a_ref, b_ref, x_ref, w_ref, o_ref, xw_ref, ss_ref):
    y = x_ref[...] + _dot(a_ref[...], b_ref[...])
    o_ref[...] = y
    xw_ref[...] = (y * w_ref[...]).astype(xw_ref.dtype)
    part = _lane_partial_sumsq(y)

    @pl.when(pl.program_id(1) == 0)
    def _():
        ss_ref[...] = part

    @pl.when(pl.program_id(1) > 0)
    def _():
        ss_ref[...] = ss_ref[...] + part


def matmul_residual(a, b, x, next_norm_w=None, in_place=True, tm=1024, tn=512, a_buffers=2):
    M, K = a.shape
    N = b.shape[1]
    tm, tn = min(tm, M), min(tn, N)
    assert M % tm == 0 and N % tn == 0
    tile = pl.BlockSpec((tm, tn), lambda i, j: (i, j))
    in_specs = [pl.BlockSpec((tm, K), lambda i, j: (i, 0), pipeline_mode=pl.Buffered(a_buffers)),
                pl.BlockSpec((K, tn), lambda i, j: (0, j)),
                tile]
    aliases = {2: 0} if in_place else {}
    if next_norm_w is None:
        return pl.pallas_call(
            _mm_res_kernel,
            grid=(M // tm, N // tn),
            in_specs=in_specs,
            out_specs=tile,
            out_shape=jax.ShapeDtypeStruct((M, N), F32),
            input_output_aliases=aliases,
            compiler_params=_params("parallel", "arbitrary"),
            name="matmul_residual",
        )(a, b, x)
    return pl.pallas_call(
        _mm_res_norm_kernel,
        grid=(M // tm, N // tn),
        in_specs=in_specs + [pl.BlockSpec((1, tn), lambda i, j: (0, j))],
        out_specs=[tile, tile, pl.BlockSpec((tm, LANES), lambda i, j: (i, 0))],
        out_shape=[jax.ShapeDtypeStruct((M, N), F32), jax.ShapeDtypeStruct((M, N), BF16),
                   jax.ShapeDtypeStruct((M, LANES), F32)],
        input_output_aliases=aliases,
        compiler_params=_params("parallel", "arbitrary"),
        name="matmul_residual_norm",
    )(a, b, x, next_norm_w.reshape(1, N))


def _swiglu_kernel(a_ref, ss_ref, wg_ref, wu_ref, o_ref, r_ref):
    @pl.when(pl.program_id(1) == 0)
    def _():
        r_ref[...] = _row_scale(ss_ref[...], a_ref.shape[1])
    wg = wg_ref[...].astype(BF16)
    wu = wu_ref[...].astype(BF16)
    for rows in _row_blocks(a_ref.shape[0]):
        scale = jnp.tile(r_ref[rows, :], (1, o_ref.shape[1] // LANES))
        a = a_ref[rows, :]
        g = _dot(a, wg) * scale
        u = _dot(a, wu) * scale
        o_ref[rows, :] = (g * _sigmoid(g) * u).astype(o_ref.dtype)


def swiglu_up(a, ss, wg, wu, layer, tm=2048, tn=256):
    M, K = a.shape
    N = wg.shape[2]
    tm, tn = min(tm, M), min(tn, N)
    assert M % tm == 0 and N % tn == 0
    return pl.pallas_call(
        _swiglu_kernel,
        grid=(M // tm, N // tn),
        in_specs=[pl.BlockSpec((tm, K), lambda i, j: (i, 0), pipeline_mode=pl.Buffered(1)),
                  pl.BlockSpec((tm, LANES), lambda i, j: (i, 0)),
                  pl.BlockSpec((None, K, tn), lambda i, j: (layer, 0, j)),
                  pl.BlockSpec((None, K, tn), lambda i, j: (layer, 0, j))],
        out_specs=pl.BlockSpec((tm, tn), lambda i, j: (i, j)),
        out_shape=jax.ShapeDtypeStruct((M, N), BF16),
        scratch_shapes=[pltpu.VMEM((tm, LANES), F32)],
        compiler_params=_params("parallel", "arbitrary"),
        name="swiglu_up",
    )(a, ss, wg, wu)


def _merge_kernel(yh_ref, yd_ref, yr_ref, gd_ref, wh_ref, wd_ref, wr_ref,
                  wgh_ref, wgd_ref, wgr_ref, bh_ref, bd_ref, br_ref, o_ref):
    gd = gd_ref[...]

    def branch(y_ref, w_ref, wg_ref, b_ref):
        gate = _sigmoid(_dot(gd, wg_ref[...]) + b_ref[...])
        return gate * _dot(y_ref[...], w_ref[...])

    acc = branch(yh_ref, wh_ref, wgh_ref, bh_ref)
    acc = acc + branch(yd_ref, wd_ref, wgd_ref, bd_ref)
    acc = acc + branch(yr_ref, wr_ref, wgr_ref, br_ref)
    o_ref[...] = acc.astype(o_ref.dtype)


def gated_merge(yh, yd, yr, gd, wh, wd, wr, wg, bg, tm=1024, tn=512):
    M = yh.shape[0]
    D = wh.shape[1]
    tm, tn = min(tm, M), min(tn, D)
    assert M % tm == 0 and D % tn == 0
    nb = D // tn
    row = lambda width: pl.BlockSpec((tm, width), lambda i, j: (i, 0))
    wcol = lambda k: pl.BlockSpec((k, tn), lambda i, j: (0, j))
    gcol = lambda br: pl.BlockSpec((GATE_RANK, tn), lambda i, j: (0, j + br * nb))
    bcol = lambda br: pl.BlockSpec((1, tn), lambda i, j: (0, j + br * nb))
    bg2 = bg.reshape(1, -1)
    return pl.pallas_call(
        _merge_kernel,
        grid=(M // tm, nb),
        in_specs=[row(yh.shape[1]), row(yd.shape[1]), row(yr.shape[1]), row(gd.shape[1]),
                  wcol(wh.shape[0]), wcol(wd.shape[0]), wcol(wr.shape[0]),
                  gcol(0), gcol(1), gcol(2), bcol(0), bcol(1), bcol(2)],
        out_specs=pl.BlockSpec((tm, tn), lambda i, j: (i, j)),
        out_shape=jax.ShapeDtypeStruct((M, D), BF16),
        compiler_params=_params("parallel", "arbitrary"),
        name="gated_merge",
    )(yh, yd, yr, gd, wh, wd, wr, wg, wg, wg, bg2, bg2, bg2)


def _rope_table_kernel(inv_ref, cos_ref, sin_ref):
    T = cos_ref.shape[0]
    pos = (lax.broadcasted_iota(jnp.int32, (T, LANES), 0) + pl.program_id(0) * T).astype(F32)
    ang = pos * inv_ref[...]
    lane = lax.broadcasted_iota(jnp.int32, (T, LANES), 1)
    cos_ref[...] = jnp.cos(ang)
    sin_ref[...] = jnp.where(lane < LANES // 2, -jnp.sin(ang), jnp.sin(ang))


def rope_tables(S, T=256):
    T = min(T, S)
    half = RET_KDIM // 2
    inv = 1.0 / (ROPE_BASE ** (jnp.arange(half, dtype=F32) / half))
    inv2 = jnp.concatenate([inv, inv]).reshape(1, LANES)
    return pl.pallas_call(
        _rope_table_kernel,
        grid=(S // T,),
        in_specs=[pl.BlockSpec((1, LANES), lambda i: (0, 0))],
        out_specs=[pl.BlockSpec((T, LANES), lambda i: (i, 0))] * 2,
        out_shape=[jax.ShapeDtypeStruct((S, LANES), F32)] * 2,
        compiler_params=_params("parallel"),
        name="rope_tables",
    )(inv2)


def _bias_tile_kernel(tab_ref, o_ref):
    h = pl.program_id(0)
    T = o_ref.shape[2]
    j = lax.broadcasted_iota(jnp.int32, (T, T), 0)
    i = lax.broadcasted_iota(jnp.int32, (T, T), 1)
    max_exact = N_BUCKETS // 2
    o_ref[0, 3] = jnp.full((T, T), NEG, F32)
    for d in range(3):
        n = jnp.maximum(i - j + d * T, 0)
        nf = jnp.maximum(n, 1).astype(F32)
        large = max_exact + (jnp.log(nf / max_exact) / math.log(MAX_DISTANCE / max_exact)
                             * (N_BUCKETS - max_exact)).astype(jnp.int32)
        large = jnp.minimum(large, N_BUCKETS - 1)
        bucket = jnp.where(n < max_exact, n, large)
        val = jnp.zeros((T, T), F32)
        for b in range(N_BUCKETS):
            val = jnp.where(bucket == b, tab_ref[h, b], val)
        val = (val - tab_ref[h, N_BUCKETS - 1]) * LOG2E
        if d == 0:
            val = jnp.where(j > i, NEG, val)
        o_ref[0, d] = val


def bias_tiles(rel_bias, T):
    H = rel_bias.shape[1]
    return pl.pallas_call(
        _bias_tile_kernel,
        grid=(H,),
        in_specs=[pl.BlockSpec(memory_space=pltpu.SMEM)],
        out_specs=pl.BlockSpec((1, 4, T, T), lambda h: (h, 0, 0, 0)),
        out_shape=jax.ShapeDtypeStruct((H, 4, T, T), F32),
        compiler_params=_params("parallel"),
        name="bias_tiles",
    )(rel_bias.T)


def _hgrn_kernel(lbl_ref, nw_ref, q_ref, f_ref, i_ref, g_ref, o_ref,
                 st_ref, p_ref, r_ref, ol_ref, *, layer):
    @pl.when(pl.program_id(2) == 0)
    def _():
        st_ref[...] = jnp.zeros_like(st_ref)

    for hh in range(HGRN_STEP_HEADS):
        lanes = slice(hh * HGRN_DIM, (hh + 1) * HGRN_DIM)
        _hgrn_head(lbl_ref[:, lanes], nw_ref[...], q_ref[:, lanes], f_ref[:, lanes], i_ref[:, lanes],
                   g_ref[:, lanes], o_ref.at[:, lanes], st_ref.at[hh], p_ref.at[hh], r_ref.at[hh],
                   ol_ref.at[hh], layer)


def _hgrn_head(lg, nw, q, f, iv_all, g_all, o_ref, st_ref, p_ref, r_ref, ol_ref, layer):
    bcum, qf, kf = _hgrn_head_gates(lg, q, f, layer)
    _hgrn_head_mix(bcum, qf, kf, nw, iv_all, g_all, o_ref, st_ref, p_ref, r_ref, ol_ref)


def _hgrn_head_gates(lg, q, f, layer):
    T = q.shape[0]
    C = HGRN_CHUNK

    e = jnp.exp(lg - jnp.max(lg, axis=0, keepdims=True))
    sm = e / jnp.sum(e, axis=0, keepdims=True)
    csum = sm[0:1]
    for r in range(1, layer + 1):
        csum = csum + sm[r:r + 1]
    lb = csum - sm[0:1]

    forget = lb + (1.0 - lb) * _sigmoid(f)
    kf = 1.0 - forget
    logf = jnp.log(forget)
    qf = q * _sigmoid(q) * (HGRN_DIM ** -0.5)

    r_i = lax.broadcasted_iota(jnp.int32, (T, T), 0)
    c_i = lax.broadcasted_iota(jnp.int32, (T, T), 1)
    shift = C.bit_length() - 1
    same_chunk = (r_i >> shift) == (c_i >> shift)
    tri = jnp.where(c_i <= r_i, jnp.where(same_chunk, 1.0, 0.0), 0.0).astype(BF16)
    hi = logf.astype(BF16)
    rem = logf - hi.astype(F32)
    mid = rem.astype(BF16)
    lo = (rem - mid.astype(F32)).astype(BF16)
    bcum = (_dot(tri, hi) + _dot(tri, mid) + _dot(tri, lo)) * LOG2E
    return bcum, qf, kf


def _hgrn_head_mix(bcum, qf, kf, nw, iv_all, g_all, o_ref, st_ref, p_ref, r_ref, ol_ref):
    T = qf.shape[0]
    C, SUB = HGRN_CHUNK, HGRN_SUB
    row_c = lax.broadcasted_iota(jnp.int32, (C, HGRN_DIM), 0)
    row_s = lax.broadcasted_iota(jnp.int32, (SUB, HGRN_DIM), 0)
    d_i = lax.broadcasted_iota(jnp.int32, (2 * HGRN_DIM, 2 * HGRN_DIM), 0)
    d_j = lax.broadcasted_iota(jnp.int32, (2 * HGRN_DIM, 2 * HGRN_DIM), 1)
    ones2 = jnp.where((d_i < HGRN_DIM) == (d_j < HGRN_DIM), 1.0, 0.0).astype(BF16)

    def slot(r):
        half, rb = divmod(r, T // 2)
        return rb * SUB, slice(half * HGRN_DIM, (half + 1) * HGRN_DIM)

    n_sub = T // SUB
    causal_add = [jnp.where(row_s[(s // F32_TILE_ROWS) * F32_TILE_ROWS:] >= s, 0.0, NEG)
                  for s in range(SUB)]
    for sb in range(n_sub):
        sub = slice(sb * SUB, (sb + 1) * SUB)
        bi, qi = bcum[sub], qf[sub]
        for s in range(SUB):
            r = sb * SUB + s
            top = (s // F32_TILE_ROWS) * F32_TILE_ROWS
            dec = jnp.exp2(bi[top:] - bcum[r:r + 1] + causal_add[s])
            p = qi[top:] * kf[r:r + 1] * dec
            if top:
                p = jnp.concatenate([jnp.zeros((top, HGRN_DIM), F32), p], axis=0)
            r0, lanes = slot(r)
            p_ref[r0:r0 + SUB, lanes] = p.astype(BF16)
    r_ref[...] = _dot(p_ref[...], ones2)

    n_blk = C // SUB
    for c in range(T // C):
        c0 = c * C
        b, kc = bcum[c0:c0 + C], kf[c0:c0 + C]
        qts, kts = [], []
        for j in range(1, n_blk):
            lo_r = j * SUB
            ref = b[lo_r - 1:lo_r]
            qts.append((qf[c0 + lo_r:c0 + lo_r + SUB] * jnp.exp2(b[lo_r:lo_r + SUB] - ref)).astype(BF16))
            kts.append((kc * jnp.exp2(jnp.where(row_c < lo_r, ref - b, NEG))).astype(BF16))
        att_all = _dot_nt(jnp.concatenate(qts, axis=0), jnp.concatenate(kts, axis=0))
        att = jnp.concatenate([att_all[j * SUB:(j + 1) * SUB, j * C:(j + 1) * C]
                               for j in range(n_blk - 1)], axis=0)
        off = _dot(att.astype(BF16), iv_all[c0:c0 + C].astype(BF16))
        for j in range(n_blk):
            lo_r = c0 + j * SUB
            r0, lanes = slot(lo_r)
            oi = r_ref[r0:r0 + SUB, lanes] * iv_all[lo_r:lo_r + 1]
            for s in range(1, SUB):
                r = lo_r + s
                top = (s // F32_TILE_ROWS) * F32_TILE_ROWS
                r0, lanes = slot(r)
                term = r_ref[r0 + top:r0 + SUB, lanes] * iv_all[r:r + 1]
                oi = oi + term if top == 0 else jnp.concatenate([oi[:top], oi[top:] + term], axis=0)
            if j > 0:
                oi = oi + off[(j - 1) * SUB:j * SUB]
            ol_ref[lo_r:lo_r + SUB, :] = oi

    state_t = st_ref[...]
    for c in range(T // C):
        rows = slice(c * C, (c + 1) * C)
        b, qc, kc, iv = bcum[rows], qf[rows], kf[rows], iv_all[rows]
        o = ol_ref[rows, :] + _dot_nt((qc * jnp.exp2(b)).astype(BF16), state_t.astype(BF16))
        b_last = b[C - 1:C]
        khat = (kc * jnp.exp2(b_last - b)).astype(BF16)
        state_t = state_t * jnp.exp2(b_last) + _dot(iv.T.astype(BF16), khat)
        ms = jnp.mean(o * o, axis=-1, keepdims=True)
        gc = g_all[rows]
        y = o * lax.rsqrt(ms + RMS_EPS) * nw * (gc * _sigmoid(gc))
        o_ref[rows, :] = y.astype(o_ref.dtype)
    st_ref[...] = state_t


def hgrn_mixer(p_h, lb_logits, norm_w, layer, B, S, T=256):
    T = min(T, S)
    nt = S // T
    H, dk, NH = HGRN_HEADS, HGRN_DIM, HGRN_STEP_HEADS
    HG = H // NH
    col = lambda grp: pl.BlockSpec((T, NH * dk), lambda b, h, t: (b * nt + t, grp * HG + h))
    return pl.pallas_call(
        functools.partial(_hgrn_kernel, layer=layer),
        grid=(B, HG, nt),
        in_specs=[pl.BlockSpec((lb_logits.shape[0], NH * dk), lambda b, h, t: (0, h)),
                  pl.BlockSpec((1, dk), lambda b, h, t: (0, 0)),
                  col(0), col(1), col(2), col(3)],
        out_specs=pl.BlockSpec((T, NH * dk), lambda b, h, t: (b * nt + t, h)),
        out_shape=jax.ShapeDtypeStruct((B * S, H * dk), BF16),
        scratch_shapes=[pltpu.VMEM((NH, dk, dk), F32),
                        pltpu.VMEM((NH, T // 2 * HGRN_SUB, 2 * dk), BF16),
                        pltpu.VMEM((NH, T // 2 * HGRN_SUB, 2 * dk), F32),
                        pltpu.VMEM((NH, T, dk), F32)],
        compiler_params=_params("parallel", "parallel", "arbitrary"),
        name="hgrn_mixer",
    )(lb_logits, norm_w.reshape(1, dk), p_h, p_h, p_h, p_h)


def _mm_hgrn_kernel(a_ref, ss_ref, b_ref, lbl_ref, nw_ref, q_ref, f_ref, i_ref, g_ref,
                    o_ref, y_ref, r_ref, st_ref, p_ref, r2_ref, ol_ref,
                    *, layer, n_col_blocks, n_time_blocks):
    step = pl.program_id(0)

    @pl.when(step % n_col_blocks == 0)
    def _():
        r_ref[...] = _row_scale(ss_ref[...], a_ref.shape[1])

    @pl.when(step % n_time_blocks == 0)
    def _():
        st_ref[...] = jnp.zeros_like(st_ref)

    b = b_ref[...]
    sub = a_ref.shape[0] // HGRN_STEP_HEADS
    for hh in range(HGRN_STEP_HEADS):
        lanes = slice(hh * HGRN_DIM, (hh + 1) * HGRN_DIM)
        bcum, qf, kf = _hgrn_head_gates(lbl_ref[:, lanes], q_ref[:, lanes], f_ref[:, lanes], layer)
        rows = slice(hh * sub, (hh + 1) * sub)
        scale = jnp.tile(r_ref[rows, :], (1, o_ref.shape[1] // LANES))
        o_ref[rows, :] = (_dot(a_ref[rows, :], b) * scale).astype(o_ref.dtype)
        _hgrn_head_mix(bcum, qf, kf, nw_ref[...], i_ref[:, lanes], g_ref[:, lanes],
                       y_ref.at[:, lanes], st_ref.at[hh], p_ref.at[hh], r2_ref.at[hh], ol_ref.at[hh])


def matmul_cols_with_hgrn(a, ss, b, col_off, n, out_dtype, p_h, lb_logits, norm_w, layer, B, S,
                          tm=1024, tn=512, T=256):
    M, K = a.shape
    tm, tn, T = min(tm, M), min(tn, n), min(T, S)
    nt = S // T
    H, dk, NH = HGRN_HEADS, HGRN_DIM, HGRN_STEP_HEADS
    HG = H // NH
    ncb = n // tn
    if (M // tm) * ncb != B * HG * nt:
        return (matmul_cols(a, ss, b, col_off, n, out_dtype),
                hgrn_mixer(p_h, lb_logits, norm_w, layer, B, S))
    assert col_off % tn == 0 and n % tn == 0 and M % tm == 0
    off = col_off // tn
    row_blk = lambda s: (s // (HG * nt)) * nt + s % nt
    head_grp = lambda s: (s // nt) % HG
    col = lambda grp: pl.BlockSpec((T, NH * dk), lambda s: (row_blk(s), grp * HG + head_grp(s)))
    return pl.pallas_call(
        functools.partial(_mm_hgrn_kernel, layer=layer, n_col_blocks=ncb, n_time_blocks=nt),
        grid=((M // tm) * ncb,),
        in_specs=[pl.BlockSpec((tm, K), lambda s: (s // ncb, 0)),
                  pl.BlockSpec((tm, LANES), lambda s: (s // ncb, 0)),
                  pl.BlockSpec((K, tn), lambda s: (0, s % ncb + off)),
                  pl.BlockSpec((lb_logits.shape[0], NH * dk), lambda s: (0, head_grp(s))),
                  pl.BlockSpec((1, dk), lambda s: (0, 0)),
                  col(0), col(1), col(2), col(3)],
        out_specs=[pl.BlockSpec((tm, tn), lambda s: (s // ncb, s % ncb)),
                   pl.BlockSpec((T, NH * dk), lambda s: (row_blk(s), head_grp(s)))],
        out_shape=[jax.ShapeDtypeStruct((M, n), out_dtype),
                   jax.ShapeDtypeStruct((B * S, H * dk), BF16)],
        scratch_shapes=[pltpu.VMEM((tm, LANES), F32),
                        pltpu.VMEM((NH, dk, dk), F32),
                        pltpu.VMEM((NH, T // 2 * HGRN_SUB, 2 * dk), BF16),
                        pltpu.VMEM((NH, T // 2 * HGRN_SUB, 2 * dk), F32),
                        pltpu.VMEM((NH, T, dk), F32)],
        compiler_params=_params("arbitrary"),
        name="matmul_cols_hgrn",
    )(a, ss, b, lb_logits, norm_w.reshape(1, dk), p_h, p_h, p_h, p_h)


def _diff_kernel(lam_ref, nw_ref, q_ref, k_ref, v_ref, bias_ref, o_ref,
                 vt_ref, s0_ref, s1_ref, p0_ref, p1_ref, gm_ref, alpha_ref, m_ref, acc_ref, *, lam_init):
    QT = DIFF_QTILES
    T = q_ref.shape[0] // QT
    S = k_ref.shape[0]
    dh = DIFF_HEAD_DIM
    G = DIFF_GROUP
    n_groups = S // (G * T)
    qi = pl.program_id(2) * QT
    ng = (qi + QT - 1) // G + 1

    @pl.when(qi == 0)
    def _():
        def body(c, carry):
            start = pl.multiple_of(c * T, T)
            vt_ref[0:2 * dh, pl.ds(start, T)] = v_ref[pl.ds(start, T), :].astype(F32).T.astype(BF16)
            return carry
        lax.fori_loop(0, S // T, body, 0)
        pad_row = lax.broadcasted_iota(jnp.int32, (BF16_TILE_ROWS, S), 0)
        vt_ref[2 * dh:, :] = jnp.where(pad_row == 0, 1.0, 0.0).astype(BF16)

    row = lax.broadcasted_iota(jnp.int32, (2 * dh, T), 0)
    cols = []
    for t in range(QT):
        qt = q_ref[t * T:(t + 1) * T, :].astype(F32).T * (dh ** -0.5 * LOG2E)
        cols += [jnp.where(row < dh, qt, 0.0), jnp.where(row >= dh, qt, 0.0)]
    q2 = jnp.concatenate(cols, axis=1).astype(BF16)

    slots = (s0_ref, s1_ref)

    p_slots = (p0_ref, p1_ref)

    def values_group(g):
        pv = _dot(vt_ref[:, g * G * T:(g + 1) * G * T], p_slots[g % 2][...])
        acc_ref[...] = alpha_ref[g % 2] * acc_ref[...] + pv

    def block(g_scores, near, g_values, g_softmax):
        if g_values is not None:
            values_group(g_values)
        if g_softmax is not None:
            sm_slot, p_slot = slots[g_softmax % 2], p_slots[g_softmax % 2]
            m_old = m_ref[...]
            m_new = jnp.maximum(m_old, gm_ref[g_softmax % 2])
            alpha_ref[g_softmax % 2] = jnp.exp2(m_old - m_new)
        gm = None
        for u in range(G):
            if g_scores is not None:
                kt = g_scores * G + u
                s = _dot(k_ref[kt * T:(kt + 1) * T, :], q2)
                if near:
                    tiles = []
                    for t in range(QT):
                        d = qi + t - kt
                        tiles += [bias_ref[0, jnp.where(d < 0, 3, jnp.minimum(d, 2))]] * 2
                    s = s + jnp.concatenate(tiles, axis=1)
                slots[g_scores % 2][u] = s
                cm = jnp.max(s, axis=0, keepdims=True)
                gm = cm if gm is None else jnp.maximum(gm, cm)
            if g_softmax is not None:
                p_slot[u * T:(u + 1) * T, :] = jnp.exp2(sm_slot[u] - m_new).astype(BF16)
        if g_scores is not None:
            gm_ref[g_scores % 2] = gm
        if g_softmax is not None:
            m_ref[...] = m_new

    m_ref[...] = jnp.full(m_ref.shape, NEG, F32)
    acc_ref[...] = jnp.zeros(acc_ref.shape, F32)
    n_far_groups = jnp.maximum(qi - 1, 0) // G

    def stage(cond, g):
        for near in (False, True):
            is_near = g >= n_far_groups
            @pl.when(jnp.logical_and(cond, is_near if near else jnp.logical_not(is_near)))
            def _():
                block(g, near, None, g - 1 if g >= 1 else None)
                if g >= 1:
                    values_group(g - 1)

    stage(True, 0)
    for g in range(n_groups):
        if g + 1 < n_groups:
            stage(g < ng - 1, g + 1)

        @pl.when(g == ng - 1)
        def _():
            block(None, False, None, g)
            values_group(g)

    lp = lam_ref[...]
    lam = (jnp.exp(jnp.sum(lp[0:1] * lp[1:2], axis=-1, keepdims=True))
           - jnp.exp(jnp.sum(lp[2:3] * lp[3:4], axis=-1, keepdims=True)) + lam_init)
    acc = acc_ref[...]
    w = acc[:2 * dh] / acc[2 * dh:2 * dh + 1]
    for t in range(QT):
        c0 = 2 * t * T
        out = (w[:, c0:c0 + T] - lam * w[:, c0 + T:c0 + 2 * T]).T
        ms = jnp.mean(out * out, axis=-1, keepdims=True)
        y = out * lax.rsqrt(ms + RMS_EPS) * nw_ref[...] * (1.0 - lam_init)
        o_ref[t * T:(t + 1) * T, :] = y.astype(o_ref.dtype)


def diff_attention(p_d, bias, lam_params, norm_w, lam_init, B, S):
    T = bias.shape[2]
    G = DIFF_GROUP
    QT = DIFF_QTILES
    assert T >= MAX_DISTANCE and S % (G * T) == 0 and G % QT == 0
    nq = S // (QT * T)
    lanes = 2 * QT * T
    H, hw = DIFF_HEADS, 2 * DIFF_HEAD_DIM
    return pl.pallas_call(
        functools.partial(_diff_kernel, lam_init=lam_init),
        grid=(B, H, nq),
        in_specs=[pl.BlockSpec(lam_params.shape, lambda b, h, i: (0, 0)),
                  pl.BlockSpec((1, hw), lambda b, h, i: (0, 0)),
                  pl.BlockSpec((QT * T, hw), lambda b, h, i: (b * nq + i, h)),
                  pl.BlockSpec((S, hw), lambda b, h, i: (b, H + h)),
                  pl.BlockSpec((S, hw), lambda b, h, i: (b, 2 * H + h)),
                  pl.BlockSpec((1, 4, T, T), lambda b, h, i: (h, 0, 0, 0))],
        out_specs=pl.BlockSpec((QT * T, hw), lambda b, h, i: (b * nq + i, h)),
        out_shape=jax.ShapeDtypeStruct((B * S, H * hw), BF16),
        scratch_shapes=[pltpu.VMEM((hw + BF16_TILE_ROWS, S), BF16),
                        pltpu.VMEM((G, T, lanes), F32), pltpu.VMEM((G, T, lanes), F32),
                        pltpu.VMEM((G * T, lanes), BF16), pltpu.VMEM((G * T, lanes), BF16),
                        pltpu.VMEM((2, 1, lanes), F32), pltpu.VMEM((2, 1, lanes), F32),
                        pltpu.VMEM((1, lanes), F32),
                        pltpu.VMEM((hw + BF16_TILE_ROWS, lanes), F32)],
        compiler_params=_params("parallel", "parallel", "arbitrary"),
        name="diff_attention",
    )(lam_params, norm_w.reshape(1, hw), p_d, p_d, p_d, bias)


def _ret_kernel(lg_ref, nw_ref, cos_ref, sin_ref, q_ref, k_ref, v_ref, g_ref, o_ref,
                st_ref, dec_ref, xi_ref, zeta_ref):
    C = q_ref.shape[0]
    dk, dv = RET_KDIM, RET_VDIM

    @pl.when(pl.program_id(2) == 0)
    def _():
        st_ref[...] = jnp.zeros_like(st_ref)
        rowf = lax.broadcasted_iota(jnp.int32, (C, dk), 0).astype(F32)
        r_i = lax.broadcasted_iota(jnp.int32, (C, C), 0)
        c_i = lax.broadcasted_iota(jnp.int32, (C, C), 1)
        for hh in range(RET_STEP_HEADS):
            lg = lg_ref[hh]
            xi_ref[hh] = jnp.exp((rowf + 1.0) * lg)
            zeta_ref[hh] = jnp.exp((C - 1.0 - rowf) * lg)
            dec_ref[hh] = jnp.exp(jnp.where(r_i >= c_i, (r_i - c_i).astype(F32) * lg[:, 0:1], NEG))

    cosf, sinf = cos_ref[...], sin_ref[...]

    def rot(x):
        return x * cosf + pltpu.roll(x, dk // 2, 1) * sinf

    for hh in range(RET_STEP_HEADS):
        qr = rot(q_ref[:, hh * dk:(hh + 1) * dk])
        kr = rot(k_ref[:, hh * dk:(hh + 1) * dk]) * (dk ** -0.5)
        v = v_ref[:, hh * dv:(hh + 1) * dv]
        state = st_ref[hh]
        scores = _dot_nt(qr.astype(BF16), kr.astype(BF16)) * dec_ref[hh]
        o = _dot(scores.astype(BF16), v) + _dot((qr * xi_ref[hh]).astype(BF16), state.astype(BF16))
        gamma_c = jnp.exp(C * lg_ref[hh][:, 0:1])
        st_ref[hh] = gamma_c * state + _dot((kr * zeta_ref[hh]).T.astype(BF16), v)

        ms = jnp.mean(o * o, axis=-1, keepdims=True)
        g = g_ref[:, hh * dv:(hh + 1) * dv].astype(F32)
        y = o * lax.rsqrt(ms + RMS_EPS) * nw_ref[...] * (g * _sigmoid(g))
        o_ref[:, hh * dv:(hh + 1) * dv] = y.astype(o_ref.dtype)


def retention_mixer(p_qk, p_vg, cos_t, sin_t, norm_w, B, S, C=256):
    C = min(C, S)
    nt = S // C
    H, dk, dv, NH = RET_HEADS, RET_KDIM, RET_VDIM, RET_STEP_HEADS
    HG = H // NH
    log_gamma = jnp.log(1.0 - 2.0 ** (-5.0 - jnp.arange(H, dtype=F32)))
    lg = jnp.broadcast_to(log_gamma[:, None, None], (H, 1, LANES))
    return pl.pallas_call(
        _ret_kernel,
        grid=(B, HG, nt),
        in_specs=[pl.BlockSpec((NH, 1, LANES), lambda b, h, t: (h, 0, 0)),
                  pl.BlockSpec((1, dv), lambda b, h, t: (0, 0)),
                  pl.BlockSpec((C, dk), lambda b, h, t: (t, 0)),
                  pl.BlockSpec((C, dk), lambda b, h, t: (t, 0)),
                  pl.BlockSpec((C, NH * dk), lambda b, h, t: (b * nt + t, h)),
                  pl.BlockSpec((C, NH * dk), lambda b, h, t: (b * nt + t, HG + h)),
                  pl.BlockSpec((C, NH * dv), lambda b, h, t: (b * nt + t, h)),
                  pl.BlockSpec((C, NH * dv), lambda b, h, t: (b * nt + t, HG + h))],
        out_specs=pl.BlockSpec((C, NH * dv), lambda b, h, t: (b * nt + t, h)),
        out_shape=jax.ShapeDtypeStruct((B * S, H * dv), BF16),
        scratch_shapes=[pltpu.VMEM((NH, dk, dv), F32), pltpu.VMEM((NH, C, C), F32),
                        pltpu.VMEM((NH, C, dk), F32), pltpu.VMEM((NH, C, dk), F32)],
        compiler_params=_params("parallel", "parallel", "arbitrary"),
        name="retention_mixer",
    )(lg, norm_w.reshape(1, dv), cos_t, sin_t, p_qk, p_qk, p_vg, p_vg)


def kernel(x, attn_norm_w, w_in, lb_logits, hgrn_norm_w, rel_bias, diff_lambda, diff_norm_w,
           ret_norm_w, w_gate_up, b_gate, w_br_hgrn, w_br_diff, w_br_ret, w_o, ffn_norm_w,
           w_ffn_gate, w_ffn_up, w_ffn_down, final_norm_w):
    B, S, D = x.shape
    depth = w_in.shape[0]
    M = B * S
    xs = x.reshape(M, D)

    cos_t, sin_t = rope_tables(S)
    bias = bias_tiles(rel_bias, min(256, S))

    off_d = 4 * HGRN_WIDTH
    off_rqk = off_d + 3 * DIFF_WIDTH
    off_rvg = off_rqk + 2 * RET_QK_WIDTH
    off_gd = off_rvg + 2 * RET_V_WIDTH

    h, ss = prenorm(xs, attn_norm_w[0])
    for l in range(depth):
        w_in_l = layer_weight_bf16(w_in, l)
        p_h = matmul_cols(h, ss, w_in_l, 0, 4 * HGRN_WIDTH, F32)
        p_d = matmul_cols(h, ss, w_in_l, off_d, 3 * DIFF_WIDTH, BF16)
        p_rqk = matmul_cols(h, ss, w_in_l, off_rqk, 2 * RET_QK_WIDTH, F32)
        p_rvg, y_h = matmul_cols_with_hgrn(h, ss, w_in_l, off_rvg, 2 * RET_V_WIDTH, BF16,
                                           p_h, lb_logits, hgrn_norm_w[l], l, B, S)
        gd = matmul_cols(h, ss, w_in_l, off_gd, GATE_RANK, BF16, tn=GATE_RANK)

        lam_init = 0.8 - 0.6 * math.exp(-0.3 * l)
        y_d = diff_attention(p_d, bias, diff_lambda[l], diff_norm_w[l], lam_init, B, S)
        y_r = retention_mixer(p_rqk, p_rvg, cos_t, sin_t, ret_norm_w[l], B, S)

        merged = gated_merge(y_h, y_d, y_r, gd,
                             layer_weight_bf16(w_br_hgrn, l), layer_weight_bf16(w_br_diff, l),
                             layer_weight_bf16(w_br_ret, l), layer_weight_bf16(w_gate_up, l), b_gate[l])
        xs, h2, ss2 = matmul_residual(merged, layer_weight_bf16(w_o, l), xs, ffn_norm_w[l],
                                      in_place=l > 0)
        act = swiglu_up(h2, ss2, w_ffn_gate, w_ffn_up, l)
        w_down = layer_weight_bf16(w_ffn_down, l)
        if l + 1 < depth:
            xs, h, ss = matmul_residual(act, w_down, xs, attn_norm_w[l + 1], tn=256, a_buffers=1)
        else:
            xs = matmul_residual(act, w_down, xs, tn=256, a_buffers=1)

    out = rms_norm_rows(xs, final_norm_w, x.dtype)
    return out.reshape(B, S, D)
```

```python
import functools
import math

import jax
import jax.numpy as jnp
from jax import lax
from jax.experimental import pallas as pl
from jax.experimental.pallas import tpu as pltpu

F32 = jnp.float32
BF16 = jnp.bfloat16

HGRN_HEADS = 8
HGRN_DIM = 128
DIFF_HEADS = 8
DIFF_HEAD_DIM = 64
RET_HEADS = 8
RET_KDIM = 128
RET_VDIM = 256
N_BUCKETS = 32
MAX_DISTANCE = 128
ROPE_BASE = 10000.0
GATE_RANK = 256
RMS_EPS = 1e-6
HGRN_WIDTH = HGRN_HEADS * HGRN_DIM
DIFF_WIDTH = DIFF_HEADS * 2 * DIFF_HEAD_DIM
RET_QK_WIDTH = RET_HEADS * RET_KDIM
RET_V_WIDTH = RET_HEADS * RET_VDIM

V7X_VMEM_BYTES = 64 * 1024 * 1024
VMEM_LIMIT = V7X_VMEM_BYTES - 8 * 1024 * 1024
LANES = 128
BF16_TILE_ROWS = 16
F32_TILE_ROWS = 8
LOG2E = math.log2(math.e)

NEG = -1e30
HGRN_CHUNK = 128
HGRN_SUB = 16
DIFF_GROUP = 4
DIFF_QTILES = 2
RET_STEP_HEADS = 4
MM_SUB_ROWS = 512
HGRN_STEP_HEADS = 4


def _params(*sem):
    return pltpu.CompilerParams(dimension_semantics=sem, vmem_limit_bytes=VMEM_LIMIT)


def _sigmoid(x):
    return 1.0 / (1.0 + jnp.exp(-x))


def _dot(a, b):
    return jnp.dot(a, b, preferred_element_type=F32)


def _dot_nt(a, b):
    return lax.dot_general(a, b, (((1,), (1,)), ((), ())), preferred_element_type=F32)


def _rms_kernel(x_ref, w_ref, o_ref):
    x = x_ref[...]
    ms = jnp.mean(x * x, axis=-1, keepdims=True)
    o_ref[...] = (x * lax.rsqrt(ms + RMS_EPS) * w_ref[...]).astype(o_ref.dtype)


def rms_norm_rows(x, w, out_dtype, tm=256):
    M, D = x.shape
    tm = min(tm, M)
    return pl.pallas_call(
        _rms_kernel,
        grid=(M // tm,),
        in_specs=[pl.BlockSpec((tm, D), lambda i: (i, 0)),
                  pl.BlockSpec((1, D), lambda i: (0, 0))],
        out_specs=pl.BlockSpec((tm, D), lambda i: (i, 0)),
        out_shape=jax.ShapeDtypeStruct((M, D), out_dtype),
        compiler_params=_params("parallel"),
        name="rmsnorm",
    )(x, w.reshape(1, D))


def _cast_kernel(w_ref, o_ref):
    o_ref[...] = w_ref[...].astype(o_ref.dtype)


def layer_weight_bf16(w, layer, block_bytes=8 * 1024 * 1024):
    _, R, C = w.shape
    rb = R
    while rb * C * 4 > block_bytes and rb % (2 * BF16_TILE_ROWS) == 0:
        rb //= 2
    assert R % rb == 0
    return pl.pallas_call(
        _cast_kernel,
        grid=(R // rb,),
        in_specs=[pl.BlockSpec((None, rb, C), lambda i: (layer, i, 0))],
        out_specs=pl.BlockSpec((rb, C), lambda i: (i, 0)),
        out_shape=jax.ShapeDtypeStruct((R, C), BF16),
        compiler_params=_params("parallel"),
        name="weight_bf16",
    )(w)


def _with_side_cast(kernel_fn, n_in, n_out):
    def wrapped(*refs):
        side_in, side_out = refs[n_in], refs[n_in + 1 + n_out]
        side_out[...] = side_in[...].astype(side_out.dtype)
        kernel_fn(*refs[:n_in], *refs[n_in + 1:n_in + 1 + n_out], *refs[n_in + 2 + n_out:])
    return wrapped


def _side_cast_specs(w, layer, grid):
    _, R, C = w.shape
    steps, nj = grid[0] * grid[1], grid[1]
    if R % steps or (R // steps) % BF16_TILE_ROWS:
        return None
    rb = R // steps
    return (pl.BlockSpec((None, rb, C), lambda i, j: (layer, i * nj + j, 0)),
            pl.BlockSpec((rb, C), lambda i, j: (i * nj + j, 0)),
            jax.ShapeDtypeStruct((R, C), BF16))


def _lane_partial_sumsq(x):
    sq = x * x
    part = sq[:, 0:LANES]
    for k in range(1, x.shape[1] // LANES):
        part = part + sq[:, k * LANES:(k + 1) * LANES]
    return part


def _row_scale(ss, d_model):
    ms = jnp.sum(ss, axis=-1, keepdims=True) / d_model
    return jnp.broadcast_to(lax.rsqrt(ms + RMS_EPS), ss.shape)


def _prenorm_kernel(x_ref, w_ref, xw_ref, ss_ref):
    x = x_ref[...]
    xw_ref[...] = (x * w_ref[...]).astype(xw_ref.dtype)
    ss_ref[...] = _lane_partial_sumsq(x)


def prenorm(x, w, tm=256):
    M, D = x.shape
    tm = min(tm, M)
    return pl.pallas_call(
        _prenorm_kernel,
        grid=(M // tm,),
        in_specs=[pl.BlockSpec((tm, D), lambda i: (i, 0)),
                  pl.BlockSpec((1, D), lambda i: (0, 0))],
        out_specs=[pl.BlockSpec((tm, D), lambda i: (i, 0)),
                   pl.BlockSpec((tm, LANES), lambda i: (i, 0))],
        out_shape=[jax.ShapeDtypeStruct((M, D), BF16), jax.ShapeDtypeStruct((M, LANES), F32)],
        compiler_params=_params("parallel"),
        name="prenorm",
    )(x, w.reshape(1, D))


def _row_blocks(tm):
    sub = min(tm, MM_SUB_ROWS)
    return [slice(r, r + sub) for r in range(0, tm, sub)]


def _mm_kernel(a_ref, ss_ref, b_ref, o_ref, r_ref):
    @pl.when(pl.program_id(1) == 0)
    def _():
        r_ref[...] = _row_scale(ss_ref[...], a_ref.shape[1])
    scale = jnp.tile(r_ref[...], (1, o_ref.shape[1] // LANES))
    o_ref[...] = (_dot(a_ref[...], b_ref[...]) * scale).astype(o_ref.dtype)


def _call_2d(kernel_fn, name, grid, in_specs, out_spec, out_shape, args, side_cast, scratch_shapes=()):
    side = _side_cast_specs(*side_cast, grid) if side_cast is not None else None
    if side is None:
        out = pl.pallas_call(
            kernel_fn, grid=grid, in_specs=in_specs, out_specs=out_spec, out_shape=out_shape,
            scratch_shapes=list(scratch_shapes),
            compiler_params=_params("parallel", "arbitrary"), name=name,
        )(*args)
        return out if side_cast is None else (out, None)
    return pl.pallas_call(
        _with_side_cast(kernel_fn, len(in_specs), 1), grid=grid,
        in_specs=list(in_specs) + [side[0]], out_specs=[out_spec, side[1]],
        out_shape=[out_shape, side[2]], scratch_shapes=list(scratch_shapes),
        compiler_params=_params("parallel", "arbitrary"), name=name + "_cast",
    )(*args, side_cast[0])


def matmul_cols(a, ss, b, col_off, n, out_dtype, tm=1024, tn=1024, side_cast=None):
    M, K = a.shape
    tm, tn = min(tm, M), min(tn, n)
    assert col_off % tn == 0 and n % tn == 0 and M % tm == 0
    off = col_off // tn
    return _call_2d(
        _mm_kernel, "matmul_cols", (M // tm, n // tn),
        [pl.BlockSpec((tm, K), lambda i, j: (i, 0)),
         pl.BlockSpec((tm, LANES), lambda i, j: (i, 0)),
         pl.BlockSpec((K, tn), lambda i, j: (0, j + off))],
        pl.BlockSpec((tm, tn), lambda i, j: (i, j)),
        jax.ShapeDtypeStruct((M, n), out_dtype),
        (a, ss, b), side_cast, scratch_shapes=[pltpu.VMEM((tm, LANES), F32)])


def _mm_res_kernel(a_ref, b_ref, x_ref, o_ref):
    o_ref[...] = x_ref[...] + _dot(a_ref[...], b_ref[...])


def _mm_res_norm_kernel(a_ref, b_ref, x_ref, w_ref, o_ref, xw_ref, ss_ref):
    y = x_ref[...] + _dot(a_ref[...], b_ref[...])
    o_ref[...] = y
    xw_ref[...] = (y * w_ref[...]).astype(xw_ref.dtype)
    part = _lane_partial_sumsq(y)

    @pl.when(pl.program_id(1) == 0)
    def _():
        ss_ref[...] = part

    @pl.when(pl.program_id(1) > 0)
    def _():
        ss_ref[...] = ss_ref[...] + part


def matmul_residual(a, b, x, next_norm_w=None, in_place=True, tm=1024, tn=512, a_buffers=2,
                    side_cast=None):
    M, K = a.shape
    N = b.shape[1]
    tm, tn = min(tm, M), min(tn, N)
    assert M % tm == 0 and N % tn == 0
    tile = pl.BlockSpec((tm, tn), lambda i, j: (i, j))
    in_specs = [pl.BlockSpec((tm, K), lambda i, j: (i, 0), pipeline_mode=pl.Buffered(a_buffers)),
                pl.BlockSpec((K, tn), lambda i, j: (0, j)),
                tile]
    aliases = {2: 0} if in_place else {}
    grid = (M // tm, N // tn)
    side = _side_cast_specs(*side_cast, grid) if side_cast is not None else None
    if side is not None:
        assert next_norm_w is not None
        out = pl.pallas_call(
            _with_side_cast(_mm_res_norm_kernel, 4, 3),
            grid=grid,
            in_specs=in_specs + [pl.BlockSpec((1, tn), lambda i, j: (0, j)), side[0]],
            out_specs=[tile, tile, pl.BlockSpec((tm, LANES), lambda i, j: (i, 0)), side[1]],
            out_shape=[jax.ShapeDtypeStruct((M, N), F32), jax.ShapeDtypeStruct((M, N), BF16),
                       jax.ShapeDtypeStruct((M, LANES), F32), side[2]],
            input_output_aliases=aliases,
            compiler_params=_params("parallel", "arbitrary"),
            name="matmul_residual_norm_cast",
        )(a, b, x, next_norm_w.reshape(1, N), side_cast[0])
        return out
    if side_cast is not None:
        return (*matmul_residual(a, b, x, next_norm_w, in_place, tm, tn, a_buffers), None)
    if next_norm_w is None:
        return pl.pallas_call(
            _mm_res_kernel,
            grid=(M // tm, N // tn),
            in_specs=in_specs,
            out_specs=tile,
            out_shape=jax.ShapeDtypeStruct((M, N), F32),
            input_output_aliases=aliases,
            compiler_params=_params("parallel", "arbitrary"),
            name="matmul_residual",
        )(a, b, x)
    return pl.pallas_call(
        _mm_res_norm_kernel,
        grid=(M // tm, N // tn),
        in_specs=in_specs + [pl.BlockSpec((1, tn), lambda i, j: (0, j))],
        out_specs=[tile, tile, pl.BlockSpec((tm, LANES), lambda i, j: (i, 0))],
        out_shape=[jax.ShapeDtypeStruct((M, N), F32), jax.ShapeDtypeStruct((M, N), BF16),
                   jax.ShapeDtypeStruct((M, LANES), F32)],
        input_output_aliases=aliases,
        compiler_params=_params("parallel", "arbitrary"),
        name="matmul_residual_norm",
    )(a, b, x, next_norm_w.reshape(1, N))


def _swiglu_kernel(a_ref, ss_ref, wg_ref, wu_ref, o_ref, r_ref):
    @pl.when(pl.program_id(1) == 0)
    def _():
        r_ref[...] = _row_scale(ss_ref[...], a_ref.shape[1])
    wg = wg_ref[...].astype(BF16)
    wu = wu_ref[...].astype(BF16)
    for rows in _row_blocks(a_ref.shape[0]):
        scale = jnp.tile(r_ref[rows, :], (1, o_ref.shape[1] // LANES))
        a = a_ref[rows, :]
        g = _dot(a, wg) * scale
        u = _dot(a, wu) * scale
        o_ref[rows, :] = (g * _sigmoid(g) * u).astype(o_ref.dtype)


def swiglu_up(a, ss, wg, wu, layer, side_cast, tm=2048, tn=256):
    M, K = a.shape
    N = wg.shape[2]
    tm, tn = min(tm, M), min(tn, N)
    assert M % tm == 0 and N % tn == 0
    grid = (M // tm, N // tn)
    in_specs = [pl.BlockSpec((tm, K), lambda i, j: (i, 0), pipeline_mode=pl.Buffered(1)),
                pl.BlockSpec((tm, LANES), lambda i, j: (i, 0)),
                pl.BlockSpec((None, K, tn), lambda i, j: (layer, 0, j)),
                pl.BlockSpec((None, K, tn), lambda i, j: (layer, 0, j))]
    tile = pl.BlockSpec((tm, tn), lambda i, j: (i, j))
    side = _side_cast_specs(*side_cast, grid)
    if side is None:
        return pl.pallas_call(
            _swiglu_kernel,
            grid=grid,
            in_specs=in_specs,
            out_specs=tile,
            out_shape=jax.ShapeDtypeStruct((M, N), BF16),
            scratch_shapes=[pltpu.VMEM((tm, LANES), F32)],
            compiler_params=_params("parallel", "arbitrary"),
            name="swiglu_up",
        )(a, ss, wg, wu), None
    return pl.pallas_call(
        _with_side_cast(_swiglu_kernel, 4, 1),
        grid=grid,
        in_specs=in_specs + [side[0]],
        out_specs=[tile, side[1]],
        out_shape=[jax.ShapeDtypeStruct((M, N), BF16), side[2]],
        scratch_shapes=[pltpu.VMEM((tm, LANES), F32)],
        compiler_params=_params("parallel", "arbitrary"),
        name="swiglu_up_cast",
    )(a, ss, wg, wu, side_cast[0])


def _merge_kernel(yh_ref, yd_ref, yr_ref, gd_ref, wh_ref, wd_ref, wr_ref,
                  wgh_ref, wgd_ref, wgr_ref, bh_ref, bd_ref, br_ref, o_ref):
    gd = gd_ref[...]

    def branch(y_ref, w_ref, wg_ref, b_ref):
        gate = _sigmoid(_dot(gd, wg_ref[...]) + b_ref[...])
        return gate * _dot(y_ref[...], w_ref[...])

    acc = branch(yh_ref, wh_ref, wgh_ref, bh_ref)
    acc = acc + branch(yd_ref, wd_ref, wgd_ref, bd_ref)
    acc = acc + branch(yr_ref, wr_ref, wgr_ref, br_ref)
    o_ref[...] = acc.astype(o_ref.dtype)


def gated_merge(yh, yd, yr, gd, wh, wd, wr, wg, bg, tm=1024, tn=512, side_cast=None):
    M = yh.shape[0]
    D = wh.shape[1]
    tm, tn = min(tm, M), min(tn, D)
    assert M % tm == 0 and D % tn == 0
    nb = D // tn
    row = lambda width: pl.BlockSpec((tm, width), lambda i, j: (i, 0))
    wcol = lambda k: pl.BlockSpec((k, tn), lambda i, j: (0, j))
    gcol = lambda br: pl.BlockSpec((GATE_RANK, tn), lambda i, j: (0, j + br * nb))
    bcol = lambda br: pl.BlockSpec((1, tn), lambda i, j: (0, j + br * nb))
    bg2 = bg.reshape(1, -1)
    return _call_2d(
        _merge_kernel, "gated_merge", (M // tm, nb),
        [row(yh.shape[1]), row(yd.shape[1]), row(yr.shape[1]), row(gd.shape[1]),
         wcol(wh.shape[0]), wcol(wd.shape[0]), wcol(wr.shape[0]),
         gcol(0), gcol(1), gcol(2), bcol(0), bcol(1), bcol(2)],
        pl.BlockSpec((tm, tn), lambda i, j: (i, j)),
        jax.ShapeDtypeStruct((M, D), BF16),
        (yh, yd, yr, gd, wh, wd, wr, wg, wg, wg, bg2, bg2, bg2), side_cast)


def _rope_table_kernel(inv_ref, cos_ref, sin_ref):
    T = cos_ref.shape[0]
    pos = (lax.broadcasted_iota(jnp.int32, (T, LANES), 0) + pl.program_id(0) * T).astype(F32)
    ang = pos * inv_ref[...]
    lane = lax.broadcasted_iota(jnp.int32, (T, LANES), 1)
    cos_ref[...] = jnp.cos(ang)
    sin_ref[...] = jnp.where(lane < LANES // 2, -jnp.sin(ang), jnp.sin(ang))


def rope_tables(S, T=256):
    T = min(T, S)
    half = RET_KDIM // 2
    inv = 1.0 / (ROPE_BASE ** (jnp.arange(half, dtype=F32) / half))
    inv2 = jnp.concatenate([inv, inv]).reshape(1, LANES)
    return pl.pallas_call(
        _rope_table_kernel,
        grid=(S // T,),
        in_specs=[pl.BlockSpec((1, LANES), lambda i: (0, 0))],
        out_specs=[pl.BlockSpec((T, LANES), lambda i: (i, 0))] * 2,
        out_shape=[jax.ShapeDtypeStruct((S, LANES), F32)] * 2,
        compiler_params=_params("parallel"),
        name="rope_tables",
    )(inv2)


def _bias_tile_kernel(tab_ref, o_ref):
    h = pl.program_id(0)
    T = o_ref.shape[2]
    j = lax.broadcasted_iota(jnp.int32, (T, T), 0)
    i = lax.broadcasted_iota(jnp.int32, (T, T), 1)
    max_exact = N_BUCKETS // 2
    o_ref[0, 3] = jnp.full((T, T), NEG, F32)
    for d in range(3):
        n = jnp.maximum(i - j + d * T, 0)
        nf = jnp.maximum(n, 1).astype(F32)
        large = max_exact + (jnp.log(nf / max_exact) / math.log(MAX_DISTANCE / max_exact)
                             * (N_BUCKETS - max_exact)).astype(jnp.int32)
        large = jnp.minimum(large, N_BUCKETS - 1)
        bucket = jnp.where(n < max_exact, n, large)
        val = jnp.zeros((T, T), F32)
        for b in range(N_BUCKETS):
            val = jnp.where(bucket == b, tab_ref[h, b], val)
        val = (val - tab_ref[h, N_BUCKETS - 1]) * LOG2E
        if d == 0:
            val = jnp.where(j > i, NEG, val)
        o_ref[0, d] = val


def bias_tiles(rel_bias, T):
    H = rel_bias.shape[1]
    return pl.pallas_call(
        _bias_tile_kernel,
        grid=(H,),
        in_specs=[pl.BlockSpec(memory_space=pltpu.SMEM)],
        out_specs=pl.BlockSpec((1, 4, T, T), lambda h: (h, 0, 0, 0)),
        out_shape=jax.ShapeDtypeStruct((H, 4, T, T), F32),
        compiler_params=_params("parallel"),
        name="bias_tiles",
    )(rel_bias.T)


def _hgrn_kernel(lbl_ref, nw_ref, q_ref, f_ref, i_ref, g_ref, o_ref,
                 st_ref, p_ref, r_ref, ol_ref, *, layer):
    @pl.when(pl.program_id(2) == 0)
    def _():
        st_ref[...] = jnp.zeros_like(st_ref)

    for hh in range(HGRN_STEP_HEADS):
        lanes = slice(hh * HGRN_DIM, (hh + 1) * HGRN_DIM)
        _hgrn_head(lbl_ref[:, lanes], nw_ref[...], q_ref[:, lanes], f_ref[:, lanes], i_ref[:, lanes],
                   g_ref[:, lanes], o_ref.at[:, lanes], st_ref.at[hh], p_ref.at[hh], r_ref.at[hh],
                   ol_ref.at[hh], layer)


def _hgrn_head(lg, nw, q, f, iv_all, g_all, o_ref, st_ref, p_ref, r_ref, ol_ref, layer):
    bcum, qf, kf = _hgrn_head_gates(lg, q, f, layer)
    _hgrn_head_mix(bcum, qf, kf, nw, iv_all, g_all, o_ref, st_ref, p_ref, r_ref, ol_ref)


def _hgrn_head_gates(lg, q, f, layer):
    T = q.shape[0]
    C = HGRN_CHUNK

    e = jnp.exp(lg - jnp.max(lg, axis=0, keepdims=True))
    sm = e / jnp.sum(e, axis=0, keepdims=True)
    csum = sm[0:1]
    for r in range(1, layer + 1):
        csum = csum + sm[r:r + 1]
    lb = csum - sm[0:1]

    forget = lb + (1.0 - lb) * _sigmoid(f)
    kf = 1.0 - forget
    logf = jnp.log(forget)
    qf = q * _sigmoid(q) * (HGRN_DIM ** -0.5)

    r_i = lax.broadcasted_iota(jnp.int32, (T, T), 0)
    c_i = lax.broadcasted_iota(jnp.int32, (T, T), 1)
    shift = C.bit_length() - 1
    same_chunk = (r_i >> shift) == (c_i >> shift)
    tri = jnp.where(c_i <= r_i, jnp.where(same_chunk, 1.0, 0.0), 0.0).astype(BF16)
    hi = logf.astype(BF16)
    rem = logf - hi.astype(F32)
    mid = rem.astype(BF16)
    lo = (rem - mid.astype(F32)).astype(BF16)
    bcum = (_dot(tri, hi) + _dot(tri, mid) + _dot(tri, lo)) * LOG2E
    return bcum, qf, kf


def _hgrn_head_mix(bcum, qf, kf, nw, iv_all, g_all, o_ref, st_ref, p_ref, r_ref, ol_ref):
    T = qf.shape[0]
    C, SUB = HGRN_CHUNK, HGRN_SUB
    row_c = lax.broadcasted_iota(jnp.int32, (C, HGRN_DIM), 0)
    row_s = lax.broadcasted_iota(jnp.int32, (SUB, HGRN_DIM), 0)
    d_i = lax.broadcasted_iota(jnp.int32, (2 * HGRN_DIM, 2 * HGRN_DIM), 0)
    d_j = lax.broadcasted_iota(jnp.int32, (2 * HGRN_DIM, 2 * HGRN_DIM), 1)
    ones2 = jnp.where((d_i < HGRN_DIM) == (d_j < HGRN_DIM), 1.0, 0.0).astype(BF16)

    def slot(r):
        half, rb = divmod(r, T // 2)
        return rb * SUB, slice(half * HGRN_DIM, (half + 1) * HGRN_DIM)

    n_sub = T // SUB
    causal_add = [jnp.where(row_s[(s // F32_TILE_ROWS) * F32_TILE_ROWS:] >= s, 0.0, NEG)
                  for s in range(SUB)]
    for sb in range(n_sub):
        sub = slice(sb * SUB, (sb + 1) * SUB)
        bi, qi = bcum[sub], qf[sub]
        for s in range(SUB):
            r = sb * SUB + s
            top = (s // F32_TILE_ROWS) * F32_TILE_ROWS
            dec = jnp.exp2(bi[top:] - bcum[r:r + 1] + causal_add[s])
            p = qi[top:] * kf[r:r + 1] * dec
            if top:
                p = jnp.concatenate([jnp.zeros((top, HGRN_DIM), F32), p], axis=0)
            r0, lanes = slot(r)
            p_ref[r0:r0 + SUB, lanes] = p.astype(BF16)
    r_ref[...] = _dot(p_ref[...], ones2)

    n_blk = C // SUB
    for c in range(T // C):
        c0 = c * C
        b, kc = bcum[c0:c0 + C], kf[c0:c0 + C]
        qts, kts = [], []
        for j in range(1, n_blk):
            lo_r = j * SUB
            ref = b[lo_r - 1:lo_r]
            qts.append((qf[c0 + lo_r:c0 + lo_r + SUB] * jnp.exp2(b[lo_r:lo_r + SUB] - ref)).astype(BF16))
            kts.append((kc * jnp.exp2(jnp.where(row_c < lo_r, ref - b, NEG))).astype(BF16))
        att_all = _dot_nt(jnp.concatenate(qts, axis=0), jnp.concatenate(kts, axis=0))
        att = jnp.concatenate([att_all[j * SUB:(j + 1) * SUB, j * C:(j + 1) * C]
                               for j in range(n_blk - 1)], axis=0)
        off = _dot(att.astype(BF16), iv_all[c0:c0 + C].astype(BF16))
        for j in range(n_blk):
            lo_r = c0 + j * SUB
            r0, lanes = slot(lo_r)
            oi = r_ref[r0:r0 + SUB, lanes] * iv_all[lo_r:lo_r + 1]
            for s in range(1, SUB):
                r = lo_r + s
                top = (s // F32_TILE_ROWS) * F32_TILE_ROWS
                r0, lanes = slot(r)
                term = r_ref[r0 + top:r0 + SUB, lanes] * iv_all[r:r + 1]
                oi = oi + term if top == 0 else jnp.concatenate([oi[:top], oi[top:] + term], axis=0)
            if j > 0:
                oi = oi + off[(j - 1) * SUB:j * SUB]
            ol_ref[lo_r:lo_r + SUB, :] = oi

    state_t = st_ref[...]
    for c in range(T // C):
        rows = slice(c * C, (c + 1) * C)
        b, qc, kc, iv = bcum[rows], qf[rows], kf[rows], iv_all[rows]
        o = ol_ref[rows, :] + _dot_nt((qc * jnp.exp2(b)).astype(BF16), state_t.astype(BF16))
        b_last = b[C - 1:C]
        khat = (kc * jnp.exp2(b_last - b)).astype(BF16)
        state_t = state_t * jnp.exp2(b_last) + _dot(iv.T.astype(BF16), khat)
        ms = jnp.mean(o * o, axis=-1, keepdims=True)
        gc = g_all[rows]
        y = o * lax.rsqrt(ms + RMS_EPS) * nw * (gc * _sigmoid(gc))
        o_ref[rows, :] = y.astype(o_ref.dtype)
    st_ref[...] = state_t


def hgrn_mixer(p_h, lb_logits, norm_w, layer, B, S, T=256):
    T = min(T, S)
    nt = S // T
    H, dk, NH = HGRN_HEADS, HGRN_DIM, HGRN_STEP_HEADS
    HG = H // NH
    col = lambda grp: pl.BlockSpec((T, NH * dk), lambda b, h, t: (b * nt + t, grp * HG + h))
    return pl.pallas_call(
        functools.partial(_hgrn_kernel, layer=layer),
        grid=(B, HG, nt),
        in_specs=[pl.BlockSpec((lb_logits.shape[0], NH * dk), lambda b, h, t: (0, h)),
                  pl.BlockSpec((1, dk), lambda b, h, t: (0, 0)),
                  col(0), col(1), col(2), col(3)],
        out_specs=pl.BlockSpec((T, NH * dk), lambda b, h, t: (b * nt + t, h)),
        out_shape=jax.ShapeDtypeStruct((B * S, H * dk), BF16),
        scratch_shapes=[pltpu.VMEM((NH, dk, dk), F32),
                        pltpu.VMEM((NH, T // 2 * HGRN_SUB, 2 * dk), BF16),
                        pltpu.VMEM((NH, T // 2 * HGRN_SUB, 2 * dk), F32),
                        pltpu.VMEM((NH, T, dk), F32)],
        compiler_params=_params("parallel", "parallel", "arbitrary"),
        name="hgrn_mixer",
    )(lb_logits, norm_w.reshape(1, dk), p_h, p_h, p_h, p_h)


def _mm_hgrn_kernel(a_ref, ss_ref, b_ref, lbl_ref, nw_ref, q_ref, f_ref, i_ref, g_ref,
                    o_ref, y_ref, r_ref, st_ref, p_ref, r2_ref, ol_ref,
                    *, layer, n_col_blocks, n_time_blocks):
    step = pl.program_id(0)

    @pl.when(step % n_col_blocks == 0)
    def _():
        r_ref[...] = _row_scale(ss_ref[...], a_ref.shape[1])

    @pl.when(step % n_time_blocks == 0)
    def _():
        st_ref[...] = jnp.zeros_like(st_ref)

    b = b_ref[...]
    sub = a_ref.shape[0] // HGRN_STEP_HEADS
    for hh in range(HGRN_STEP_HEADS):
        lanes = slice(hh * HGRN_DIM, (hh + 1) * HGRN_DIM)
        bcum, qf, kf = _hgrn_head_gates(lbl_ref[:, lanes], q_ref[:, lanes], f_ref[:, lanes], layer)
        rows = slice(hh * sub, (hh + 1) * sub)
        scale = jnp.tile(r_ref[rows, :], (1, o_ref.shape[1] // LANES))
        o_ref[rows, :] = (_dot(a_ref[rows, :], b) * scale).astype(o_ref.dtype)
        _hgrn_head_mix(bcum, qf, kf, nw_ref[...], i_ref[:, lanes], g_ref[:, lanes],
                       y_ref.at[:, lanes], st_ref.at[hh], p_ref.at[hh], r2_ref.at[hh], ol_ref.at[hh])


def matmul_cols_with_hgrn(a, ss, b, col_off, n, out_dtype, p_h, lb_logits, norm_w, layer, B, S,
                          tm=1024, tn=512, T=256):
    M, K = a.shape
    tm, tn, T = min(tm, M), min(tn, n), min(T, S)
    nt = S // T
    H, dk, NH = HGRN_HEADS, HGRN_DIM, HGRN_STEP_HEADS
    HG = H // NH
    ncb = n // tn
    if (M // tm) * ncb != B * HG * nt:
        return (matmul_cols(a, ss, b, col_off, n, out_dtype),
                hgrn_mixer(p_h, lb_logits, norm_w, layer, B, S))
    assert col_off % tn == 0 and n % tn == 0 and M % tm == 0
    off = col_off // tn
    row_blk = lambda s: (s // (HG * nt)) * nt + s % nt
    head_grp = lambda s: (s // nt) % HG
    col = lambda grp: pl.BlockSpec((T, NH * dk), lambda s: (row_blk(s), grp * HG + head_grp(s)))
    return pl.pallas_call(
        functools.partial(_mm_hgrn_kernel, layer=layer, n_col_blocks=ncb, n_time_blocks=nt),
        grid=((M // tm) * ncb,),
        in_specs=[pl.BlockSpec((tm, K), lambda s: (s // ncb, 0)),
                  pl.BlockSpec((tm, LANES), lambda s: (s // ncb, 0)),
                  pl.BlockSpec((K, tn), lambda s: (0, s % ncb + off)),
                  pl.BlockSpec((lb_logits.shape[0], NH * dk), lambda s: (0, head_grp(s))),
                  pl.BlockSpec((1, dk), lambda s: (0, 0)),
                  col(0), col(1), col(2), col(3)],
        out_specs=[pl.BlockSpec((tm, tn), lambda s: (s // ncb, s % ncb)),
                   pl.BlockSpec((T, NH * dk), lambda s: (row_blk(s), head_grp(s)))],
        out_shape=[jax.ShapeDtypeStruct((M, n), out_dtype),
                   jax.ShapeDtypeStruct((B * S, H * dk), BF16)],
        scratch_shapes=[pltpu.VMEM((tm, LANES), F32),
                        pltpu.VMEM((NH, dk, dk), F32),
                        pltpu.VMEM((NH, T // 2 * HGRN_SUB, 2 * dk), BF16),
                        pltpu.VMEM((NH, T // 2 * HGRN_SUB, 2 * dk), F32),
                        pltpu.VMEM((NH, T, dk), F32)],
        compiler_params=_params("arbitrary"),
        name="matmul_cols_hgrn",
    )(a, ss, b, lb_logits, norm_w.reshape(1, dk), p_h, p_h, p_h, p_h)


def _diff_kernel(lam_ref, nw_ref, q_ref, k_ref, v_ref, bias_ref, o_ref,
                 vt_ref, s0_ref, s1_ref, p0_ref, p1_ref, gm_ref, alpha_ref, m_ref, acc_ref, *, lam_init):
    QT = DIFF_QTILES
    T = q_ref.shape[0] // QT
    S = k_ref.shape[0]
    dh = DIFF_HEAD_DIM
    G = DIFF_GROUP
    n_groups = S // (G * T)
    qi = pl.program_id(2) * QT
    ng = (qi + QT - 1) // G + 1

    @pl.when(qi == 0)
    def _():
        def body(c, carry):
            start = pl.multiple_of(c * T, T)
            vt_ref[0:2 * dh, pl.ds(start, T)] = v_ref[pl.ds(start, T), :].astype(F32).T.astype(BF16)
            return carry
        lax.fori_loop(0, S // T, body, 0)
        pad_row = lax.broadcasted_iota(jnp.int32, (BF16_TILE_ROWS, S), 0)
        vt_ref[2 * dh:, :] = jnp.where(pad_row == 0, 1.0, 0.0).astype(BF16)

    row = lax.broadcasted_iota(jnp.int32, (2 * dh, T), 0)
    cols = []
    for t in range(QT):
        qt = q_ref[t * T:(t + 1) * T, :].astype(F32).T * (dh ** -0.5 * LOG2E)
        cols += [jnp.where(row < dh, qt, 0.0), jnp.where(row >= dh, qt, 0.0)]
    q2 = jnp.concatenate(cols, axis=1).astype(BF16)

    slots = (s0_ref, s1_ref)

    p_slots = (p0_ref, p1_ref)

    def values_group(g):
        pv = _dot(vt_ref[:, g * G * T:(g + 1) * G * T], p_slots[g % 2][...])
        acc_ref[...] = alpha_ref[g % 2] * acc_ref[...] + pv

    def block(g_scores, near, g_values, g_softmax):
        if g_values is not None:
            values_group(g_values)
        if g_softmax is not None:
            sm_slot, p_slot = slots[g_softmax % 2], p_slots[g_softmax % 2]
            m_old = m_ref[...]
            m_new = jnp.maximum(m_old, gm_ref[g_softmax % 2])
            alpha_ref[g_softmax % 2] = jnp.exp2(m_old - m_new)
        gm = None
        for u in range(G):
            if g_scores is not None:
                kt = g_scores * G + u
                s = _dot(k_ref[kt * T:(kt + 1) * T, :], q2)
                if near:
                    tiles = []
                    for t in range(QT):
                        d = qi + t - kt
                        tiles += [bias_ref[0, jnp.where(d < 0, 3, jnp.minimum(d, 2))]] * 2
                    s = s + jnp.concatenate(tiles, axis=1)
                slots[g_scores % 2][u] = s
                cm = jnp.max(s, axis=0, keepdims=True)
                gm = cm if gm is None else jnp.maximum(gm, cm)
            if g_softmax is not None:
                p_slot[u * T:(u + 1) * T, :] = jnp.exp2(sm_slot[u] - m_new).astype(BF16)
        if g_scores is not None:
            gm_ref[g_scores % 2] = gm
        if g_softmax is not None:
            m_ref[...] = m_new

    m_ref[...] = jnp.full(m_ref.shape, NEG, F32)
    acc_ref[...] = jnp.zeros(acc_ref.shape, F32)
    n_far_groups = jnp.maximum(qi - 1, 0) // G

    def stage(cond, g):
        for near in (False, True):
            is_near = g >= n_far_groups
            @pl.when(jnp.logical_and(cond, is_near if near else jnp.logical_not(is_near)))
            def _():
                block(g, near, None, g - 1 if g >= 1 else None)
                if g >= 1:
                    values_group(g - 1)

    stage(True, 0)
    for g in range(n_groups):
        if g + 1 < n_groups:
            stage(g < ng - 1, g + 1)

        @pl.when(g == ng - 1)
        def _():
            block(None, False, None, g)
            values_group(g)

    lp = lam_ref[...]
    lam = (jnp.exp(jnp.sum(lp[0:1] * lp[1:2], axis=-1, keepdims=True))
           - jnp.exp(jnp.sum(lp[2:3] * lp[3:4], axis=-1, keepdims=True)) + lam_init)
    acc = acc_ref[...]
    w = acc[:2 * dh] / acc[2 * dh:2 * dh + 1]
    for t in range(QT):
        c0 = 2 * t * T
        out = (w[:, c0:c0 + T] - lam * w[:, c0 + T:c0 + 2 * T]).T
        ms = jnp.mean(out * out, axis=-1, keepdims=True)
        y = out * lax.rsqrt(ms + RMS_EPS) * nw_ref[...] * (1.0 - lam_init)
        o_ref[t * T:(t + 1) * T, :] = y.astype(o_ref.dtype)


def diff_attention(p_d, bias, lam_params, norm_w, lam_init, B, S):
    T = bias.shape[2]
    G = DIFF_GROUP
    QT = DIFF_QTILES
    assert T >= MAX_DISTANCE and S % (G * T) == 0 and G % QT == 0
    nq = S // (QT * T)
    lanes = 2 * QT * T
    H, hw = DIFF_HEADS, 2 * DIFF_HEAD_DIM
    return pl.pallas_call(
        functools.partial(_diff_kernel, lam_init=lam_init),
        grid=(B, H, nq),
        in_specs=[pl.BlockSpec(lam_params.shape, lambda b, h, i: (0, 0)),
                  pl.BlockSpec((1, hw), lambda b, h, i: (0, 0)),
                  pl.BlockSpec((QT * T, hw), lambda b, h, i: (b * nq + i, h)),
                  pl.BlockSpec((S, hw), lambda b, h, i: (b, H + h)),
                  pl.BlockSpec((S, hw), lambda b, h, i: (b, 2 * H + h)),
                  pl.BlockSpec((1, 4, T, T), lambda b, h, i: (h, 0, 0, 0))],
        out_specs=pl.BlockSpec((QT * T, hw), lambda b, h, i: (b * nq + i, h)),
        out_shape=jax.ShapeDtypeStruct((B * S, H * hw), BF16),
        scratch_shapes=[pltpu.VMEM((hw + BF16_TILE_ROWS, S), BF16),
                        pltpu.VMEM((G, T, lanes), F32), pltpu.VMEM((G, T, lanes), F32),
                        pltpu.VMEM((G * T, lanes), BF16), pltpu.VMEM((G * T, lanes), BF16),
                        pltpu.VMEM((2, 1, lanes), F32), pltpu.VMEM((2, 1, lanes), F32),
                        pltpu.VMEM((1, lanes), F32),
                        pltpu.VMEM((hw + BF16_TILE_ROWS, lanes), F32)],
        compiler_params=_params("parallel", "parallel", "arbitrary"),
        name="diff_attention",
    )(lam_params, norm_w.reshape(1, hw), p_d, p_d, p_d, bias)


def _ret_kernel(lg_ref, nw_ref, cos_ref, sin_ref, q_ref, k_ref, v_ref, g_ref, o_ref,
                st_ref, dec_ref, xi_ref, zeta_ref):
    C = q_ref.shape[0]
    dk, dv = RET_KDIM, RET_VDIM

    @pl.when(pl.program_id(2) == 0)
    def _():
        st_ref[...] = jnp.zeros_like(st_ref)
        rowf = lax.broadcasted_iota(jnp.int32, (C, dk), 0).astype(F32)
        r_i = lax.broadcasted_iota(jnp.int32, (C, C), 0)
        c_i = lax.broadcasted_iota(jnp.int32, (C, C), 1)
        for hh in range(RET_STEP_HEADS):
            lg = lg_ref[hh]
            xi_ref[hh] = jnp.exp((rowf + 1.0) * lg)
            zeta_ref[hh] = jnp.exp((C - 1.0 - rowf) * lg)
            dec_ref[hh] = jnp.exp(jnp.where(r_i >= c_i, (r_i - c_i).astype(F32) * lg[:, 0:1], NEG))

    cosf, sinf = cos_ref[...], sin_ref[...]

    def rot(x):
        return x * cosf + pltpu.roll(x, dk // 2, 1) * sinf

    for hh in range(RET_STEP_HEADS):
        qr = rot(q_ref[:, hh * dk:(hh + 1) * dk])
        kr = rot(k_ref[:, hh * dk:(hh + 1) * dk]) * (dk ** -0.5)
        v = v_ref[:, hh * dv:(hh + 1) * dv]
        state = st_ref[hh]
        scores = _dot_nt(qr.astype(BF16), kr.astype(BF16)) * dec_ref[hh]
        o = _dot(scores.astype(BF16), v) + _dot((qr * xi_ref[hh]).astype(BF16), state.astype(BF16))
        gamma_c = jnp.exp(C * lg_ref[hh][:, 0:1])
        st_ref[hh] = gamma_c * state + _dot((kr * zeta_ref[hh]).T.astype(BF16), v)

        ms = jnp.mean(o * o, axis=-1, keepdims=True)
        g = g_ref[:, hh * dv:(hh + 1) * dv].astype(F32)
        y = o * lax.rsqrt(ms + RMS_EPS) * nw_ref[...] * (g * _sigmoid(g))
        o_ref[:, hh * dv:(hh + 1) * dv] = y.astype(o_ref.dtype)


def retention_mixer(p_qk, p_vg, cos_t, sin_t, norm_w, B, S, C=256):
    C = min(C, S)
    nt = S // C
    H, dk, dv, NH = RET_HEADS, RET_KDIM, RET_VDIM, RET_STEP_HEADS
    HG = H // NH
    log_gamma = jnp.log(1.0 - 2.0 ** (-5.0 - jnp.arange(H, dtype=F32)))
    lg = jnp.broadcast_to(log_gamma[:, None, None], (H, 1, LANES))
    return pl.pallas_call(
        _ret_kernel,
        grid=(B, HG, nt),
        in_specs=[pl.BlockSpec((NH, 1, LANES), lambda b, h, t: (h, 0, 0)),
                  pl.BlockSpec((1, dv), lambda b, h, t: (0, 0)),
                  pl.BlockSpec((C, dk), lambda b, h, t: (t, 0)),
                  pl.BlockSpec((C, dk), lambda b, h, t: (t, 0)),
                  pl.BlockSpec((C, NH * dk), lambda b, h, t: (b * nt + t, h)),
                  pl.BlockSpec((C, NH * dk), lambda b, h, t: (b * nt + t, HG + h)),
                  pl.BlockSpec((C, NH * dv), lambda b, h, t: (b * nt + t, h)),
                  pl.BlockSpec((C, NH * dv), lambda b, h, t: (b * nt + t, HG + h))],
        out_specs=pl.BlockSpec((C, NH * dv), lambda b, h, t: (b * nt + t, h)),
        out_shape=jax.ShapeDtypeStruct((B * S, H * dv), BF16),
        scratch_shapes=[pltpu.VMEM((NH, dk, dv), F32), pltpu.VMEM((NH, C, C), F32),
                        pltpu.VMEM((NH, C, dk), F32), pltpu.VMEM((NH, C, dk), F32)],
        compiler_params=_params("parallel", "parallel", "arbitrary"),
        name="retention_mixer",
    )(lg, norm_w.reshape(1, dv), cos_t, sin_t, p_qk, p_qk, p_vg, p_vg)


def kernel(x, attn_norm_w, w_in, lb_logits, hgrn_norm_w, rel_bias, diff_lambda, diff_norm_w,
           ret_norm_w, w_gate_up, b_gate, w_br_hgrn, w_br_diff, w_br_ret, w_o, ffn_norm_w,
           w_ffn_gate, w_ffn_up, w_ffn_down, final_norm_w):
    B, S, D = x.shape
    depth = w_in.shape[0]
    M = B * S
    xs = x.reshape(M, D)

    cos_t, sin_t = rope_tables(S)
    bias = bias_tiles(rel_bias, min(256, S))

    off_d = 4 * HGRN_WIDTH
    off_rqk = off_d + 3 * DIFF_WIDTH
    off_rvg = off_rqk + 2 * RET_QK_WIDTH
    off_gd = off_rvg + 2 * RET_V_WIDTH

    h, ss = prenorm(xs, attn_norm_w[0])
    w_in_l = layer_weight_bf16(w_in, 0)
    for l in range(depth):
        def riding(call, w, **kw):
            out, w_bf = call(side_cast=(w, l), **kw)
            return out, (layer_weight_bf16(w, l) if w_bf is None else w_bf)

        p_h, w_ret = riding(functools.partial(matmul_cols, h, ss, w_in_l, 0, 4 * HGRN_WIDTH, F32),
                            w_br_ret)
        p_d = matmul_cols(h, ss, w_in_l, off_d, 3 * DIFF_WIDTH, BF16)
        p_rqk, w_hgrn = riding(functools.partial(matmul_cols, h, ss, w_in_l, off_rqk,
                                                 2 * RET_QK_WIDTH, F32), w_br_hgrn)
        p_rvg, y_h = matmul_cols_with_hgrn(h, ss, w_in_l, off_rvg, 2 * RET_V_WIDTH, BF16,
                                           p_h, lb_logits, hgrn_norm_w[l], l, B, S)
        gd, w_diff = riding(functools.partial(matmul_cols, h, ss, w_in_l, off_gd, GATE_RANK, BF16,
                                              tn=GATE_RANK), w_br_diff)

        lam_init = 0.8 - 0.6 * math.exp(-0.3 * l)
        y_d = diff_attention(p_d, bias, diff_lambda[l], diff_norm_w[l], lam_init, B, S)
        y_r = retention_mixer(p_rqk, p_rvg, cos_t, sin_t, ret_norm_w[l], B, S)

        merged, w_o_l = riding(functools.partial(gated_merge, y_h, y_d, y_r, gd, w_hgrn, w_diff, w_ret,
                                                 layer_weight_bf16(w_gate_up, l), b_gate[l]), w_o)
        xs, h2, ss2 = matmul_residual(merged, w_o_l, xs, ffn_norm_w[l], in_place=l > 0)
        act, w_down = swiglu_up(h2, ss2, w_ffn_gate, w_ffn_up, l, (w_ffn_down, l))
        if w_down is None:
            w_down = layer_weight_bf16(w_ffn_down, l)
        if l + 1 < depth:
            xs, h, ss, w_in_l = matmul_residual(act, w_down, xs, attn_norm_w[l + 1], tn=256,
                                                a_buffers=1, side_cast=(w_in, l + 1))
            if w_in_l is None:
                w_in_l = layer_weight_bf16(w_in, l + 1)
        else:
            xs = matmul_residual(act, w_down, xs, tn=256, a_buffers=1)

    out = rms_norm_rows(xs, final_norm_w, x.dtype)
    return out.reshape(B, S, D)
```

```python
import functools
import math

import jax
import jax.numpy as jnp
from jax import lax
from jax.experimental import pallas as pl
from jax.experimental.pallas import tpu as pltpu

F32 = jnp.float32
BF16 = jnp.bfloat16

HGRN_HEADS = 8
HGRN_DIM = 128
DIFF_HEADS = 8
DIFF_HEAD_DIM = 64
RET_HEADS = 8
RET_KDIM = 128
RET_VDIM = 256
N_BUCKETS = 32
MAX_DISTANCE = 128
ROPE_BASE = 10000.0
GATE_RANK = 256
RMS_EPS = 1e-6
HGRN_WIDTH = HGRN_HEADS * HGRN_DIM
DIFF_WIDTH = DIFF_HEADS * 2 * DIFF_HEAD_DIM
RET_QK_WIDTH = RET_HEADS * RET_KDIM
RET_V_WIDTH = RET_HEADS * RET_VDIM

V7X_VMEM_BYTES = 64 * 1024 * 1024
VMEM_LIMIT = V7X_VMEM_BYTES - 8 * 1024 * 1024
LANES = 128
BF16_TILE_ROWS = 16
F32_TILE_ROWS = 8
LOG2E = math.log2(math.e)

NEG = -1e30
HGRN_CHUNK = 128
HGRN_SUB = 16
DIFF_GROUP = 4
DIFF_QTILES = 2
RET_STEP_HEADS = 4
MM_SUB_ROWS = 512
HGRN_STEP_HEADS = 4


def _params(*sem):
    return pltpu.CompilerParams(dimension_semantics=sem, vmem_limit_bytes=VMEM_LIMIT)


def _sigmoid(x):
    return 1.0 / (1.0 + jnp.exp(-x))


def _dot(a, b):
    return jnp.dot(a, b, preferred_element_type=F32)


def _dot_nt(a, b):
    return lax.dot_general(a, b, (((1,), (1,)), ((), ())), preferred_element_type=F32)


def _rms_kernel(x_ref, w_ref, o_ref):
    x = x_ref[...]
    ms = jnp.mean(x * x, axis=-1, keepdims=True)
    o_ref[...] = (x * lax.rsqrt(ms + RMS_EPS) * w_ref[...]).astype(o_ref.dtype)


def rms_norm_rows(x, w, out_dtype, tm=256):
    M, D = x.shape
    tm = min(tm, M)
    return pl.pallas_call(
        _rms_kernel,
        grid=(M // tm,),
        in_specs=[pl.BlockSpec((tm, D), lambda i: (i, 0)),
                  pl.BlockSpec((1, D), lambda i: (0, 0))],
        out_specs=pl.BlockSpec((tm, D), lambda i: (i, 0)),
        out_shape=jax.ShapeDtypeStruct((M, D), out_dtype),
        compiler_params=_params("parallel"),
        name="rmsnorm",
    )(x, w.reshape(1, D))


def _cast_kernel(w_ref, o_ref):
    o_ref[...] = w_ref[...].astype(o_ref.dtype)


def layer_weight_bf16(w, layer, block_bytes=8 * 1024 * 1024):
    _, R, C = w.shape
    rb = R
    while rb * C * 4 > block_bytes and rb % (2 * BF16_TILE_ROWS) == 0:
        rb //= 2
    assert R % rb == 0
    return pl.pallas_call(
        _cast_kernel,
        grid=(R // rb,),
        in_specs=[pl.BlockSpec((None, rb, C), lambda i: (layer, i, 0))],
        out_specs=pl.BlockSpec((rb, C), lambda i: (i, 0)),
        out_shape=jax.ShapeDtypeStruct((R, C), BF16),
        compiler_params=_params("parallel"),
        name="weight_bf16",
    )(w)


def _with_side_cast(kernel_fn, n_in, n_out, n_side=1):
    def wrapped(*refs):
        o0 = n_in + n_side
        for k in range(n_side):
            side_out = refs[o0 + n_out + k]
            side_out[...] = refs[n_in + k][...].astype(side_out.dtype)
        kernel_fn(*refs[:n_in], *refs[o0:o0 + n_out], *refs[o0 + n_out + n_side:])
    return wrapped


def _side_cast_specs(w, layer, grid):
    _, R, C = w.shape
    steps = math.prod(grid)
    if R % steps or (R // steps) % BF16_TILE_ROWS:
        return None
    rb = R // steps

    def step(*ids):
        lin = ids[0]
        for extent, idx in zip(grid[1:], ids[1:]):
            lin = lin * extent + idx
        return lin
    return (pl.BlockSpec((None, rb, C), lambda *ids: (layer, step(*ids), 0)),
            pl.BlockSpec((rb, C), lambda *ids: (step(*ids), 0)),
            jax.ShapeDtypeStruct((R, C), BF16))


def _lane_partial_sumsq(x):
    sq = x * x
    part = sq[:, 0:LANES]
    for k in range(1, x.shape[1] // LANES):
        part = part + sq[:, k * LANES:(k + 1) * LANES]
    return part


def _row_scale(ss, d_model):
    ms = jnp.sum(ss, axis=-1, keepdims=True) / d_model
    return jnp.broadcast_to(lax.rsqrt(ms + RMS_EPS), ss.shape)


def _prenorm_kernel(x_ref, w_ref, xw_ref, ss_ref):
    x = x_ref[...]
    xw_ref[...] = (x * w_ref[...]).astype(xw_ref.dtype)
    ss_ref[...] = _lane_partial_sumsq(x)


def prenorm(x, w, tm=256):
    M, D = x.shape
    tm = min(tm, M)
    return pl.pallas_call(
        _prenorm_kernel,
        grid=(M // tm,),
        in_specs=[pl.BlockSpec((tm, D), lambda i: (i, 0)),
                  pl.BlockSpec((1, D), lambda i: (0, 0))],
        out_specs=[pl.BlockSpec((tm, D), lambda i: (i, 0)),
                   pl.BlockSpec((tm, LANES), lambda i: (i, 0))],
        out_shape=[jax.ShapeDtypeStruct((M, D), BF16), jax.ShapeDtypeStruct((M, LANES), F32)],
        compiler_params=_params("parallel"),
        name="prenorm",
    )(x, w.reshape(1, D))


def _row_blocks(tm):
    sub = min(tm, MM_SUB_ROWS)
    return [slice(r, r + sub) for r in range(0, tm, sub)]


def _mm_kernel(a_ref, ss_ref, b_ref, o_ref, r_ref):
    @pl.when(pl.program_id(1) == 0)
    def _():
        r_ref[...] = _row_scale(ss_ref[...], a_ref.shape[1])
    scale = jnp.tile(r_ref[...], (1, o_ref.shape[1] // LANES))
    o_ref[...] = (_dot(a_ref[...], b_ref[...]) * scale).astype(o_ref.dtype)


def _call_2d(kernel_fn, name, grid, in_specs, out_spec, out_shape, args, side_cast, scratch_shapes=()):
    side = _side_cast_specs(*side_cast, grid) if side_cast is not None else None
    if side is None:
        out = pl.pallas_call(
            kernel_fn, grid=grid, in_specs=in_specs, out_specs=out_spec, out_shape=out_shape,
            scratch_shapes=list(scratch_shapes),
            compiler_params=_params("parallel", "arbitrary"), name=name,
        )(*args)
        return out if side_cast is None else (out, None)
    return pl.pallas_call(
        _with_side_cast(kernel_fn, len(in_specs), 1), grid=grid,
        in_specs=list(in_specs) + [side[0]], out_specs=[out_spec, side[1]],
        out_shape=[out_shape, side[2]], scratch_shapes=list(scratch_shapes),
        compiler_params=_params("parallel", "arbitrary"), name=name + "_cast",
    )(*args, side_cast[0])


def matmul_cols(a, ss, b, col_off, n, out_dtype, tm=1024, tn=1024, side_cast=None):
    M, K = a.shape
    tm, tn = min(tm, M), min(tn, n)
    assert col_off % tn == 0 and n % tn == 0 and M % tm == 0
    off = col_off // tn
    return _call_2d(
        _mm_kernel, "matmul_cols", (M // tm, n // tn),
        [pl.BlockSpec((tm, K), lambda i, j: (i, 0)),
         pl.BlockSpec((tm, LANES), lambda i, j: (i, 0)),
         pl.BlockSpec((K, tn), lambda i, j: (0, j + off))],
        pl.BlockSpec((tm, tn), lambda i, j: (i, j)),
        jax.ShapeDtypeStruct((M, n), out_dtype),
        (a, ss, b), side_cast, scratch_shapes=[pltpu.VMEM((tm, LANES), F32)])


def _mm_res_kernel(a_ref, b_ref, x_ref, o_ref):
    o_ref[...] = x_ref[...] + _dot(a_ref[...], b_ref[...])


def _mm_res_norm_kernel(a_ref, b_ref, x_ref, w_ref, o_ref, xw_ref, ss_ref):
    y = x_ref[...] + _dot(a_ref[...], b_ref[...])
    o_ref[...] = y
    xw_ref[...] = (y * w_ref[...]).astype(xw_ref.dtype)
    part = _lane_partial_sumsq(y)

    @pl.when(pl.program_id(1) == 0)
    def _():
        ss_ref[...] = part

    @pl.when(pl.program_id(1) > 0)
    def _():
        ss_ref[...] = ss_ref[...] + part


def matmul_residual(a, b, x, next_norm_w=None, in_place=True, tm=1024, tn=512, a_buffers=2,
                    side_cast=None):
    M, K = a.shape
    N = b.shape[1]
    tm, tn = min(tm, M), min(tn, N)
    assert M % tm == 0 and N % tn == 0
    tile = pl.BlockSpec((tm, tn), lambda i, j: (i, j))
    in_specs = [pl.BlockSpec((tm, K), lambda i, j: (i, 0), pipeline_mode=pl.Buffered(a_buffers)),
                pl.BlockSpec((K, tn), lambda i, j: (0, j)),
                tile]
    aliases = {2: 0} if in_place else {}
    grid = (M // tm, N // tn)
    side = _side_cast_specs(*side_cast, grid) if side_cast is not None else None
    if side is not None:
        assert next_norm_w is not None
        out = pl.pallas_call(
            _with_side_cast(_mm_res_norm_kernel, 4, 3),
            grid=grid,
            in_specs=in_specs + [pl.BlockSpec((1, tn), lambda i, j: (0, j)), side[0]],
            out_specs=[tile, tile, pl.BlockSpec((tm, LANES), lambda i, j: (i, 0)), side[1]],
            out_shape=[jax.ShapeDtypeStruct((M, N), F32), jax.ShapeDtypeStruct((M, N), BF16),
                       jax.ShapeDtypeStruct((M, LANES), F32), side[2]],
            input_output_aliases=aliases,
            compiler_params=_params("parallel", "arbitrary"),
            name="matmul_residual_norm_cast",
        )(a, b, x, next_norm_w.reshape(1, N), side_cast[0])
        return out
    if side_cast is not None:
        return (*matmul_residual(a, b, x, next_norm_w, in_place, tm, tn, a_buffers), None)
    if next_norm_w is None:
        return pl.pallas_call(
            _mm_res_kernel,
            grid=(M // tm, N // tn),
            in_specs=in_specs,
            out_specs=tile,
            out_shape=jax.ShapeDtypeStruct((M, N), F32),
            input_output_aliases=aliases,
            compiler_params=_params("parallel", "arbitrary"),
            name="matmul_residual",
        )(a, b, x)
    return pl.pallas_call(
        _mm_res_norm_kernel,
        grid=(M // tm, N // tn),
        in_specs=in_specs + [pl.BlockSpec((1, tn), lambda i, j: (0, j))],
        out_specs=[tile, tile, pl.BlockSpec((tm, LANES), lambda i, j: (i, 0))],
        out_shape=[jax.ShapeDtypeStruct((M, N), F32), jax.ShapeDtypeStruct((M, N), BF16),
                   jax.ShapeDtypeStruct((M, LANES), F32)],
        input_output_aliases=aliases,
        compiler_params=_params("parallel", "arbitrary"),
        name="matmul_residual_norm",
    )(a, b, x, next_norm_w.reshape(1, N))


def _swiglu_kernel(a_ref, ss_ref, wg_ref, wu_ref, o_ref, r_ref):
    @pl.when(pl.program_id(1) == 0)
    def _():
        r_ref[...] = _row_scale(ss_ref[...], a_ref.shape[1])
    wg = wg_ref[...].astype(BF16)
    wu = wu_ref[...].astype(BF16)
    for rows in _row_blocks(a_ref.shape[0]):
        scale = jnp.tile(r_ref[rows, :], (1, o_ref.shape[1] // LANES))
        a = a_ref[rows, :]
        g = _dot(a, wg) * scale
        u = _dot(a, wu) * scale
        o_ref[rows, :] = (g * _sigmoid(g) * u).astype(o_ref.dtype)


def swiglu_up(a, ss, wg, wu, layer, side_cast, tm=2048, tn=256):
    M, K = a.shape
    N = wg.shape[-1]
    tm, tn = min(tm, M), min(tn, N)
    assert M % tm == 0 and N % tn == 0
    grid = (M // tm, N // tn)
    if wg.ndim == 3:
        w_spec = pl.BlockSpec((None, K, tn), lambda i, j: (layer, 0, j))
        a_buffers = 1
    else:
        w_spec = pl.BlockSpec((K, tn), lambda i, j: (0, j))
        a_buffers = 2
    in_specs = [pl.BlockSpec((tm, K), lambda i, j: (i, 0), pipeline_mode=pl.Buffered(a_buffers)),
                pl.BlockSpec((tm, LANES), lambda i, j: (i, 0)),
                w_spec, w_spec]
    tile = pl.BlockSpec((tm, tn), lambda i, j: (i, j))
    side = _side_cast_specs(*side_cast, grid)
    if side is None:
        return pl.pallas_call(
            _swiglu_kernel,
            grid=grid,
            in_specs=in_specs,
            out_specs=tile,
            out_shape=jax.ShapeDtypeStruct((M, N), BF16),
            scratch_shapes=[pltpu.VMEM((tm, LANES), F32)],
            compiler_params=_params("parallel", "arbitrary"),
            name="swiglu_up",
        )(a, ss, wg, wu), None
    return pl.pallas_call(
        _with_side_cast(_swiglu_kernel, 4, 1),
        grid=grid,
        in_specs=in_specs + [side[0]],
        out_specs=[tile, side[1]],
        out_shape=[jax.ShapeDtypeStruct((M, N), BF16), side[2]],
        scratch_shapes=[pltpu.VMEM((tm, LANES), F32)],
        compiler_params=_params("parallel", "arbitrary"),
        name="swiglu_up_cast",
    )(a, ss, wg, wu, side_cast[0])


def _merge_kernel(yh_ref, yd_ref, yr_ref, gd_ref, wh_ref, wd_ref, wr_ref,
                  wgh_ref, wgd_ref, wgr_ref, bh_ref, bd_ref, br_ref, o_ref):
    gd = gd_ref[...]

    def branch(y_ref, w_ref, wg_ref, b_ref):
        gate = _sigmoid(_dot(gd, wg_ref[...]) + b_ref[...])
        return gate * _dot(y_ref[...], w_ref[...])

    acc = branch(yh_ref, wh_ref, wgh_ref, bh_ref)
    acc = acc + branch(yd_ref, wd_ref, wgd_ref, bd_ref)
    acc = acc + branch(yr_ref, wr_ref, wgr_ref, br_ref)
    o_ref[...] = acc.astype(o_ref.dtype)


def gated_merge(yh, yd, yr, gd, wh, wd, wr, wg, bg, tm=1024, tn=512, side_cast=None):
    M = yh.shape[0]
    D = wh.shape[1]
    tm, tn = min(tm, M), min(tn, D)
    assert M % tm == 0 and D % tn == 0
    nb = D // tn
    row = lambda width: pl.BlockSpec((tm, width), lambda i, j: (i, 0))
    wcol = lambda k: pl.BlockSpec((k, tn), lambda i, j: (0, j))
    gcol = lambda br: pl.BlockSpec((GATE_RANK, tn), lambda i, j: (0, j + br * nb))
    bcol = lambda br: pl.BlockSpec((1, tn), lambda i, j: (0, j + br * nb))
    bg2 = bg.reshape(1, -1)
    return _call_2d(
        _merge_kernel, "gated_merge", (M // tm, nb),
        [row(yh.shape[1]), row(yd.shape[1]), row(yr.shape[1]), row(gd.shape[1]),
         wcol(wh.shape[0]), wcol(wd.shape[0]), wcol(wr.shape[0]),
         gcol(0), gcol(1), gcol(2), bcol(0), bcol(1), bcol(2)],
        pl.BlockSpec((tm, tn), lambda i, j: (i, j)),
        jax.ShapeDtypeStruct((M, D), BF16),
        (yh, yd, yr, gd, wh, wd, wr, wg, wg, wg, bg2, bg2, bg2), side_cast)


def _rope_table_kernel(inv_ref, cos_ref, sin_ref):
    T = cos_ref.shape[0]
    pos = (lax.broadcasted_iota(jnp.int32, (T, LANES), 0) + pl.program_id(0) * T).astype(F32)
    ang = pos * inv_ref[...]
    lane = lax.broadcasted_iota(jnp.int32, (T, LANES), 1)
    cos_ref[...] = jnp.cos(ang)
    sin_ref[...] = jnp.where(lane < LANES // 2, -jnp.sin(ang), jnp.sin(ang))


def rope_tables(S, T=256):
    T = min(T, S)
    half = RET_KDIM // 2
    inv = 1.0 / (ROPE_BASE ** (jnp.arange(half, dtype=F32) / half))
    inv2 = jnp.concatenate([inv, inv]).reshape(1, LANES)
    return pl.pallas_call(
        _rope_table_kernel,
        grid=(S // T,),
        in_specs=[pl.BlockSpec((1, LANES), lambda i: (0, 0))],
        out_specs=[pl.BlockSpec((T, LANES), lambda i: (i, 0))] * 2,
        out_shape=[jax.ShapeDtypeStruct((S, LANES), F32)] * 2,
        compiler_params=_params("parallel"),
        name="rope_tables",
    )(inv2)


def _bias_tile_kernel(tab_ref, o_ref):
    h = pl.program_id(0)
    T = o_ref.shape[2]
    j = lax.broadcasted_iota(jnp.int32, (T, T), 0)
    i = lax.broadcasted_iota(jnp.int32, (T, T), 1)
    max_exact = N_BUCKETS // 2
    o_ref[0, 3] = jnp.full((T, T), NEG, F32)
    for d in range(3):
        n = jnp.maximum(i - j + d * T, 0)
        nf = jnp.maximum(n, 1).astype(F32)
        large = max_exact + (jnp.log(nf / max_exact) / math.log(MAX_DISTANCE / max_exact)
                             * (N_BUCKETS - max_exact)).astype(jnp.int32)
        large = jnp.minimum(large, N_BUCKETS - 1)
        bucket = jnp.where(n < max_exact, n, large)
        val = jnp.zeros((T, T), F32)
        for b in range(N_BUCKETS):
            val = jnp.where(bucket == b, tab_ref[h, b], val)
        val = (val - tab_ref[h, N_BUCKETS - 1]) * LOG2E
        if d == 0:
            val = jnp.where(j > i, NEG, val)
        o_ref[0, d] = val


def bias_tiles(rel_bias, T):
    H = rel_bias.shape[1]
    return pl.pallas_call(
        _bias_tile_kernel,
        grid=(H,),
        in_specs=[pl.BlockSpec(memory_space=pltpu.SMEM)],
        out_specs=pl.BlockSpec((1, 4, T, T), lambda h: (h, 0, 0, 0)),
        out_shape=jax.ShapeDtypeStruct((H, 4, T, T), F32),
        compiler_params=_params("parallel"),
        name="bias_tiles",
    )(rel_bias.T)


def _hgrn_kernel(lbl_ref, nw_ref, q_ref, f_ref, i_ref, g_ref, o_ref,
                 st_ref, p_ref, r_ref, ol_ref, *, layer):
    @pl.when(pl.program_id(2) == 0)
    def _():
        st_ref[...] = jnp.zeros_like(st_ref)

    for hh in range(HGRN_STEP_HEADS):
        lanes = slice(hh * HGRN_DIM, (hh + 1) * HGRN_DIM)
        _hgrn_head(lbl_ref[:, lanes], nw_ref[...], q_ref[:, lanes], f_ref[:, lanes], i_ref[:, lanes],
                   g_ref[:, lanes], o_ref.at[:, lanes], st_ref.at[hh], p_ref.at[hh], r_ref.at[hh],
                   ol_ref.at[hh], layer)


def _hgrn_head(lg, nw, q, f, iv_all, g_all, o_ref, st_ref, p_ref, r_ref, ol_ref, layer):
    bcum, qf, kf = _hgrn_head_gates(lg, q, f, layer)
    _hgrn_head_mix(bcum, qf, kf, nw, iv_all, g_all, o_ref, st_ref, p_ref, r_ref, ol_ref)


def _hgrn_head_gates(lg, q, f, layer):
    T = q.shape[0]
    C = HGRN_CHUNK

    e = jnp.exp(lg - jnp.max(lg, axis=0, keepdims=True))
    sm = e / jnp.sum(e, axis=0, keepdims=True)
    csum = sm[0:1]
    for r in range(1, layer + 1):
        csum = csum + sm[r:r + 1]
    lb = csum - sm[0:1]

    forget = lb + (1.0 - lb) * _sigmoid(f)
    kf = 1.0 - forget
    logf = jnp.log(forget)
    qf = q * _sigmoid(q) * (HGRN_DIM ** -0.5)

    r_i = lax.broadcasted_iota(jnp.int32, (T, T), 0)
    c_i = lax.broadcasted_iota(jnp.int32, (T, T), 1)
    shift = C.bit_length() - 1
    same_chunk = (r_i >> shift) == (c_i >> shift)
    tri = jnp.where(c_i <= r_i, jnp.where(same_chunk, 1.0, 0.0), 0.0).astype(BF16)
    hi = logf.astype(BF16)
    rem = logf - hi.astype(F32)
    mid = rem.astype(BF16)
    lo = (rem - mid.astype(F32)).astype(BF16)
    bcum = (_dot(tri, hi) + _dot(tri, mid) + _dot(tri, lo)) * LOG2E
    return bcum, qf, kf


def _hgrn_head_mix(bcum, qf, kf, nw, iv_all, g_all, o_ref, st_ref, p_ref, r_ref, ol_ref):
    T = qf.shape[0]
    C, SUB = HGRN_CHUNK, HGRN_SUB
    row_c = lax.broadcasted_iota(jnp.int32, (C, HGRN_DIM), 0)
    row_s = lax.broadcasted_iota(jnp.int32, (SUB, HGRN_DIM), 0)
    d_i = lax.broadcasted_iota(jnp.int32, (2 * HGRN_DIM, 2 * HGRN_DIM), 0)
    d_j = lax.broadcasted_iota(jnp.int32, (2 * HGRN_DIM, 2 * HGRN_DIM), 1)
    ones2 = jnp.where((d_i < HGRN_DIM) == (d_j < HGRN_DIM), 1.0, 0.0).astype(BF16)

    def slot(r):
        half, rb = divmod(r, T // 2)
        return rb * SUB, slice(half * HGRN_DIM, (half + 1) * HGRN_DIM)

    n_sub = T // SUB
    causal_add = [jnp.where(row_s[(s // F32_TILE_ROWS) * F32_TILE_ROWS:] >= s, 0.0, NEG)
                  for s in range(SUB)]
    for sb in range(n_sub):
        sub = slice(sb * SUB, (sb + 1) * SUB)
        bi, qi = bcum[sub], qf[sub]
        for s in range(SUB):
            r = sb * SUB + s
            top = (s // F32_TILE_ROWS) * F32_TILE_ROWS
            dec = jnp.exp2(bi[top:] - bcum[r:r + 1] + causal_add[s])
            p = qi[top:] * kf[r:r + 1] * dec
            if top:
                p = jnp.concatenate([jnp.zeros((top, HGRN_DIM), F32), p], axis=0)
            r0, lanes = slot(r)
            p_ref[r0:r0 + SUB, lanes] = p.astype(BF16)
    r_ref[...] = _dot(p_ref[...], ones2)

    n_blk = C // SUB
    for c in range(T // C):
        c0 = c * C
        b, kc = bcum[c0:c0 + C], kf[c0:c0 + C]
        qts, kts = [], []
        for j in range(1, n_blk):
            lo_r = j * SUB
            ref = b[lo_r - 1:lo_r]
            qts.append((qf[c0 + lo_r:c0 + lo_r + SUB] * jnp.exp2(b[lo_r:lo_r + SUB] - ref)).astype(BF16))
            kts.append((kc * jnp.exp2(jnp.where(row_c < lo_r, ref - b, NEG))).astype(BF16))
        att_all = _dot_nt(jnp.concatenate(qts, axis=0), jnp.concatenate(kts, axis=0))
        att = jnp.concatenate([att_all[j * SUB:(j + 1) * SUB, j * C:(j + 1) * C]
                               for j in range(n_blk - 1)], axis=0)
        off = _dot(att.astype(BF16), iv_all[c0:c0 + C].astype(BF16))
        for j in range(n_blk):
            lo_r = c0 + j * SUB
            r0, lanes = slot(lo_r)
            oi = r_ref[r0:r0 + SUB, lanes] * iv_all[lo_r:lo_r + 1]
            for s in range(1, SUB):
                r = lo_r + s
                top = (s // F32_TILE_ROWS) * F32_TILE_ROWS
                r0, lanes = slot(r)
                term = r_ref[r0 + top:r0 + SUB, lanes] * iv_all[r:r + 1]
                oi = oi + term if top == 0 else jnp.concatenate([oi[:top], oi[top:] + term], axis=0)
            if j > 0:
                oi = oi + off[(j - 1) * SUB:j * SUB]
            ol_ref[lo_r:lo_r + SUB, :] = oi

    state_t = st_ref[...]
    for c in range(T // C):
        rows = slice(c * C, (c + 1) * C)
        b, qc, kc, iv = bcum[rows], qf[rows], kf[rows], iv_all[rows]
        o = ol_ref[rows, :] + _dot_nt((qc * jnp.exp2(b)).astype(BF16), state_t.astype(BF16))
        b_last = b[C - 1:C]
        khat = (kc * jnp.exp2(b_last - b)).astype(BF16)
        state_t = state_t * jnp.exp2(b_last) + _dot(iv.T.astype(BF16), khat)
        ms = jnp.mean(o * o, axis=-1, keepdims=True)
        gc = g_all[rows]
        y = o * lax.rsqrt(ms + RMS_EPS) * nw * (gc * _sigmoid(gc))
        o_ref[rows, :] = y.astype(o_ref.dtype)
    st_ref[...] = state_t


def hgrn_mixer(p_h, lb_logits, norm_w, layer, B, S, T=256):
    T = min(T, S)
    nt = S // T
    H, dk, NH = HGRN_HEADS, HGRN_DIM, HGRN_STEP_HEADS
    HG = H // NH
    col = lambda grp: pl.BlockSpec((T, NH * dk), lambda b, h, t: (b * nt + t, grp * HG + h))
    return pl.pallas_call(
        functools.partial(_hgrn_kernel, layer=layer),
        grid=(B, HG, nt),
        in_specs=[pl.BlockSpec((lb_logits.shape[0], NH * dk), lambda b, h, t: (0, h)),
                  pl.BlockSpec((1, dk), lambda b, h, t: (0, 0)),
                  col(0), col(1), col(2), col(3)],
        out_specs=pl.BlockSpec((T, NH * dk), lambda b, h, t: (b * nt + t, h)),
        out_shape=jax.ShapeDtypeStruct((B * S, H * dk), BF16),
        scratch_shapes=[pltpu.VMEM((NH, dk, dk), F32),
                        pltpu.VMEM((NH, T // 2 * HGRN_SUB, 2 * dk), BF16),
                        pltpu.VMEM((NH, T // 2 * HGRN_SUB, 2 * dk), F32),
                        pltpu.VMEM((NH, T, dk), F32)],
        compiler_params=_params("parallel", "parallel", "arbitrary"),
        name="hgrn_mixer",
    )(lb_logits, norm_w.reshape(1, dk), p_h, p_h, p_h, p_h)


def _mm_hgrn_kernel(a_ref, ss_ref, b_ref, lbl_ref, nw_ref, q_ref, f_ref, i_ref, g_ref,
                    o_ref, y_ref, r_ref, st_ref, p_ref, r2_ref, ol_ref,
                    *, layer, n_col_blocks, n_time_blocks):
    step = pl.program_id(0)

    @pl.when(step % n_col_blocks == 0)
    def _():
        r_ref[...] = _row_scale(ss_ref[...], a_ref.shape[1])

    @pl.when(step % n_time_blocks == 0)
    def _():
        st_ref[...] = jnp.zeros_like(st_ref)

    b = b_ref[...]
    sub = a_ref.shape[0] // HGRN_STEP_HEADS
    for hh in range(HGRN_STEP_HEADS):
        lanes = slice(hh * HGRN_DIM, (hh + 1) * HGRN_DIM)
        bcum, qf, kf = _hgrn_head_gates(lbl_ref[:, lanes], q_ref[:, lanes], f_ref[:, lanes], layer)
        rows = slice(hh * sub, (hh + 1) * sub)
        scale = jnp.tile(r_ref[rows, :], (1, o_ref.shape[1] // LANES))
        o_ref[rows, :] = (_dot(a_ref[rows, :], b) * scale).astype(o_ref.dtype)
        _hgrn_head_mix(bcum, qf, kf, nw_ref[...], i_ref[:, lanes], g_ref[:, lanes],
                       y_ref.at[:, lanes], st_ref.at[hh], p_ref.at[hh], r2_ref.at[hh], ol_ref.at[hh])


def matmul_cols_with_hgrn(a, ss, b, col_off, n, out_dtype, p_h, lb_logits, norm_w, layer, B, S,
                          tm=1024, tn=512, T=256):
    M, K = a.shape
    tm, tn, T = min(tm, M), min(tn, n), min(T, S)
    nt = S // T
    H, dk, NH = HGRN_HEADS, HGRN_DIM, HGRN_STEP_HEADS
    HG = H // NH
    ncb = n // tn
    if (M // tm) * ncb != B * HG * nt:
        return (matmul_cols(a, ss, b, col_off, n, out_dtype),
                hgrn_mixer(p_h, lb_logits, norm_w, layer, B, S))
    assert col_off % tn == 0 and n % tn == 0 and M % tm == 0
    off = col_off // tn
    row_blk = lambda s: (s // (HG * nt)) * nt + s % nt
    head_grp = lambda s: (s // nt) % HG
    col = lambda grp: pl.BlockSpec((T, NH * dk), lambda s: (row_blk(s), grp * HG + head_grp(s)))
    return pl.pallas_call(
        functools.partial(_mm_hgrn_kernel, layer=layer, n_col_blocks=ncb, n_time_blocks=nt),
        grid=((M // tm) * ncb,),
        in_specs=[pl.BlockSpec((tm, K), lambda s: (s // ncb, 0)),
                  pl.BlockSpec((tm, LANES), lambda s: (s // ncb, 0)),
                  pl.BlockSpec((K, tn), lambda s: (0, s % ncb + off)),
                  pl.BlockSpec((lb_logits.shape[0], NH * dk), lambda s: (0, head_grp(s))),
                  pl.BlockSpec((1, dk), lambda s: (0, 0)),
                  col(0), col(1), col(2), col(3)],
        out_specs=[pl.BlockSpec((tm, tn), lambda s: (s // ncb, s % ncb)),
                   pl.BlockSpec((T, NH * dk), lambda s: (row_blk(s), head_grp(s)))],
        out_shape=[jax.ShapeDtypeStruct((M, n), out_dtype),
                   jax.ShapeDtypeStruct((B * S, H * dk), BF16)],
        scratch_shapes=[pltpu.VMEM((tm, LANES), F32),
                        pltpu.VMEM((NH, dk, dk), F32),
                        pltpu.VMEM((NH, T // 2 * HGRN_SUB, 2 * dk), BF16),
                        pltpu.VMEM((NH, T // 2 * HGRN_SUB, 2 * dk), F32),
                        pltpu.VMEM((NH, T, dk), F32)],
        compiler_params=_params("arbitrary"),
        name="matmul_cols_hgrn",
    )(a, ss, b, lb_logits, norm_w.reshape(1, dk), p_h, p_h, p_h, p_h)


def _diff_kernel(lam_ref, nw_ref, q_ref, k_ref, v_ref, bias_ref, o_ref,
                 vt_ref, s0_ref, s1_ref, p0_ref, p1_ref, gm_ref, alpha_ref, m_ref, acc_ref, *, lam_init):
    QT = DIFF_QTILES
    T = q_ref.shape[0] // QT
    S = k_ref.shape[0]
    dh = DIFF_HEAD_DIM
    G = DIFF_GROUP
    n_groups = S // (G * T)
    qi = pl.program_id(2) * QT
    ng = (qi + QT - 1) // G + 1

    @pl.when(qi == 0)
    def _():
        def body(c, carry):
            start = pl.multiple_of(c * T, T)
            vt_ref[0:2 * dh, pl.ds(start, T)] = v_ref[pl.ds(start, T), :].astype(F32).T.astype(BF16)
            return carry
        lax.fori_loop(0, S // T, body, 0)
        pad_row = lax.broadcasted_iota(jnp.int32, (BF16_TILE_ROWS, S), 0)
        vt_ref[2 * dh:, :] = jnp.where(pad_row == 0, 1.0, 0.0).astype(BF16)

    row = lax.broadcasted_iota(jnp.int32, (2 * dh, T), 0)
    cols = []
    for t in range(QT):
        qt = q_ref[t * T:(t + 1) * T, :].astype(F32).T * (dh ** -0.5 * LOG2E)
        cols += [jnp.where(row < dh, qt, 0.0), jnp.where(row >= dh, qt, 0.0)]
    q2 = jnp.concatenate(cols, axis=1).astype(BF16)

    slots = (s0_ref, s1_ref)

    p_slots = (p0_ref, p1_ref)

    def values_group(g):
        pv = _dot(vt_ref[:, g * G * T:(g + 1) * G * T], p_slots[g % 2][...])
        acc_ref[...] = alpha_ref[g % 2] * acc_ref[...] + pv

    def block(g_scores, near, g_values, g_softmax):
        if g_values is not None:
            values_group(g_values)
        if g_softmax is not None:
            sm_slot, p_slot = slots[g_softmax % 2], p_slots[g_softmax % 2]
            m_old = m_ref[...]
            m_new = jnp.maximum(m_old, gm_ref[g_softmax % 2])
            alpha_ref[g_softmax % 2] = jnp.exp2(m_old - m_new)
        gm = None
        for u in range(G):
            if g_scores is not None:
                kt = g_scores * G + u
                s = _dot(k_ref[kt * T:(kt + 1) * T, :], q2)
                if near:
                    tiles = []
                    for t in range(QT):
                        d = qi + t - kt
                        tiles += [bias_ref[0, jnp.where(d < 0, 3, jnp.minimum(d, 2))]] * 2
                    s = s + jnp.concatenate(tiles, axis=1)
                slots[g_scores % 2][u] = s
                cm = jnp.max(s, axis=0, keepdims=True)
                gm = cm if gm is None else jnp.maximum(gm, cm)
            if g_softmax is not None:
                p_slot[u * T:(u + 1) * T, :] = jnp.exp2(sm_slot[u] - m_new).astype(BF16)
        if g_scores is not None:
            gm_ref[g_scores % 2] = gm
        if g_softmax is not None:
            m_ref[...] = m_new

    m_ref[...] = jnp.full(m_ref.shape, NEG, F32)
    acc_ref[...] = jnp.zeros(acc_ref.shape, F32)
    n_far_groups = jnp.maximum(qi - 1, 0) // G

    def stage(cond, g):
        for near in (False, True):
            is_near = g >= n_far_groups
            @pl.when(jnp.logical_and(cond, is_near if near else jnp.logical_not(is_near)))
            def _():
                block(g, near, None, g - 1 if g >= 1 else None)
                if g >= 1:
                    values_group(g - 1)

    stage(True, 0)
    for g in range(n_groups):
        if g + 1 < n_groups:
            stage(g < ng - 1, g + 1)

        @pl.when(g == ng - 1)
        def _():
            block(None, False, None, g)
            values_group(g)

    lp = lam_ref[...]
    lam = (jnp.exp(jnp.sum(lp[0:1] * lp[1:2], axis=-1, keepdims=True))
           - jnp.exp(jnp.sum(lp[2:3] * lp[3:4], axis=-1, keepdims=True)) + lam_init)
    acc = acc_ref[...]
    w = acc[:2 * dh] / acc[2 * dh:2 * dh + 1]
    for t in range(QT):
        c0 = 2 * t * T
        out = (w[:, c0:c0 + T] - lam * w[:, c0 + T:c0 + 2 * T]).T
        ms = jnp.mean(out * out, axis=-1, keepdims=True)
        y = out * lax.rsqrt(ms + RMS_EPS) * nw_ref[...] * (1.0 - lam_init)
        o_ref[t * T:(t + 1) * T, :] = y.astype(o_ref.dtype)


def diff_attention(p_d, bias, lam_params, norm_w, lam_init, B, S, side_casts=()):
    T = bias.shape[2]
    G = DIFF_GROUP
    QT = DIFF_QTILES
    assert T >= MAX_DISTANCE and S % (G * T) == 0 and G % QT == 0
    nq = S // (QT * T)
    lanes = 2 * QT * T
    H, hw = DIFF_HEADS, 2 * DIFF_HEAD_DIM
    grid = (B, H, nq)
    sides = [_side_cast_specs(w, layer, grid) for w, layer in side_casts]
    if any(side is None for side in sides):
        return (diff_attention(p_d, bias, lam_params, norm_w, lam_init, B, S),
                *[None] * len(side_casts))
    kernel_fn = functools.partial(_diff_kernel, lam_init=lam_init)
    if sides:
        kernel_fn = _with_side_cast(kernel_fn, 6, 1, len(sides))
    out_spec = pl.BlockSpec((QT * T, hw), lambda b, h, i: (b * nq + i, h))
    out_shape = jax.ShapeDtypeStruct((B * S, H * hw), BF16)
    return pl.pallas_call(
        kernel_fn,
        grid=grid,
        in_specs=[pl.BlockSpec(lam_params.shape, lambda b, h, i: (0, 0)),
                  pl.BlockSpec((1, hw), lambda b, h, i: (0, 0)),
                  pl.BlockSpec((QT * T, hw), lambda b, h, i: (b * nq + i, h)),
                  pl.BlockSpec((S, hw), lambda b, h, i: (b, H + h)),
                  pl.BlockSpec((S, hw), lambda b, h, i: (b, 2 * H + h)),
                  pl.BlockSpec((1, 4, T, T), lambda b, h, i: (h, 0, 0, 0))] + [s[0] for s in sides],
        out_specs=[out_spec] + [s[1] for s in sides] if sides else out_spec,
        out_shape=[out_shape] + [s[2] for s in sides] if sides else out_shape,
        scratch_shapes=[pltpu.VMEM((hw + BF16_TILE_ROWS, S), BF16),
                        pltpu.VMEM((G, T, lanes), F32), pltpu.VMEM((G, T, lanes), F32),
                        pltpu.VMEM((G * T, lanes), BF16), pltpu.VMEM((G * T, lanes), BF16),
                        pltpu.VMEM((2, 1, lanes), F32), pltpu.VMEM((2, 1, lanes), F32),
                        pltpu.VMEM((1, lanes), F32),
                        pltpu.VMEM((hw + BF16_TILE_ROWS, lanes), F32)],
        compiler_params=_params("parallel", "parallel", "arbitrary"),
        name="diff_attention",
    )(lam_params, norm_w.reshape(1, hw), p_d, p_d, p_d, bias, *[w for w, _ in side_casts])


def _ret_kernel(lg_ref, nw_ref, cos_ref, sin_ref, q_ref, k_ref, v_ref, g_ref, o_ref,
                st_ref, dec_ref, xi_ref, zeta_ref):
    C = q_ref.shape[0]
    dk, dv = RET_KDIM, RET_VDIM

    @pl.when(pl.program_id(2) == 0)
    def _():
        st_ref[...] = jnp.zeros_like(st_ref)
        rowf = lax.broadcasted_iota(jnp.int32, (C, dk), 0).astype(F32)
        r_i = lax.broadcasted_iota(jnp.int32, (C, C), 0)
        c_i = lax.broadcasted_iota(jnp.int32, (C, C), 1)
        for hh in range(RET_STEP_HEADS):
            lg = lg_ref[hh]
            xi_ref[hh] = jnp.exp((rowf + 1.0) * lg)
            zeta_ref[hh] = jnp.exp((C - 1.0 - rowf) * lg)
            dec_ref[hh] = jnp.exp(jnp.where(r_i >= c_i, (r_i - c_i).astype(F32) * lg[:, 0:1], NEG))

    cosf, sinf = cos_ref[...], sin_ref[...]

    def rot(x):
        return x * cosf + pltpu.roll(x, dk // 2, 1) * sinf

    for hh in range(RET_STEP_HEADS):
        qr = rot(q_ref[:, hh * dk:(hh + 1) * dk])
        kr = rot(k_ref[:, hh * dk:(hh + 1) * dk]) * (dk ** -0.5)
        v = v_ref[:, hh * dv:(hh + 1) * dv]
        state = st_ref[hh]
        scores = _dot_nt(qr.astype(BF16), kr.astype(BF16)) * dec_ref[hh]
        o = _dot(scores.astype(BF16), v) + _dot((qr * xi_ref[hh]).astype(BF16), state.astype(BF16))
        gamma_c = jnp.exp(C * lg_ref[hh][:, 0:1])
        st_ref[hh] = gamma_c * state + _dot((kr * zeta_ref[hh]).T.astype(BF16), v)

        ms = jnp.mean(o * o, axis=-1, keepdims=True)
        g = g_ref[:, hh * dv:(hh + 1) * dv].astype(F32)
        y = o * lax.rsqrt(ms + RMS_EPS) * nw_ref[...] * (g * _sigmoid(g))
        o_ref[:, hh * dv:(hh + 1) * dv] = y.astype(o_ref.dtype)


def retention_mixer(p_qk, p_vg, cos_t, sin_t, norm_w, B, S, C=256):
    C = min(C, S)
    nt = S // C
    H, dk, dv, NH = RET_HEADS, RET_KDIM, RET_VDIM, RET_STEP_HEADS
    HG = H // NH
    log_gamma = jnp.log(1.0 - 2.0 ** (-5.0 - jnp.arange(H, dtype=F32)))
    lg = jnp.broadcast_to(log_gamma[:, None, None], (H, 1, LANES))
    return pl.pallas_call(
        _ret_kernel,
        grid=(B, HG, nt),
        in_specs=[pl.BlockSpec((NH, 1, LANES), lambda b, h, t: (h, 0, 0)),
                  pl.BlockSpec((1, dv), lambda b, h, t: (0, 0)),
                  pl.BlockSpec((C, dk), lambda b, h, t: (t, 0)),
                  pl.BlockSpec((C, dk), lambda b, h, t: (t, 0)),
                  pl.BlockSpec((C, NH * dk), lambda b, h, t: (b * nt + t, h)),
                  pl.BlockSpec((C, NH * dk), lambda b, h, t: (b * nt + t, HG + h)),
                  pl.BlockSpec((C, NH * dv), lambda b, h, t: (b * nt + t, h)),
                  pl.BlockSpec((C, NH * dv), lambda b, h, t: (b * nt + t, HG + h))],
        out_specs=pl.BlockSpec((C, NH * dv), lambda b, h, t: (b * nt + t, h)),
        out_shape=jax.ShapeDtypeStruct((B * S, H * dv), BF16),
        scratch_shapes=[pltpu.VMEM((NH, dk, dv), F32), pltpu.VMEM((NH, C, C), F32),
                        pltpu.VMEM((NH, C, dk), F32), pltpu.VMEM((NH, C, dk), F32)],
        compiler_params=_params("parallel", "parallel", "arbitrary"),
        name="retention_mixer",
    )(lg, norm_w.reshape(1, dv), cos_t, sin_t, p_qk, p_qk, p_vg, p_vg)


def kernel(x, attn_norm_w, w_in, lb_logits, hgrn_norm_w, rel_bias, diff_lambda, diff_norm_w,
           ret_norm_w, w_gate_up, b_gate, w_br_hgrn, w_br_diff, w_br_ret, w_o, ffn_norm_w,
           w_ffn_gate, w_ffn_up, w_ffn_down, final_norm_w):
    B, S, D = x.shape
    depth = w_in.shape[0]
    M = B * S
    xs = x.reshape(M, D)

    cos_t, sin_t = rope_tables(S)
    bias = bias_tiles(rel_bias, min(256, S))

    off_d = 4 * HGRN_WIDTH
    off_rqk = off_d + 3 * DIFF_WIDTH
    off_rvg = off_rqk + 2 * RET_QK_WIDTH
    off_gd = off_rvg + 2 * RET_V_WIDTH

    h, ss = prenorm(xs, attn_norm_w[0])
    w_in_l = layer_weight_bf16(w_in, 0)
    for l in range(depth):
        def riding(call, w, **kw):
            out, w_bf = call(side_cast=(w, l), **kw)
            return out, (layer_weight_bf16(w, l) if w_bf is None else w_bf)

        p_h, w_ret = riding(functools.partial(matmul_cols, h, ss, w_in_l, 0, 4 * HGRN_WIDTH, F32),
                            w_br_ret)
        p_d = matmul_cols(h, ss, w_in_l, off_d, 3 * DIFF_WIDTH, BF16)
        p_rqk, w_hgrn = riding(functools.partial(matmul_cols, h, ss, w_in_l, off_rqk,
                                                 2 * RET_QK_WIDTH, F32), w_br_hgrn)
        p_rvg, y_h = matmul_cols_with_hgrn(h, ss, w_in_l, off_rvg, 2 * RET_V_WIDTH, BF16,
                                           p_h, lb_logits, hgrn_norm_w[l], l, B, S)
        gd, w_diff = riding(functools.partial(matmul_cols, h, ss, w_in_l, off_gd, GATE_RANK, BF16,
                                              tn=GATE_RANK), w_br_diff)

        lam_init = 0.8 - 0.6 * math.exp(-0.3 * l)
        y_d, w_gate, w_up = diff_attention(p_d, bias, diff_lambda[l], diff_norm_w[l], lam_init, B, S,
                                           side_casts=((w_ffn_gate, l), (w_ffn_up, l)))
        if w_gate is None:
            w_gate, w_up = w_ffn_gate, w_ffn_up
        y_r = retention_mixer(p_rqk, p_rvg, cos_t, sin_t, ret_norm_w[l], B, S)

        merged, w_o_l = riding(functools.partial(gated_merge, y_h, y_d, y_r, gd, w_hgrn, w_diff, w_ret,
                                                 layer_weight_bf16(w_gate_up, l), b_gate[l]), w_o)
        xs, h2, ss2 = matmul_residual(merged, w_o_l, xs, ffn_norm_w[l], in_place=l > 0)
        act, w_down = swiglu_up(h2, ss2, w_gate, w_up, l, (w_ffn_down, l))
        if w_down is None:
            w_down = layer_weight_bf16(w_ffn_down, l)
        if l + 1 < depth:
            xs, h, ss, w_in_l = matmul_residual(act, w_down, xs, attn_norm_w[l + 1], tn=256,
                                                a_buffers=1, side_cast=(w_in, l + 1))
            if w_in_l is None:
                w_in_l = layer_weight_bf16(w_in, l + 1)
        else:
            xs = matmul_residual(act, w_down, xs, tn=256, a_buffers=1)

    out = rms_norm_rows(xs, final_norm_w, x.dtype)
    return out.reshape(B, S, D)
```

```python
import functools
import math

import jax
import jax.numpy as jnp
from jax import lax
from jax.experimental import pallas as pl
from jax.experimental.pallas import tpu as pltpu

F32 = jnp.float32
BF16 = jnp.bfloat16

HGRN_HEADS = 8
HGRN_DIM = 128
DIFF_HEADS = 8
DIFF_HEAD_DIM = 64
RET_HEADS = 8
RET_KDIM = 128
RET_VDIM = 256
N_BUCKETS = 32
MAX_DISTANCE = 128
ROPE_BASE = 10000.0
GATE_RANK = 256
RMS_EPS = 1e-6
HGRN_WIDTH = HGRN_HEADS * HGRN_DIM
DIFF_WIDTH = DIFF_HEADS * 2 * DIFF_HEAD_DIM
RET_QK_WIDTH = RET_HEADS * RET_KDIM
RET_V_WIDTH = RET_HEADS * RET_VDIM

V7X_VMEM_BYTES = 64 * 1024 * 1024
VMEM_LIMIT = V7X_VMEM_BYTES - 8 * 1024 * 1024
LANES = 128
BF16_TILE_ROWS = 16
F32_TILE_ROWS = 8
LOG2E = math.log2(math.e)

NEG = -1e30
HGRN_CHUNK = 128
HGRN_SUB = 16
DIFF_GROUP = 4
DIFF_QTILES = 2
RET_STEP_HEADS = 4
MM_SUB_ROWS = 512
HGRN_STEP_HEADS = 4


def _params(*sem):
    return pltpu.CompilerParams(dimension_semantics=sem, vmem_limit_bytes=VMEM_LIMIT)


def _sigmoid(x):
    return 1.0 / (1.0 + jnp.exp(-x))


def _dot(a, b):
    return jnp.dot(a, b, preferred_element_type=F32)


def _dot_nt(a, b):
    return lax.dot_general(a, b, (((1,), (1,)), ((), ())), preferred_element_type=F32)


def _rms_kernel(x_ref, w_ref, o_ref):
    x = x_ref[...]
    ms = jnp.mean(x * x, axis=-1, keepdims=True)
    o_ref[...] = (x * lax.rsqrt(ms + RMS_EPS) * w_ref[...]).astype(o_ref.dtype)


def rms_norm_rows(x, w, out_dtype, tm=256):
    M, D = x.shape
    tm = min(tm, M)
    return pl.pallas_call(
        _rms_kernel,
        grid=(M // tm,),
        in_specs=[pl.BlockSpec((tm, D), lambda i: (i, 0)),
                  pl.BlockSpec((1, D), lambda i: (0, 0))],
        out_specs=pl.BlockSpec((tm, D), lambda i: (i, 0)),
        out_shape=jax.ShapeDtypeStruct((M, D), out_dtype),
        compiler_params=_params("parallel"),
        name="rmsnorm",
    )(x, w.reshape(1, D))


def _cast_kernel(w_ref, o_ref):
    o_ref[...] = w_ref[...].astype(o_ref.dtype)


def layer_weight_bf16(w, layer, block_bytes=8 * 1024 * 1024):
    _, R, C = w.shape
    rb = R
    while rb * C * 4 > block_bytes and rb % (2 * BF16_TILE_ROWS) == 0:
        rb //= 2
    assert R % rb == 0
    return pl.pallas_call(
        _cast_kernel,
        grid=(R // rb,),
        in_specs=[pl.BlockSpec((None, rb, C), lambda i: (layer, i, 0))],
        out_specs=pl.BlockSpec((rb, C), lambda i: (i, 0)),
        out_shape=jax.ShapeDtypeStruct((R, C), BF16),
        compiler_params=_params("parallel"),
        name="weight_bf16",
    )(w)


def _with_side_cast(kernel_fn, n_in, n_out, n_side=1):
    def wrapped(*refs):
        o0 = n_in + n_side
        for k in range(n_side):
            side_out = refs[o0 + n_out + k]
            side_out[...] = refs[n_in + k][...].astype(side_out.dtype)
        kernel_fn(*refs[:n_in], *refs[o0:o0 + n_out], *refs[o0 + n_out + n_side:])
    return wrapped


def _side_cast_specs(w, layer, grid):
    _, R, C = w.shape
    steps = math.prod(grid)
    if R % steps or (R // steps) % BF16_TILE_ROWS:
        return None
    rb = R // steps

    def step(*ids):
        lin = ids[0]
        for extent, idx in zip(grid[1:], ids[1:]):
            lin = lin * extent + idx
        return lin
    return (pl.BlockSpec((None, rb, C), lambda *ids: (layer, step(*ids), 0)),
            pl.BlockSpec((rb, C), lambda *ids: (step(*ids), 0)),
            jax.ShapeDtypeStruct((R, C), BF16))


def _lane_partial_sumsq(x):
    sq = x * x
    part = sq[:, 0:LANES]
    for k in range(1, x.shape[1] // LANES):
        part = part + sq[:, k * LANES:(k + 1) * LANES]
    return part


def _row_scale(ss, d_model):
    ms = jnp.sum(ss, axis=-1, keepdims=True) / d_model
    return jnp.broadcast_to(lax.rsqrt(ms + RMS_EPS), ss.shape)


def _prenorm_kernel(x_ref, w_ref, xw_ref, ss_ref):
    x = x_ref[...]
    xw_ref[...] = (x * w_ref[...]).astype(xw_ref.dtype)
    ss_ref[...] = _lane_partial_sumsq(x)


def prenorm(x, w, tm=256):
    M, D = x.shape
    tm = min(tm, M)
    return pl.pallas_call(
        _prenorm_kernel,
        grid=(M // tm,),
        in_specs=[pl.BlockSpec((tm, D), lambda i: (i, 0)),
                  pl.BlockSpec((1, D), lambda i: (0, 0))],
        out_specs=[pl.BlockSpec((tm, D), lambda i: (i, 0)),
                   pl.BlockSpec((tm, LANES), lambda i: (i, 0))],
        out_shape=[jax.ShapeDtypeStruct((M, D), BF16), jax.ShapeDtypeStruct((M, LANES), F32)],
        compiler_params=_params("parallel"),
        name="prenorm",
    )(x, w.reshape(1, D))


def _row_blocks(tm):
    sub = min(tm, MM_SUB_ROWS)
    return [slice(r, r + sub) for r in range(0, tm, sub)]


def _mm_kernel(a_ref, ss_ref, b_ref, o_ref, r_ref):
    @pl.when(pl.program_id(1) == 0)
    def _():
        r_ref[...] = _row_scale(ss_ref[...], a_ref.shape[1])
    scale = jnp.tile(r_ref[...], (1, o_ref.shape[1] // LANES))
    o_ref[...] = (_dot(a_ref[...], b_ref[...]) * scale).astype(o_ref.dtype)


def _call_2d(kernel_fn, name, grid, in_specs, out_spec, out_shape, args, side_cast, scratch_shapes=()):
    side = _side_cast_specs(*side_cast, grid) if side_cast is not None else None
    if side is None:
        out = pl.pallas_call(
            kernel_fn, grid=grid, in_specs=in_specs, out_specs=out_spec, out_shape=out_shape,
            scratch_shapes=list(scratch_shapes),
            compiler_params=_params("parallel", "arbitrary"), name=name,
        )(*args)
        return out if side_cast is None else (out, None)
    return pl.pallas_call(
        _with_side_cast(kernel_fn, len(in_specs), 1), grid=grid,
        in_specs=list(in_specs) + [side[0]], out_specs=[out_spec, side[1]],
        out_shape=[out_shape, side[2]], scratch_shapes=list(scratch_shapes),
        compiler_params=_params("parallel", "arbitrary"), name=name + "_cast",
    )(*args, side_cast[0])


def matmul_cols(a, ss, b, col_off, n, out_dtype, tm=1024, tn=1024, side_cast=None):
    M, K = a.shape
    tm, tn = min(tm, M), min(tn, n)
    assert col_off % tn == 0 and n % tn == 0 and M % tm == 0
    off = col_off // tn
    return _call_2d(
        _mm_kernel, "matmul_cols", (M // tm, n // tn),
        [pl.BlockSpec((tm, K), lambda i, j: (i, 0)),
         pl.BlockSpec((tm, LANES), lambda i, j: (i, 0)),
         pl.BlockSpec((K, tn), lambda i, j: (0, j + off))],
        pl.BlockSpec((tm, tn), lambda i, j: (i, j)),
        jax.ShapeDtypeStruct((M, n), out_dtype),
        (a, ss, b), side_cast, scratch_shapes=[pltpu.VMEM((tm, LANES), F32)])


def _mm_res_kernel(a_ref, b_ref, x_ref, o_ref):
    o_ref[...] = x_ref[...] + _dot(a_ref[...], b_ref[...])


def _mm_res_norm_kernel(a_ref, b_ref, x_ref, w_ref, o_ref, xw_ref, ss_ref):
    y = x_ref[...] + _dot(a_ref[...], b_ref[...])
    o_ref[...] = y
    xw_ref[...] = (y * w_ref[...]).astype(xw_ref.dtype)
    part = _lane_partial_sumsq(y)

    @pl.when(pl.program_id(1) == 0)
    def _():
        ss_ref[...] = part

    @pl.when(pl.program_id(1) > 0)
    def _():
        ss_ref[...] = ss_ref[...] + part


def matmul_residual(a, b, x, next_norm_w=None, in_place=True, tm=1024, tn=512, a_buffers=2,
                    side_cast=None):
    M, K = a.shape
    N = b.shape[1]
    tm, tn = min(tm, M), min(tn, N)
    assert M % tm == 0 and N % tn == 0
    tile = pl.BlockSpec((tm, tn), lambda i, j: (i, j))
    in_specs = [pl.BlockSpec((tm, K), lambda i, j: (i, 0), pipeline_mode=pl.Buffered(a_buffers)),
                pl.BlockSpec((K, tn), lambda i, j: (0, j)),
                tile]
    aliases = {2: 0} if in_place else {}
    grid = (M // tm, N // tn)
    side = _side_cast_specs(*side_cast, grid) if side_cast is not None else None
    if side is not None:
        assert next_norm_w is not None
        out = pl.pallas_call(
            _with_side_cast(_mm_res_norm_kernel, 4, 3),
            grid=grid,
            in_specs=in_specs + [pl.BlockSpec((1, tn), lambda i, j: (0, j)), side[0]],
            out_specs=[tile, tile, pl.BlockSpec((tm, LANES), lambda i, j: (i, 0)), side[1]],
            out_shape=[jax.ShapeDtypeStruct((M, N), F32), jax.ShapeDtypeStruct((M, N), BF16),
                       jax.ShapeDtypeStruct((M, LANES), F32), side[2]],
            input_output_aliases=aliases,
            compiler_params=_params("parallel", "arbitrary"),
            name="matmul_residual_norm_cast",
        )(a, b, x, next_norm_w.reshape(1, N), side_cast[0])
        return out
    if side_cast is not None:
        return (*matmul_residual(a, b, x, next_norm_w, in_place, tm, tn, a_buffers), None)
    if next_norm_w is None:
        return pl.pallas_call(
            _mm_res_kernel,
            grid=(M // tm, N // tn),
            in_specs=in_specs,
            out_specs=tile,
            out_shape=jax.ShapeDtypeStruct((M, N), F32),
            input_output_aliases=aliases,
            compiler_params=_params("parallel", "arbitrary"),
            name="matmul_residual",
        )(a, b, x)
    return pl.pallas_call(
        _mm_res_norm_kernel,
        grid=(M // tm, N // tn),
        in_specs=in_specs + [pl.BlockSpec((1, tn), lambda i, j: (0, j))],
        out_specs=[tile, tile, pl.BlockSpec((tm, LANES), lambda i, j: (i, 0))],
        out_shape=[jax.ShapeDtypeStruct((M, N), F32), jax.ShapeDtypeStruct((M, N), BF16),
                   jax.ShapeDtypeStruct((M, LANES), F32)],
        input_output_aliases=aliases,
        compiler_params=_params("parallel", "arbitrary"),
        name="matmul_residual_norm",
    )(a, b, x, next_norm_w.reshape(1, N))


def _swiglu_kernel(a_ref, ss_ref, wg_ref, wu_ref, o_ref, r_ref):
    @pl.when(pl.program_id(1) == 0)
    def _():
        r_ref[...] = _row_scale(ss_ref[...], a_ref.shape[1])
    wg = wg_ref[...].astype(BF16)
    wu = wu_ref[...].astype(BF16)
    for rows in _row_blocks(a_ref.shape[0]):
        scale = jnp.tile(r_ref[rows, :], (1, o_ref.shape[1] // LANES))
        a = a_ref[rows, :]
        g = _dot(a, wg) * scale
        u = _dot(a, wu) * scale
        o_ref[rows, :] = (g * _sigmoid(g) * u).astype(o_ref.dtype)


def swiglu_up(a, ss, wg, wu, layer, side_cast, tm=2048, tn=256):
    M, K = a.shape
    N = wg.shape[-1]
    tm, tn = min(tm, M), min(tn, N)
    assert M % tm == 0 and N % tn == 0
    grid = (M // tm, N // tn)
    if wg.ndim == 3:
        w_spec = pl.BlockSpec((None, K, tn), lambda i, j: (layer, 0, j))
        a_buffers = 1
    else:
        w_spec = pl.BlockSpec((K, tn), lambda i, j: (0, j))
        a_buffers = 2
    in_specs = [pl.BlockSpec((tm, K), lambda i, j: (i, 0), pipeline_mode=pl.Buffered(a_buffers)),
                pl.BlockSpec((tm, LANES), lambda i, j: (i, 0)),
                w_spec, w_spec]
    tile = pl.BlockSpec((tm, tn), lambda i, j: (i, j))
    side = _side_cast_specs(*side_cast, grid)
    if side is None:
        return pl.pallas_call(
            _swiglu_kernel,
            grid=grid,
            in_specs=in_specs,
            out_specs=tile,
            out_shape=jax.ShapeDtypeStruct((M, N), BF16),
            scratch_shapes=[pltpu.VMEM((tm, LANES), F32)],
            compiler_params=_params("parallel", "arbitrary"),
            name="swiglu_up",
        )(a, ss, wg, wu), None
    return pl.pallas_call(
        _with_side_cast(_swiglu_kernel, 4, 1),
        grid=grid,
        in_specs=in_specs + [side[0]],
        out_specs=[tile, side[1]],
        out_shape=[jax.ShapeDtypeStruct((M, N), BF16), side[2]],
        scratch_shapes=[pltpu.VMEM((tm, LANES), F32)],
        compiler_params=_params("parallel", "arbitrary"),
        name="swiglu_up_cast",
    )(a, ss, wg, wu, side_cast[0])


def _merge_kernel(yh_ref, yd_ref, yr_ref, gd_ref, wh_ref, wd_ref, wr_ref,
                  wgh_ref, wgd_ref, wgr_ref, bh_ref, bd_ref, br_ref, o_ref):
    gd = gd_ref[...]

    def branch(y_ref, w_ref, wg_ref, b_ref):
        gate = _sigmoid(_dot(gd, wg_ref[...]) + b_ref[...])
        return gate * _dot(y_ref[...], w_ref[...])

    acc = branch(yh_ref, wh_ref, wgh_ref, bh_ref)
    acc = acc + branch(yd_ref, wd_ref, wgd_ref, bd_ref)
    acc = acc + branch(yr_ref, wr_ref, wgr_ref, br_ref)
    o_ref[...] = acc.astype(o_ref.dtype)


def gated_merge(yh, yd, yr, gd, wh, wd, wr, wg, bg, tm=1024, tn=512, side_cast=None):
    M = yh.shape[0]
    D = wh.shape[1]
    tm, tn = min(tm, M), min(tn, D)
    assert M % tm == 0 and D % tn == 0
    nb = D // tn
    row = lambda width: pl.BlockSpec((tm, width), lambda i, j: (i, 0))
    wcol = lambda k: pl.BlockSpec((k, tn), lambda i, j: (0, j))
    gcol = lambda br: pl.BlockSpec((GATE_RANK, tn), lambda i, j: (0, j + br * nb))
    bcol = lambda br: pl.BlockSpec((1, tn), lambda i, j: (0, j + br * nb))
    bg2 = bg.reshape(1, -1)
    return _call_2d(
        _merge_kernel, "gated_merge", (M // tm, nb),
        [row(yh.shape[1]), row(yd.shape[1]), row(yr.shape[1]), row(gd.shape[1]),
         wcol(wh.shape[0]), wcol(wd.shape[0]), wcol(wr.shape[0]),
         gcol(0), gcol(1), gcol(2), bcol(0), bcol(1), bcol(2)],
        pl.BlockSpec((tm, tn), lambda i, j: (i, j)),
        jax.ShapeDtypeStruct((M, D), BF16),
        (yh, yd, yr, gd, wh, wd, wr, wg, wg, wg, bg2, bg2, bg2), side_cast)


def _rope_table_kernel(inv_ref, cos_ref, sin_ref):
    T = cos_ref.shape[0]
    pos = (lax.broadcasted_iota(jnp.int32, (T, LANES), 0) + pl.program_id(0) * T).astype(F32)
    ang = pos * inv_ref[...]
    lane = lax.broadcasted_iota(jnp.int32, (T, LANES), 1)
    cos_ref[...] = jnp.cos(ang)
    sin_ref[...] = jnp.where(lane < LANES // 2, -jnp.sin(ang), jnp.sin(ang))


def rope_tables(S, T=256):
    T = min(T, S)
    half = RET_KDIM // 2
    inv = 1.0 / (ROPE_BASE ** (jnp.arange(half, dtype=F32) / half))
    inv2 = jnp.concatenate([inv, inv]).reshape(1, LANES)
    return pl.pallas_call(
        _rope_table_kernel,
        grid=(S // T,),
        in_specs=[pl.BlockSpec((1, LANES), lambda i: (0, 0))],
        out_specs=[pl.BlockSpec((T, LANES), lambda i: (i, 0))] * 2,
        out_shape=[jax.ShapeDtypeStruct((S, LANES), F32)] * 2,
        compiler_params=_params("parallel"),
        name="rope_tables",
    )(inv2)


def _bias_tile_kernel(tab_ref, o_ref):
    h = pl.program_id(0)
    T = o_ref.shape[2]
    j = lax.broadcasted_iota(jnp.int32, (T, T), 0)
    i = lax.broadcasted_iota(jnp.int32, (T, T), 1)
    max_exact = N_BUCKETS // 2
    o_ref[0, 3] = jnp.full((T, T), NEG, F32)
    for d in range(3):
        n = jnp.maximum(i - j + d * T, 0)
        nf = jnp.maximum(n, 1).astype(F32)
        large = max_exact + (jnp.log(nf / max_exact) / math.log(MAX_DISTANCE / max_exact)
                             * (N_BUCKETS - max_exact)).astype(jnp.int32)
        large = jnp.minimum(large, N_BUCKETS - 1)
        bucket = jnp.where(n < max_exact, n, large)
        val = jnp.zeros((T, T), F32)
        for b in range(N_BUCKETS):
            val = jnp.where(bucket == b, tab_ref[h, b], val)
        val = (val - tab_ref[h, N_BUCKETS - 1]) * LOG2E
        if d == 0:
            val = jnp.where(j > i, NEG, val)
        o_ref[0, d] = val


def bias_tiles(rel_bias, T):
    H = rel_bias.shape[1]
    return pl.pallas_call(
        _bias_tile_kernel,
        grid=(H,),
        in_specs=[pl.BlockSpec(memory_space=pltpu.SMEM)],
        out_specs=pl.BlockSpec((1, 4, T, T), lambda h: (h, 0, 0, 0)),
        out_shape=jax.ShapeDtypeStruct((H, 4, T, T), F32),
        compiler_params=_params("parallel"),
        name="bias_tiles",
    )(rel_bias.T)


def _hgrn_kernel(lbl_ref, nw_ref, q_ref, f_ref, i_ref, g_ref, o_ref,
                 st_ref, p_ref, r_ref, ol_ref, *, layer):
    @pl.when(pl.program_id(2) == 0)
    def _():
        st_ref[...] = jnp.zeros_like(st_ref)

    for hh in range(HGRN_STEP_HEADS):
        lanes = slice(hh * HGRN_DIM, (hh + 1) * HGRN_DIM)
        _hgrn_head(lbl_ref[:, lanes], nw_ref[...], q_ref[:, lanes], f_ref[:, lanes], i_ref[:, lanes],
                   g_ref[:, lanes], o_ref.at[:, lanes], st_ref.at[hh], p_ref.at[hh], r_ref.at[hh],
                   ol_ref.at[hh], layer)


def _hgrn_head(lg, nw, q, f, iv_all, g_all, o_ref, st_ref, p_ref, r_ref, ol_ref, layer):
    bcum, qf, kf = _hgrn_head_gates(lg, q, f, layer)
    _hgrn_head_mix(bcum, qf, kf, nw, iv_all, g_all, o_ref, st_ref, p_ref, r_ref, ol_ref)


def _hgrn_head_gates(lg, q, f, layer):
    T = q.shape[0]
    C = HGRN_CHUNK

    e = jnp.exp(lg - jnp.max(lg, axis=0, keepdims=True))
    sm = e / jnp.sum(e, axis=0, keepdims=True)
    csum = sm[0:1]
    for r in range(1, layer + 1):
        csum = csum + sm[r:r + 1]
    lb = csum - sm[0:1]

    forget = lb + (1.0 - lb) * _sigmoid(f)
    kf = 1.0 - forget
    logf = jnp.log(forget)
    qf = q * _sigmoid(q) * (HGRN_DIM ** -0.5)

    r_i = lax.broadcasted_iota(jnp.int32, (T, T), 0)
    c_i = lax.broadcasted_iota(jnp.int32, (T, T), 1)
    shift = C.bit_length() - 1
    same_chunk = (r_i >> shift) == (c_i >> shift)
    tri = jnp.where(c_i <= r_i, jnp.where(same_chunk, 1.0, 0.0), 0.0).astype(BF16)
    hi = logf.astype(BF16)
    rem = logf - hi.astype(F32)
    mid = rem.astype(BF16)
    lo = (rem - mid.astype(F32)).astype(BF16)
    bcum = (_dot(tri, hi) + _dot(tri, mid) + _dot(tri, lo)) * LOG2E
    return bcum, qf, kf


def _hgrn_head_mix(bcum, qf, kf, nw, iv_all, g_all, o_ref, st_ref, p_ref, r_ref, ol_ref):
    T = qf.shape[0]
    C, SUB = HGRN_CHUNK, HGRN_SUB
    row_c = lax.broadcasted_iota(jnp.int32, (C, HGRN_DIM), 0)
    row_s = lax.broadcasted_iota(jnp.int32, (SUB, HGRN_DIM), 0)
    d_i = lax.broadcasted_iota(jnp.int32, (2 * HGRN_DIM, 2 * HGRN_DIM), 0)
    d_j = lax.broadcasted_iota(jnp.int32, (2 * HGRN_DIM, 2 * HGRN_DIM), 1)
    ones2 = jnp.where((d_i < HGRN_DIM) == (d_j < HGRN_DIM), 1.0, 0.0).astype(BF16)

    def slot(r):
        half, rb = divmod(r, T // 2)
        return rb * SUB, slice(half * HGRN_DIM, (half + 1) * HGRN_DIM)

    n_sub = T // SUB
    causal_add = [jnp.where(row_s[(s // F32_TILE_ROWS) * F32_TILE_ROWS:] >= s, 0.0, NEG)
                  for s in range(SUB)]
    for sb in range(n_sub):
        sub = slice(sb * SUB, (sb + 1) * SUB)
        bi, qi = bcum[sub], qf[sub]
        for s in range(SUB):
            r = sb * SUB + s
            top = (s // F32_TILE_ROWS) * F32_TILE_ROWS
            dec = jnp.exp2(bi[top:] - bcum[r:r + 1] + causal_add[s])
            p = qi[top:] * kf[r:r + 1] * dec
            if top:
                p = jnp.concatenate([jnp.zeros((top, HGRN_DIM), F32), p], axis=0)
            r0, lanes = slot(r)
            p_ref[r0:r0 + SUB, lanes] = p.astype(BF16)
    r_ref[...] = _dot(p_ref[...], ones2)

    n_blk = C // SUB
    for c in range(T // C):
        c0 = c * C
        b, kc = bcum[c0:c0 + C], kf[c0:c0 + C]
        qts, kts = [], []
        for j in range(1, n_blk):
            lo_r = j * SUB
            ref = b[lo_r - 1:lo_r]
            qts.append((qf[c0 + lo_r:c0 + lo_r + SUB] * jnp.exp2(b[lo_r:lo_r + SUB] - ref)).astype(BF16))
            kts.append((kc * jnp.exp2(jnp.where(row_c < lo_r, ref - b, NEG))).astype(BF16))
        att_all = _dot_nt(jnp.concatenate(qts, axis=0), jnp.concatenate(kts, axis=0))
        att = jnp.concatenate([att_all[j * SUB:(j + 1) * SUB, j * C:(j + 1) * C]
                               for j in range(n_blk - 1)], axis=0)
        off = _dot(att.astype(BF16), iv_all[c0:c0 + C].astype(BF16))
        for j in range(n_blk):
            lo_r = c0 + j * SUB
            r0, lanes = slot(lo_r)
            oi = r_ref[r0:r0 + SUB, lanes] * iv_all[lo_r:lo_r + 1]
            for s in range(1, SUB):
                r = lo_r + s
                top = (s // F32_TILE_ROWS) * F32_TILE_ROWS
                r0, lanes = slot(r)
                term = r_ref[r0 + top:r0 + SUB, lanes] * iv_all[r:r + 1]
                oi = oi + term if top == 0 else jnp.concatenate([oi[:top], oi[top:] + term], axis=0)
            if j > 0:
                oi = oi + off[(j - 1) * SUB:j * SUB]
            ol_ref[lo_r:lo_r + SUB, :] = oi

    state_t = st_ref[...]
    for c in range(T // C):
        rows = slice(c * C, (c + 1) * C)
        b, qc, kc, iv = bcum[rows], qf[rows], kf[rows], iv_all[rows]
        o = ol_ref[rows, :] + _dot_nt((qc * jnp.exp2(b)).astype(BF16), state_t.astype(BF16))
        b_last = b[C - 1:C]
        khat = (kc * jnp.exp2(b_last - b)).astype(BF16)
        state_t = state_t * jnp.exp2(b_last) + _dot(iv.T.astype(BF16), khat)
        ms = jnp.mean(o * o, axis=-1, keepdims=True)
        gc = g_all[rows]
        y = o * lax.rsqrt(ms + RMS_EPS) * nw * (gc * _sigmoid(gc))
        o_ref[rows, :] = y.astype(o_ref.dtype)
    st_ref[...] = state_t


def hgrn_mixer(p_h, lb_logits, norm_w, layer, B, S, T=256):
    T = min(T, S)
    nt = S // T
    H, dk, NH = HGRN_HEADS, HGRN_DIM, HGRN_STEP_HEADS
    HG = H // NH
    col = lambda grp: pl.BlockSpec((T, NH * dk), lambda b, h, t: (b * nt + t, grp * HG + h))
    return pl.pallas_call(
        functools.partial(_hgrn_kernel, layer=layer),
        grid=(B, HG, nt),
        in_specs=[pl.BlockSpec((lb_logits.shape[0], NH * dk), lambda b, h, t: (0, h)),
                  pl.BlockSpec((1, dk), lambda b, h, t: (0, 0)),
                  col(0), col(1), col(2), col(3)],
        out_specs=pl.BlockSpec((T, NH * dk), lambda b, h, t: (b * nt + t, h)),
        out_shape=jax.ShapeDtypeStruct((B * S, H * dk), BF16),
        scratch_shapes=[pltpu.VMEM((NH, dk, dk), F32),
                        pltpu.VMEM((NH, T // 2 * HGRN_SUB, 2 * dk), BF16),
                        pltpu.VMEM((NH, T // 2 * HGRN_SUB, 2 * dk), F32),
                        pltpu.VMEM((NH, T, dk), F32)],
        compiler_params=_params("parallel", "parallel", "arbitrary"),
        name="hgrn_mixer",
    )(lb_logits, norm_w.reshape(1, dk), p_h, p_h, p_h, p_h)


def _mm_hgrn_kernel(a_ref, ss_ref, b_ref, lbl_ref, nw_ref, q_ref, f_ref, i_ref, g_ref,
                    o_ref, y_ref, r_ref, st_ref, p_ref, r2_ref, ol_ref,
                    *, layer, n_col_blocks, n_time_blocks):
    step = pl.program_id(0)

    @pl.when(step % n_col_blocks == 0)
    def _():
        r_ref[...] = _row_scale(ss_ref[...], a_ref.shape[1])

    @pl.when(step % n_time_blocks == 0)
    def _():
        st_ref[...] = jnp.zeros_like(st_ref)

    b = b_ref[...]
    sub = a_ref.shape[0] // HGRN_STEP_HEADS
    for hh in range(HGRN_STEP_HEADS):
        lanes = slice(hh * HGRN_DIM, (hh + 1) * HGRN_DIM)
        bcum, qf, kf = _hgrn_head_gates(lbl_ref[:, lanes], q_ref[:, lanes], f_ref[:, lanes], layer)
        rows = slice(hh * sub, (hh + 1) * sub)
        scale = jnp.tile(r_ref[rows, :], (1, o_ref.shape[1] // LANES))
        o_ref[rows, :] = (_dot(a_ref[rows, :], b) * scale).astype(o_ref.dtype)
        _hgrn_head_mix(bcum, qf, kf, nw_ref[...], i_ref[:, lanes], g_ref[:, lanes],
                       y_ref.at[:, lanes], st_ref.at[hh], p_ref.at[hh], r2_ref.at[hh], ol_ref.at[hh])


def matmul_cols_with_hgrn(a, ss, b, col_off, n, out_dtype, p_h, lb_logits, norm_w, layer, B, S,
                          tm=1024, tn=512, T=256):
    M, K = a.shape
    tm, tn, T = min(tm, M), min(tn, n), min(T, S)
    nt = S // T
    H, dk, NH = HGRN_HEADS, HGRN_DIM, HGRN_STEP_HEADS
    HG = H // NH
    ncb = n // tn
    if (M // tm) * ncb != B * HG * nt:
        return (matmul_cols(a, ss, b, col_off, n, out_dtype),
                hgrn_mixer(p_h, lb_logits, norm_w, layer, B, S))
    assert col_off % tn == 0 and n % tn == 0 and M % tm == 0
    off = col_off // tn
    row_blk = lambda s: (s // (HG * nt)) * nt + s % nt
    head_grp = lambda s: (s // nt) % HG
    col = lambda grp: pl.BlockSpec((T, NH * dk), lambda s: (row_blk(s), grp * HG + head_grp(s)))
    return pl.pallas_call(
        functools.partial(_mm_hgrn_kernel, layer=layer, n_col_blocks=ncb, n_time_blocks=nt),
        grid=((M // tm) * ncb,),
        in_specs=[pl.BlockSpec((tm, K), lambda s: (s // ncb, 0)),
                  pl.BlockSpec((tm, LANES), lambda s: (s // ncb, 0)),
                  pl.BlockSpec((K, tn), lambda s: (0, s % ncb + off)),
                  pl.BlockSpec((lb_logits.shape[0], NH * dk), lambda s: (0, head_grp(s))),
                  pl.BlockSpec((1, dk), lambda s: (0, 0)),
                  col(0), col(1), col(2), col(3)],
        out_specs=[pl.BlockSpec((tm, tn), lambda s: (s // ncb, s % ncb)),
                   pl.BlockSpec((T, NH * dk), lambda s: (row_blk(s), head_grp(s)))],
        out_shape=[jax.ShapeDtypeStruct((M, n), out_dtype),
                   jax.ShapeDtypeStruct((B * S, H * dk), BF16)],
        scratch_shapes=[pltpu.VMEM((tm, LANES), F32),
                        pltpu.VMEM((NH, dk, dk), F32),
                        pltpu.VMEM((NH, T // 2 * HGRN_SUB, 2 * dk), BF16),
                        pltpu.VMEM((NH, T // 2 * HGRN_SUB, 2 * dk), F32),
                        pltpu.VMEM((NH, T, dk), F32)],
        compiler_params=_params("arbitrary"),
        name="matmul_cols_hgrn",
    )(a, ss, b, lb_logits, norm_w.reshape(1, dk), p_h, p_h, p_h, p_h)


def _diff_kernel(lam_ref, nw_ref, q_ref, k_ref, v_ref, bias_ref, o_ref,
                 vt_ref, s0_ref, s1_ref, p0_ref, p1_ref, gm_ref, alpha_ref, m_ref, acc_ref, *, lam_init):
    QT = DIFF_QTILES
    T = q_ref.shape[0] // QT
    S = k_ref.shape[0]
    dh = DIFF_HEAD_DIM
    G = DIFF_GROUP
    n_groups = S // (G * T)
    qi = pl.program_id(2) * QT
    ng = (qi + QT - 1) // G + 1

    @pl.when(qi == 0)
    def _():
        def body(c, carry):
            start = pl.multiple_of(c * T, T)
            vt_ref[0:2 * dh, pl.ds(start, T)] = v_ref[pl.ds(start, T), :].astype(F32).T.astype(BF16)
            return carry
        lax.fori_loop(0, S // T, body, 0)
        pad_row = lax.broadcasted_iota(jnp.int32, (BF16_TILE_ROWS, S), 0)
        vt_ref[2 * dh:, :] = jnp.where(pad_row == 0, 1.0, 0.0).astype(BF16)

    row = lax.broadcasted_iota(jnp.int32, (2 * dh, T), 0)
    cols = []
    for t in range(QT):
        qt = q_ref[t * T:(t + 1) * T, :].astype(F32).T * (dh ** -0.5 * LOG2E)
        cols += [jnp.where(row < dh, qt, 0.0), jnp.where(row >= dh, qt, 0.0)]
    q2 = jnp.concatenate(cols, axis=1).astype(BF16)

    slots = (s0_ref, s1_ref)

    p_slots = (p0_ref, p1_ref)

    def values_group(g):
        pv = _dot(vt_ref[:, g * G * T:(g + 1) * G * T], p_slots[g % 2][...])
        acc_ref[...] = alpha_ref[g % 2] * acc_ref[...] + pv

    def block(g_scores, near, g_values, g_softmax):
        if g_values is not None:
            values_group(g_values)
        if g_softmax is not None:
            sm_slot, p_slot = slots[g_softmax % 2], p_slots[g_softmax % 2]
            m_old = m_ref[...]
            m_new = jnp.maximum(m_old, gm_ref[g_softmax % 2])
            alpha_ref[g_softmax % 2] = jnp.exp2(m_old - m_new)
        gm = None
        for u in range(G):
            if g_scores is not None:
                kt = g_scores * G + u
                s = _dot(k_ref[kt * T:(kt + 1) * T, :], q2)
                if near:
                    tiles = []
                    for t in range(QT):
                        d = qi + t - kt
                        tiles += [bias_ref[0, jnp.where(d < 0, 3, jnp.minimum(d, 2))]] * 2
                    s = s + jnp.concatenate(tiles, axis=1)
                slots[g_scores % 2][u] = s
                cm = jnp.max(s, axis=0, keepdims=True)
                gm = cm if gm is None else jnp.maximum(gm, cm)
            if g_softmax is not None:
                p_slot[u * T:(u + 1) * T, :] = jnp.exp2(sm_slot[u] - m_new).astype(BF16)
        if g_scores is not None:
            gm_ref[g_scores % 2] = gm
        if g_softmax is not None:
            m_ref[...] = m_new

    m_ref[...] = jnp.full(m_ref.shape, NEG, F32)
    acc_ref[...] = jnp.zeros(acc_ref.shape, F32)
    n_far_groups = jnp.maximum(qi - 1, 0) // G

    def stage(cond, g):
        for near in (False, True):
            is_near = g >= n_far_groups
            @pl.when(jnp.logical_and(cond, is_near if near else jnp.logical_not(is_near)))
            def _():
                block(g, near, None, g - 1 if g >= 1 else None)
                if g >= 1:
                    values_group(g - 1)

    stage(True, 0)
    for g in range(n_groups):
        if g + 1 < n_groups:
            stage(g < ng - 1, g + 1)

        @pl.when(g == ng - 1)
        def _():
            block(None, False, None, g)
            values_group(g)

    lp = lam_ref[...]
    lam = (jnp.exp(jnp.sum(lp[0:1] * lp[1:2], axis=-1, keepdims=True))
           - jnp.exp(jnp.sum(lp[2:3] * lp[3:4], axis=-1, keepdims=True)) + lam_init)
    acc = acc_ref[...]
    w = acc[:2 * dh] / acc[2 * dh:2 * dh + 1]
    for t in range(QT):
        c0 = 2 * t * T
        out = (w[:, c0:c0 + T] - lam * w[:, c0 + T:c0 + 2 * T]).T
        ms = jnp.mean(out * out, axis=-1, keepdims=True)
        y = out * lax.rsqrt(ms + RMS_EPS) * nw_ref[...] * (1.0 - lam_init)
        o_ref[t * T:(t + 1) * T, :] = y.astype(o_ref.dtype)


def diff_attention(p_d, bias, lam_params, norm_w, lam_init, B, S, side_casts=()):
    T = bias.shape[2]
    G = DIFF_GROUP
    QT = DIFF_QTILES
    assert T >= MAX_DISTANCE and S % (G * T) == 0 and G % QT == 0
    nq = S // (QT * T)
    lanes = 2 * QT * T
    H, hw = DIFF_HEADS, 2 * DIFF_HEAD_DIM
    grid = (B, H, nq)
    sides = [_side_cast_specs(w, layer, grid) for w, layer in side_casts]
    if any(side is None for side in sides):
        return (diff_attention(p_d, bias, lam_params, norm_w, lam_init, B, S),
                *[None] * len(side_casts))
    kernel_fn = functools.partial(_diff_kernel, lam_init=lam_init)
    if sides:
        kernel_fn = _with_side_cast(kernel_fn, 6, 1, len(sides))
    out_spec = pl.BlockSpec((QT * T, hw), lambda b, h, i: (b * nq + i, h))
    out_shape = jax.ShapeDtypeStruct((B * S, H * hw), BF16)
    return pl.pallas_call(
        kernel_fn,
        grid=grid,
        in_specs=[pl.BlockSpec(lam_params.shape, lambda b, h, i: (0, 0)),
                  pl.BlockSpec((1, hw), lambda b, h, i: (0, 0)),
                  pl.BlockSpec((QT * T, hw), lambda b, h, i: (b * nq + i, h)),
                  pl.BlockSpec((S, hw), lambda b, h, i: (b, H + h)),
                  pl.BlockSpec((S, hw), lambda b, h, i: (b, 2 * H + h)),
                  pl.BlockSpec((1, 4, T, T), lambda b, h, i: (h, 0, 0, 0))] + [s[0] for s in sides],
        out_specs=[out_spec] + [s[1] for s in sides] if sides else out_spec,
        out_shape=[out_shape] + [s[2] for s in sides] if sides else out_shape,
        scratch_shapes=[pltpu.VMEM((hw + BF16_TILE_ROWS, S), BF16),
                        pltpu.VMEM((G, T, lanes), F32), pltpu.VMEM((G, T, lanes), F32),
                        pltpu.VMEM((G * T, lanes), BF16), pltpu.VMEM((G * T, lanes), BF16),
                        pltpu.VMEM((2, 1, lanes), F32), pltpu.VMEM((2, 1, lanes), F32),
                        pltpu.VMEM((1, lanes), F32),
                        pltpu.VMEM((hw + BF16_TILE_ROWS, lanes), F32)],
        compiler_params=_params("parallel", "parallel", "arbitrary"),
        name="diff_attention",
    )(lam_params, norm_w.reshape(1, hw), p_d, p_d, p_d, bias, *[w for w, _ in side_casts])


def _ret_kernel(lg_ref, nw_ref, cos_ref, sin_ref, q_ref, k_ref, v_ref, g_ref, o_ref,
                st_ref, dec_ref, xi_ref, zeta_ref):
    C = q_ref.shape[0]
    dk, dv = RET_KDIM, RET_VDIM

    @pl.when(pl.program_id(2) == 0)
    def _():
        st_ref[...] = jnp.zeros_like(st_ref)
        rowf = lax.broadcasted_iota(jnp.int32, (C, dk), 0).astype(F32)
        r_i = lax.broadcasted_iota(jnp.int32, (C, C), 0)
        c_i = lax.broadcasted_iota(jnp.int32, (C, C), 1)
        for hh in range(RET_STEP_HEADS):
            lg = lg_ref[hh]
            xi_ref[hh] = jnp.exp((rowf + 1.0) * lg)
            zeta_ref[hh] = jnp.exp((C - 1.0 - rowf) * lg)
            dec_ref[hh] = jnp.exp(jnp.where(r_i >= c_i, (r_i - c_i).astype(F32) * lg[:, 0:1], NEG))

    cosf, sinf = cos_ref[...], sin_ref[...]

    def rot(x):
        return x * cosf + pltpu.roll(x, dk // 2, 1) * sinf

    for hh in range(RET_STEP_HEADS):
        qr = rot(q_ref[:, hh * dk:(hh + 1) * dk])
        kr = rot(k_ref[:, hh * dk:(hh + 1) * dk]) * (dk ** -0.5)
        v = v_ref[:, hh * dv:(hh + 1) * dv]
        state = st_ref[hh]
        scores = _dot_nt(qr.astype(BF16), kr.astype(BF16)) * dec_ref[hh]
        o = _dot(scores.astype(BF16), v) + _dot((qr * xi_ref[hh]).astype(BF16), state.astype(BF16))
        gamma_c = jnp.exp(C * lg_ref[hh][:, 0:1])
        st_ref[hh] = gamma_c * state + _dot((kr * zeta_ref[hh]).T.astype(BF16), v)

        ms = jnp.mean(o * o, axis=-1, keepdims=True)
        g = g_ref[:, hh * dv:(hh + 1) * dv].astype(F32)
        y = o * lax.rsqrt(ms + RMS_EPS) * nw_ref[...] * (g * _sigmoid(g))
        o_ref[:, hh * dv:(hh + 1) * dv] = y.astype(o_ref.dtype)


def retention_mixer(p_qk, p_vg, cos_t, sin_t, norm_w, B, S, C=256):
    C = min(C, S)
    nt = S // C
    H, dk, dv, NH = RET_HEADS, RET_KDIM, RET_VDIM, RET_STEP_HEADS
    HG = H // NH
    log_gamma = jnp.log(1.0 - 2.0 ** (-5.0 - jnp.arange(H, dtype=F32)))
    lg = jnp.broadcast_to(log_gamma[:, None, None], (H, 1, LANES))
    return pl.pallas_call(
        _ret_kernel,
        grid=(B, HG, nt),
        in_specs=[pl.BlockSpec((NH, 1, LANES), lambda b, h, t: (h, 0, 0)),
                  pl.BlockSpec((1, dv), lambda b, h, t: (0, 0)),
                  pl.BlockSpec((C, dk), lambda b, h, t: (t, 0)),
                  pl.BlockSpec((C, dk), lambda b, h, t: (t, 0)),
                  pl.BlockSpec((C, NH * dk), lambda b, h, t: (b * nt + t, h)),
                  pl.BlockSpec((C, NH * dk), lambda b, h, t: (b * nt + t, HG + h)),
                  pl.BlockSpec((C, NH * dv), lambda b, h, t: (b * nt + t, h)),
                  pl.BlockSpec((C, NH * dv), lambda b, h, t: (b * nt + t, HG + h))],
        out_specs=pl.BlockSpec((C, NH * dv), lambda b, h, t: (b * nt + t, h)),
        out_shape=jax.ShapeDtypeStruct((B * S, H * dv), BF16),
        scratch_shapes=[pltpu.VMEM((NH, dk, dv), F32), pltpu.VMEM((NH, C, C), F32),
                        pltpu.VMEM((NH, C, dk), F32), pltpu.VMEM((NH, C, dk), F32)],
        compiler_params=_params("parallel", "parallel", "arbitrary"),
        name="retention_mixer",
    )(lg, norm_w.reshape(1, dv), cos_t, sin_t, p_qk, p_qk, p_vg, p_vg)


def kernel(x, attn_norm_w, w_in, lb_logits, hgrn_norm_w, rel_bias, diff_lambda, diff_norm_w,
           ret_norm_w, w_gate_up, b_gate, w_br_hgrn, w_br_diff, w_br_ret, w_o, ffn_norm_w,
           w_ffn_gate, w_ffn_up, w_ffn_down, final_norm_w):
    B, S, D = x.shape
    depth = w_in.shape[0]
    M = B * S
    xs = x.reshape(M, D)

    cos_t, sin_t = rope_tables(S)
    bias = bias_tiles(rel_bias, min(256, S))

    off_d = 4 * HGRN_WIDTH
    off_rqk = off_d + 3 * DIFF_WIDTH
    off_rvg = off_rqk + 2 * RET_QK_WIDTH
    off_gd = off_rvg + 2 * RET_V_WIDTH

    h, ss = prenorm(xs, attn_norm_w[0])
    w_in_l = layer_weight_bf16(w_in, 0)
    for l in range(depth):
        def riding(call, w, **kw):
            out, w_bf = call(side_cast=(w, l), **kw)
            return out, (layer_weight_bf16(w, l) if w_bf is None else w_bf)

        p_h, w_ret = riding(functools.partial(matmul_cols, h, ss, w_in_l, 0, 4 * HGRN_WIDTH, F32),
                            w_br_ret)
        p_d = matmul_cols(h, ss, w_in_l, off_d, 3 * DIFF_WIDTH, BF16)
        p_rqk, w_hgrn = riding(functools.partial(matmul_cols, h, ss, w_in_l, off_rqk,
                                                 2 * RET_QK_WIDTH, F32), w_br_hgrn)
        p_rvg, y_h = matmul_cols_with_hgrn(h, ss, w_in_l, off_rvg, 2 * RET_V_WIDTH, BF16,
                                           p_h, lb_logits, hgrn_norm_w[l], l, B, S)
        gd, w_diff = riding(functools.partial(matmul_cols, h, ss, w_in_l, off_gd, GATE_RANK, BF16,
                                              tn=GATE_RANK), w_br_diff)

        lam_init = 0.8 - 0.6 * math.exp(-0.3 * l)
        y_d, w_gate, w_up = diff_attention(p_d, bias, diff_lambda[l], diff_norm_w[l], lam_init, B, S,
                                           side_casts=((w_ffn_gate, l), (w_ffn_up, l)))
        if w_gate is None:
            w_gate, w_up = w_ffn_gate, w_ffn_up
        y_r = retention_mixer(p_rqk, p_rvg, cos_t, sin_t, ret_norm_w[l], B, S)

        merged, w_o_l = riding(functools.partial(gated_merge, y_h, y_d, y_r, gd, w_hgrn, w_diff, w_ret,
                                                 layer_weight_bf16(w_gate_up, l), b_gate[l]), w_o)
        xs, h2, ss2 = matmul_residual(merged, w_o_l, xs, ffn_norm_w[l], in_place=l > 0)
        act, w_down = swiglu_up(h2, ss2, w_gate, w_up, l, (w_ffn_down, l))
        if w_down is None:
            w_down = layer_weight_bf16(w_ffn_down, l)
        if l + 1 < depth:
            xs, h, ss, w_in_l = matmul_residual(act, w_down, xs, attn_norm_w[l + 1], tm=512, tn=512,
                                                side_cast=(w_in, l + 1))
            if w_in_l is None:
                w_in_l = layer_weight_bf16(w_in, l + 1)
        else:
            xs = matmul_residual(act, w_down, xs, tm=512, tn=512)

    out = rms_norm_rows(xs, final_norm_w, x.dtype)
    return out.reshape(B, S, D)
```

```python
import functools
import math

import jax
import jax.numpy as jnp
from jax import lax
from jax.experimental import pallas as pl
from jax.experimental.pallas import tpu as pltpu

F32 = jnp.float32
BF16 = jnp.bfloat16

HGRN_HEADS = 8
HGRN_DIM = 128
DIFF_HEADS = 8
DIFF_HEAD_DIM = 64
RET_HEADS = 8
RET_KDIM = 128
RET_VDIM = 256
N_BUCKETS = 32
MAX_DISTANCE = 128
ROPE_BASE = 10000.0
GATE_RANK = 256
RMS_EPS = 1e-6
HGRN_WIDTH = HGRN_HEADS * HGRN_DIM
DIFF_WIDTH = DIFF_HEADS * 2 * DIFF_HEAD_DIM
RET_QK_WIDTH = RET_HEADS * RET_KDIM
RET_V_WIDTH = RET_HEADS * RET_VDIM

V7X_VMEM_BYTES = 64 * 1024 * 1024
VMEM_LIMIT = V7X_VMEM_BYTES - 8 * 1024 * 1024
LANES = 128
BF16_TILE_ROWS = 16
F32_TILE_ROWS = 8
LOG2E = math.log2(math.e)

NEG = -1e30
HGRN_CHUNK = 128
HGRN_SUB = 16
DIFF_GROUP = 4
DIFF_QTILES = 2
RET_STEP_HEADS = 8
MM_SUB_ROWS = 512
HGRN_STEP_HEADS = 4


def _params(*sem):
    return pltpu.CompilerParams(dimension_semantics=sem, vmem_limit_bytes=VMEM_LIMIT)


def _sigmoid(x):
    return 1.0 / (1.0 + jnp.exp(-x))


def _dot(a, b):
    return jnp.dot(a, b, preferred_element_type=F32)


def _dot_nt(a, b):
    return lax.dot_general(a, b, (((1,), (1,)), ((), ())), preferred_element_type=F32)


def _rms_kernel(x_ref, w_ref, o_ref):
    x = x_ref[...]
    ms = jnp.mean(x * x, axis=-1, keepdims=True)
    o_ref[...] = (x * lax.rsqrt(ms + RMS_EPS) * w_ref[...]).astype(o_ref.dtype)


def rms_norm_rows(x, w, out_dtype, tm=256):
    M, D = x.shape
    tm = min(tm, M)
    return pl.pallas_call(
        _rms_kernel,
        grid=(M // tm,),
        in_specs=[pl.BlockSpec((tm, D), lambda i: (i, 0)),
                  pl.BlockSpec((1, D), lambda i: (0, 0))],
        out_specs=pl.BlockSpec((tm, D), lambda i: (i, 0)),
        out_shape=jax.ShapeDtypeStruct((M, D), out_dtype),
        compiler_params=_params("parallel"),
        name="rmsnorm",
    )(x, w.reshape(1, D))


def _cast_kernel(w_ref, o_ref):
    o_ref[...] = w_ref[...].astype(o_ref.dtype)


def layer_weight_bf16(w, layer, block_bytes=8 * 1024 * 1024):
    _, R, C = w.shape
    rb = R
    while rb * C * 4 > block_bytes and rb % (2 * BF16_TILE_ROWS) == 0:
        rb //= 2
    assert R % rb == 0
    return pl.pallas_call(
        _cast_kernel,
        grid=(R // rb,),
        in_specs=[pl.BlockSpec((None, rb, C), lambda i: (layer, i, 0))],
        out_specs=pl.BlockSpec((rb, C), lambda i: (i, 0)),
        out_shape=jax.ShapeDtypeStruct((R, C), BF16),
        compiler_params=_params("parallel"),
        name="weight_bf16",
    )(w)


def _with_side_cast(kernel_fn, n_in, n_out, n_side=1):
    def wrapped(*refs):
        o0 = n_in + n_side
        for k in range(n_side):
            side_out = refs[o0 + n_out + k]
            side_out[...] = refs[n_in + k][...].astype(side_out.dtype)
        kernel_fn(*refs[:n_in], *refs[o0:o0 + n_out], *refs[o0 + n_out + n_side:])
    return wrapped


def _side_cast_specs(w, layer, grid):
    _, R, C = w.shape
    steps = math.prod(grid)
    if R % steps or (R // steps) % BF16_TILE_ROWS:
        return None
    rb = R // steps

    def step(*ids):
        lin = ids[0]
        for extent, idx in zip(grid[1:], ids[1:]):
            lin = lin * extent + idx
        return lin
    return (pl.BlockSpec((None, rb, C), lambda *ids: (layer, step(*ids), 0)),
            pl.BlockSpec((rb, C), lambda *ids: (step(*ids), 0)),
            jax.ShapeDtypeStruct((R, C), BF16))


def _lane_partial_sumsq(x):
    sq = x * x
    part = sq[:, 0:LANES]
    for k in range(1, x.shape[1] // LANES):
        part = part + sq[:, k * LANES:(k + 1) * LANES]
    return part


def _row_scale(ss, d_model):
    ms = jnp.sum(ss, axis=-1, keepdims=True) / d_model
    return jnp.broadcast_to(lax.rsqrt(ms + RMS_EPS), ss.shape)


def _prenorm_kernel(x_ref, w_ref, xw_ref, ss_ref):
    x = x_ref[...]
    xw_ref[...] = (x * w_ref[...]).astype(xw_ref.dtype)
    ss_ref[...] = _lane_partial_sumsq(x)


def prenorm(x, w, tm=256):
    M, D = x.shape
    tm = min(tm, M)
    return pl.pallas_call(
        _prenorm_kernel,
        grid=(M // tm,),
        in_specs=[pl.BlockSpec((tm, D), lambda i: (i, 0)),
                  pl.BlockSpec((1, D), lambda i: (0, 0))],
        out_specs=[pl.BlockSpec((tm, D), lambda i: (i, 0)),
                   pl.BlockSpec((tm, LANES), lambda i: (i, 0))],
        out_shape=[jax.ShapeDtypeStruct((M, D), BF16), jax.ShapeDtypeStruct((M, LANES), F32)],
        compiler_params=_params("parallel"),
        name="prenorm",
    )(x, w.reshape(1, D))


def _row_blocks(tm):
    sub = min(tm, MM_SUB_ROWS)
    return [slice(r, r + sub) for r in range(0, tm, sub)]


def _mm_kernel(a_ref, ss_ref, b_ref, o_ref, r_ref):
    @pl.when(pl.program_id(1) == 0)
    def _():
        r_ref[...] = _row_scale(ss_ref[...], a_ref.shape[1])
    scale = jnp.tile(r_ref[...], (1, o_ref.shape[1] // LANES))
    o_ref[...] = (_dot(a_ref[...], b_ref[...]) * scale).astype(o_ref.dtype)


def _call_2d(kernel_fn, name, grid, in_specs, out_spec, out_shape, args, side_cast, scratch_shapes=()):
    side = _side_cast_specs(*side_cast, grid) if side_cast is not None else None
    if side is None:
        out = pl.pallas_call(
            kernel_fn, grid=grid, in_specs=in_specs, out_specs=out_spec, out_shape=out_shape,
            scratch_shapes=list(scratch_shapes),
            compiler_params=_params("parallel", "arbitrary"), name=name,
        )(*args)
        return out if side_cast is None else (out, None)
    return pl.pallas_call(
        _with_side_cast(kernel_fn, len(in_specs), 1), grid=grid,
        in_specs=list(in_specs) + [side[0]], out_specs=[out_spec, side[1]],
        out_shape=[out_shape, side[2]], scratch_shapes=list(scratch_shapes),
        compiler_params=_params("parallel", "arbitrary"), name=name + "_cast",
    )(*args, side_cast[0])


def matmul_cols(a, ss, b, col_off, n, out_dtype, tm=1024, tn=1024, side_cast=None):
    M, K = a.shape
    tm, tn = min(tm, M), min(tn, n)
    assert col_off % tn == 0 and n % tn == 0 and M % tm == 0
    off = col_off // tn
    return _call_2d(
        _mm_kernel, "matmul_cols", (M // tm, n // tn),
        [pl.BlockSpec((tm, K), lambda i, j: (i, 0)),
         pl.BlockSpec((tm, LANES), lambda i, j: (i, 0)),
         pl.BlockSpec((K, tn), lambda i, j: (0, j + off))],
        pl.BlockSpec((tm, tn), lambda i, j: (i, j)),
        jax.ShapeDtypeStruct((M, n), out_dtype),
        (a, ss, b), side_cast, scratch_shapes=[pltpu.VMEM((tm, LANES), F32)])


def _mm_res_kernel(a_ref, b_ref, x_ref, o_ref):
    o_ref[...] = x_ref[...] + _dot(a_ref[...], b_ref[...])


def _mm_res_norm_kernel(a_ref, b_ref, x_ref, w_ref, o_ref, xw_ref, ss_ref):
    y = x_ref[...] + _dot(a_ref[...], b_ref[...])
    o_ref[...] = y
    xw_ref[...] = (y * w_ref[...]).astype(xw_ref.dtype)
    part = _lane_partial_sumsq(y)

    @pl.when(pl.program_id(1) == 0)
    def _():
        ss_ref[...] = part

    @pl.when(pl.program_id(1) > 0)
    def _():
        ss_ref[...] = ss_ref[...] + part


def matmul_residual(a, b, x, next_norm_w=None, in_place=True, tm=1024, tn=512, a_buffers=2,
                    side_cast=None):
    M, K = a.shape
    N = b.shape[1]
    tm, tn = min(tm, M), min(tn, N)
    assert M % tm == 0 and N % tn == 0
    tile = pl.BlockSpec((tm, tn), lambda i, j: (i, j))
    in_specs = [pl.BlockSpec((tm, K), lambda i, j: (i, 0), pipeline_mode=pl.Buffered(a_buffers)),
                pl.BlockSpec((K, tn), lambda i, j: (0, j)),
                tile]
    aliases = {2: 0} if in_place else {}
    grid = (M // tm, N // tn)
    side = _side_cast_specs(*side_cast, grid) if side_cast is not None else None
    if side is not None:
        assert next_norm_w is not None
        out = pl.pallas_call(
            _with_side_cast(_mm_res_norm_kernel, 4, 3),
            grid=grid,
            in_specs=in_specs + [pl.BlockSpec((1, tn), lambda i, j: (0, j)), side[0]],
            out_specs=[tile, tile, pl.BlockSpec((tm, LANES), lambda i, j: (i, 0)), side[1]],
            out_shape=[jax.ShapeDtypeStruct((M, N), F32), jax.ShapeDtypeStruct((M, N), BF16),
                       jax.ShapeDtypeStruct((M, LANES), F32), side[2]],
            input_output_aliases=aliases,
            compiler_params=_params("parallel", "arbitrary"),
            name="matmul_residual_norm_cast",
        )(a, b, x, next_norm_w.reshape(1, N), side_cast[0])
        return out
    if side_cast is not None:
        return (*matmul_residual(a, b, x, next_norm_w, in_place, tm, tn, a_buffers), None)
    if next_norm_w is None:
        return pl.pallas_call(
            _mm_res_kernel,
            grid=(M // tm, N // tn),
            in_specs=in_specs,
            out_specs=tile,
            out_shape=jax.ShapeDtypeStruct((M, N), F32),
            input_output_aliases=aliases,
            compiler_params=_params("parallel", "arbitrary"),
            name="matmul_residual",
        )(a, b, x)
    return pl.pallas_call(
        _mm_res_norm_kernel,
        grid=(M // tm, N // tn),
        in_specs=in_specs + [pl.BlockSpec((1, tn), lambda i, j: (0, j))],
        out_specs=[tile, tile, pl.BlockSpec((tm, LANES), lambda i, j: (i, 0))],
        out_shape=[jax.ShapeDtypeStruct((M, N), F32), jax.ShapeDtypeStruct((M, N), BF16),
                   jax.ShapeDtypeStruct((M, LANES), F32)],
        input_output_aliases=aliases,
        compiler_params=_params("parallel", "arbitrary"),
        name="matmul_residual_norm",
    )(a, b, x, next_norm_w.reshape(1, N))


def _swiglu_kernel(a_ref, ss_ref, wg_ref, wu_ref, o_ref, r_ref):
    @pl.when(pl.program_id(1) == 0)
    def _():
        r_ref[...] = _row_scale(ss_ref[...], a_ref.shape[1])
    wg = wg_ref[...].astype(BF16)
    wu = wu_ref[...].astype(BF16)
    for rows in _row_blocks(a_ref.shape[0]):
        scale = jnp.tile(r_ref[rows, :], (1, o_ref.shape[1] // LANES))
        a = a_ref[rows, :]
        g = _dot(a, wg) * scale
        u = _dot(a, wu) * scale
        o_ref[rows, :] = (g * _sigmoid(g) * u).astype(o_ref.dtype)


def swiglu_up(a, ss, wg, wu, layer, side_cast, tm=2048, tn=256):
    M, K = a.shape
    N = wg.shape[-1]
    tm, tn = min(tm, M), min(tn, N)
    assert M % tm == 0 and N % tn == 0
    grid = (M // tm, N // tn)
    if wg.ndim == 3:
        w_spec = pl.BlockSpec((None, K, tn), lambda i, j: (layer, 0, j))
        a_buffers = 1
    else:
        w_spec = pl.BlockSpec((K, tn), lambda i, j: (0, j))
        a_buffers = 2
    in_specs = [pl.BlockSpec((tm, K), lambda i, j: (i, 0), pipeline_mode=pl.Buffered(a_buffers)),
                pl.BlockSpec((tm, LANES), lambda i, j: (i, 0)),
                w_spec, w_spec]
    tile = pl.BlockSpec((tm, tn), lambda i, j: (i, j))
    side = _side_cast_specs(*side_cast, grid)
    if side is None:
        return pl.pallas_call(
            _swiglu_kernel,
            grid=grid,
            in_specs=in_specs,
            out_specs=tile,
            out_shape=jax.ShapeDtypeStruct((M, N), BF16),
            scratch_shapes=[pltpu.VMEM((tm, LANES), F32)],
            compiler_params=_params("parallel", "arbitrary"),
            name="swiglu_up",
        )(a, ss, wg, wu), None
    return pl.pallas_call(
        _with_side_cast(_swiglu_kernel, 4, 1),
        grid=grid,
        in_specs=in_specs + [side[0]],
        out_specs=[tile, side[1]],
        out_shape=[jax.ShapeDtypeStruct((M, N), BF16), side[2]],
        scratch_shapes=[pltpu.VMEM((tm, LANES), F32)],
        compiler_params=_params("parallel", "arbitrary"),
        name="swiglu_up_cast",
    )(a, ss, wg, wu, side_cast[0])


def _merge_kernel(yh_ref, yd_ref, yr_ref, gd_ref, wh_ref, wd_ref, wr_ref,
                  wgh_ref, wgd_ref, wgr_ref, bh_ref, bd_ref, br_ref, o_ref):
    gd = gd_ref[...]

    def branch(y_ref, w_ref, wg_ref, b_ref):
        gate = _sigmoid(_dot(gd, wg_ref[...]) + b_ref[...])
        return gate * _dot(y_ref[...], w_ref[...])

    acc = branch(yh_ref, wh_ref, wgh_ref, bh_ref)
    acc = acc + branch(yd_ref, wd_ref, wgd_ref, bd_ref)
    acc = acc + branch(yr_ref, wr_ref, wgr_ref, br_ref)
    o_ref[...] = acc.astype(o_ref.dtype)


def gated_merge(yh, yd, yr, gd, wh, wd, wr, wg, bg, tm=1024, tn=1024, side_cast=None):
    M = yh.shape[0]
    D = wh.shape[1]
    tm, tn = min(tm, M), min(tn, D)
    assert M % tm == 0 and D % tn == 0
    nb = D // tn
    row = lambda width: pl.BlockSpec((tm, width), lambda i, j: (i, 0))
    wcol = lambda k: pl.BlockSpec((k, tn), lambda i, j: (0, j))
    gcol = lambda br: pl.BlockSpec((GATE_RANK, tn), lambda i, j: (0, j + br * nb))
    bcol = lambda br: pl.BlockSpec((1, tn), lambda i, j: (0, j + br * nb))
    bg2 = bg.reshape(1, -1)
    return _call_2d(
        _merge_kernel, "gated_merge", (M // tm, nb),
        [row(yh.shape[1]), row(yd.shape[1]), row(yr.shape[1]), row(gd.shape[1]),
         wcol(wh.shape[0]), wcol(wd.shape[0]), wcol(wr.shape[0]),
         gcol(0), gcol(1), gcol(2), bcol(0), bcol(1), bcol(2)],
        pl.BlockSpec((tm, tn), lambda i, j: (i, j)),
        jax.ShapeDtypeStruct((M, D), BF16),
        (yh, yd, yr, gd, wh, wd, wr, wg, wg, wg, bg2, bg2, bg2), side_cast)


def _rope_table_kernel(inv_ref, cos_ref, sin_ref):
    T = cos_ref.shape[0]
    pos = (lax.broadcasted_iota(jnp.int32, (T, LANES), 0) + pl.program_id(0) * T).astype(F32)
    ang = pos * inv_ref[...]
    lane = lax.broadcasted_iota(jnp.int32, (T, LANES), 1)
    cos_ref[...] = jnp.cos(ang)
    sin_ref[...] = jnp.where(lane < LANES // 2, -jnp.sin(ang), jnp.sin(ang))


def rope_tables(S, T=256):
    T = min(T, S)
    half = RET_KDIM // 2
    inv = 1.0 / (ROPE_BASE ** (jnp.arange(half, dtype=F32) / half))
    inv2 = jnp.concatenate([inv, inv]).reshape(1, LANES)
    return pl.pallas_call(
        _rope_table_kernel,
        grid=(S // T,),
        in_specs=[pl.BlockSpec((1, LANES), lambda i: (0, 0))],
        out_specs=[pl.BlockSpec((T, LANES), lambda i: (i, 0))] * 2,
        out_shape=[jax.ShapeDtypeStruct((S, LANES), F32)] * 2,
        compiler_params=_params("parallel"),
        name="rope_tables",
    )(inv2)


def _bias_tile_kernel(tab_ref, o_ref):
    h = pl.program_id(0)
    T = o_ref.shape[2]
    j = lax.broadcasted_iota(jnp.int32, (T, T), 0)
    i = lax.broadcasted_iota(jnp.int32, (T, T), 1)
    max_exact = N_BUCKETS // 2
    o_ref[0, 3] = jnp.full((T, T), NEG, F32)
    for d in range(3):
        n = jnp.maximum(i - j + d * T, 0)
        nf = jnp.maximum(n, 1).astype(F32)
        large = max_exact + (jnp.log(nf / max_exact) / math.log(MAX_DISTANCE / max_exact)
                             * (N_BUCKETS - max_exact)).astype(jnp.int32)
        large = jnp.minimum(large, N_BUCKETS - 1)
        bucket = jnp.where(n < max_exact, n, large)
        val = jnp.zeros((T, T), F32)
        for b in range(N_BUCKETS):
            val = jnp.where(bucket == b, tab_ref[h, b], val)
        val = (val - tab_ref[h, N_BUCKETS - 1]) * LOG2E
        if d == 0:
            val = jnp.where(j > i, NEG, val)
        o_ref[0, d] = val


def bias_tiles(rel_bias, T):
    H = rel_bias.shape[1]
    return pl.pallas_call(
        _bias_tile_kernel,
        grid=(H,),
        in_specs=[pl.BlockSpec(memory_space=pltpu.SMEM)],
        out_specs=pl.BlockSpec((1, 4, T, T), lambda h: (h, 0, 0, 0)),
        out_shape=jax.ShapeDtypeStruct((H, 4, T, T), F32),
        compiler_params=_params("parallel"),
        name="bias_tiles",
    )(rel_bias.T)


def _hgrn_kernel(lbl_ref, nw_ref, q_ref, f_ref, i_ref, g_ref, o_ref,
                 st_ref, p_ref, r_ref, ol_ref, *, layer):
    @pl.when(pl.program_id(2) == 0)
    def _():
        st_ref[...] = jnp.zeros_like(st_ref)

    for hh in range(HGRN_STEP_HEADS):
        lanes = slice(hh * HGRN_DIM, (hh + 1) * HGRN_DIM)
        _hgrn_head(lbl_ref[:, lanes], nw_ref[...], q_ref[:, lanes], f_ref[:, lanes], i_ref[:, lanes],
                   g_ref[:, lanes], o_ref.at[:, lanes], st_ref.at[hh], p_ref.at[hh], r_ref.at[hh],
                   ol_ref.at[hh], layer)


def _hgrn_head(lg, nw, q, f, iv_all, g_all, o_ref, st_ref, p_ref, r_ref, ol_ref, layer):
    bcum, qf, kf = _hgrn_head_gates(lg, q, f, layer)
    _hgrn_head_mix(bcum, qf, kf, nw, iv_all, g_all, o_ref, st_ref, p_ref, r_ref, ol_ref)


def _hgrn_head_gates(lg, q, f, layer):
    T = q.shape[0]
    C = HGRN_CHUNK

    e = jnp.exp(lg - jnp.max(lg, axis=0, keepdims=True))
    sm = e / jnp.sum(e, axis=0, keepdims=True)
    csum = sm[0:1]
    for r in range(1, layer + 1):
        csum = csum + sm[r:r + 1]
    lb = csum - sm[0:1]

    forget = lb + (1.0 - lb) * _sigmoid(f)
    kf = 1.0 - forget
    logf = jnp.log(forget)
    qf = q * _sigmoid(q) * (HGRN_DIM ** -0.5)

    r_i = lax.broadcasted_iota(jnp.int32, (T, T), 0)
    c_i = lax.broadcasted_iota(jnp.int32, (T, T), 1)
    shift = C.bit_length() - 1
    same_chunk = (r_i >> shift) == (c_i >> shift)
    tri = jnp.where(c_i <= r_i, jnp.where(same_chunk, 1.0, 0.0), 0.0).astype(BF16)
    hi = logf.astype(BF16)
    rem = logf - hi.astype(F32)
    mid = rem.astype(BF16)
    lo = (rem - mid.astype(F32)).astype(BF16)
    bcum = (_dot(tri, hi) + _dot(tri, mid) + _dot(tri, lo)) * LOG2E
    return bcum, qf, kf


def _hgrn_head_mix(bcum, qf, kf, nw, iv_all, g_all, o_ref, st_ref, p_ref, r_ref, ol_ref):
    T = qf.shape[0]
    C, SUB = HGRN_CHUNK, HGRN_SUB
    row_c = lax.broadcasted_iota(jnp.int32, (C, HGRN_DIM), 0)
    row_s = lax.broadcasted_iota(jnp.int32, (SUB, HGRN_DIM), 0)
    d_i = lax.broadcasted_iota(jnp.int32, (2 * HGRN_DIM, 2 * HGRN_DIM), 0)
    d_j = lax.broadcasted_iota(jnp.int32, (2 * HGRN_DIM, 2 * HGRN_DIM), 1)
    ones2 = jnp.where((d_i < HGRN_DIM) == (d_j < HGRN_DIM), 1.0, 0.0).astype(BF16)

    def slot(r):
        half, rb = divmod(r, T // 2)
        return rb * SUB, slice(half * HGRN_DIM, (half + 1) * HGRN_DIM)

    n_sub = T // SUB
    causal_add = [jnp.where(row_s[(s // F32_TILE_ROWS) * F32_TILE_ROWS:] >= s, 0.0, NEG)
                  for s in range(SUB)]
    for sb in range(n_sub):
        sub = slice(sb * SUB, (sb + 1) * SUB)
        bi, qi = bcum[sub], qf[sub]
        for s in range(SUB):
            r = sb * SUB + s
            top = (s // F32_TILE_ROWS) * F32_TILE_ROWS
            dec = jnp.exp2(bi[top:] - bcum[r:r + 1] + causal_add[s])
            p = qi[top:] * kf[r:r + 1] * dec
            if top:
                p = jnp.concatenate([jnp.zeros((top, HGRN_DIM), F32), p], axis=0)
            r0, lanes = slot(r)
            p_ref[r0:r0 + SUB, lanes] = p.astype(BF16)
    r_ref[...] = _dot(p_ref[...], ones2)

    n_blk = C // SUB
    for c in range(T // C):
        c0 = c * C
        b, kc = bcum[c0:c0 + C], kf[c0:c0 + C]
        qts, kts = [], []
        for j in range(1, n_blk):
            lo_r = j * SUB
            ref = b[lo_r - 1:lo_r]
            qts.append((qf[c0 + lo_r:c0 + lo_r + SUB] * jnp.exp2(b[lo_r:lo_r + SUB] - ref)).astype(BF16))
            kts.append((kc * jnp.exp2(jnp.where(row_c < lo_r, ref - b, NEG))).astype(BF16))
        att_all = _dot_nt(jnp.concatenate(qts, axis=0), jnp.concatenate(kts, axis=0))
        att = jnp.concatenate([att_all[j * SUB:(j + 1) * SUB, j * C:(j + 1) * C]
                               for j in range(n_blk - 1)], axis=0)
        off = _dot(att.astype(BF16), iv_all[c0:c0 + C].astype(BF16))
        for j in range(n_blk):
            lo_r = c0 + j * SUB
            r0, lanes = slot(lo_r)
            oi = r_ref[r0:r0 + SUB, lanes] * iv_all[lo_r:lo_r + 1]
            for s in range(1, SUB):
                r = lo_r + s
                top = (s // F32_TILE_ROWS) * F32_TILE_ROWS
                r0, lanes = slot(r)
                term = r_ref[r0 + top:r0 + SUB, lanes] * iv_all[r:r + 1]
                oi = oi + term if top == 0 else jnp.concatenate([oi[:top], oi[top:] + term], axis=0)
            if j > 0:
                oi = oi + off[(j - 1) * SUB:j * SUB]
            ol_ref[lo_r:lo_r + SUB, :] = oi

    state_t = st_ref[...]
    for c in range(T // C):
        rows = slice(c * C, (c + 1) * C)
        b, qc, kc, iv = bcum[rows], qf[rows], kf[rows], iv_all[rows]
        o = ol_ref[rows, :] + _dot_nt((qc * jnp.exp2(b)).astype(BF16), state_t.astype(BF16))
        b_last = b[C - 1:C]
        khat = (kc * jnp.exp2(b_last - b)).astype(BF16)
        state_t = state_t * jnp.exp2(b_last) + _dot(iv.T.astype(BF16), khat)
        ms = jnp.mean(o * o, axis=-1, keepdims=True)
        gc = g_all[rows]
        y = o * lax.rsqrt(ms + RMS_EPS) * nw * (gc * _sigmoid(gc))
        o_ref[rows, :] = y.astype(o_ref.dtype)
    st_ref[...] = state_t


def hgrn_mixer(p_h, lb_logits, norm_w, layer, B, S, T=256):
    T = min(T, S)
    nt = S // T
    H, dk, NH = HGRN_HEADS, HGRN_DIM, HGRN_STEP_HEADS
    HG = H // NH
    col = lambda grp: pl.BlockSpec((T, NH * dk), lambda b, h, t: (b * nt + t, grp * HG + h))
    return pl.pallas_call(
        functools.partial(_hgrn_kernel, layer=layer),
        grid=(B, HG, nt),
        in_specs=[pl.BlockSpec((lb_logits.shape[0], NH * dk), lambda b, h, t: (0, h)),
                  pl.BlockSpec((1, dk), lambda b, h, t: (0, 0)),
                  col(0), col(1), col(2), col(3)],
        out_specs=pl.BlockSpec((T, NH * dk), lambda b, h, t: (b * nt + t, h)),
        out_shape=jax.ShapeDtypeStruct((B * S, H * dk), BF16),
        scratch_shapes=[pltpu.VMEM((NH, dk, dk), F32),
                        pltpu.VMEM((NH, T // 2 * HGRN_SUB, 2 * dk), BF16),
                        pltpu.VMEM((NH, T // 2 * HGRN_SUB, 2 * dk), F32),
                        pltpu.VMEM((NH, T, dk), F32)],
        compiler_params=_params("parallel", "parallel", "arbitrary"),
        name="hgrn_mixer",
    )(lb_logits, norm_w.reshape(1, dk), p_h, p_h, p_h, p_h)


def _mm_hgrn_kernel(a_ref, ss_ref, b_ref, lbl_ref, nw_ref, q_ref, f_ref, i_ref, g_ref,
                    o_ref, y_ref, r_ref, st_ref, p_ref, r2_ref, ol_ref,
                    *, layer, n_col_blocks, n_time_blocks):
    step = pl.program_id(0)

    @pl.when(step % n_col_blocks == 0)
    def _():
        r_ref[...] = _row_scale(ss_ref[...], a_ref.shape[1])

    @pl.when(step % n_time_blocks == 0)
    def _():
        st_ref[...] = jnp.zeros_like(st_ref)

    b = b_ref[...]
    sub = a_ref.shape[0] // HGRN_STEP_HEADS
    for hh in range(HGRN_STEP_HEADS):
        lanes = slice(hh * HGRN_DIM, (hh + 1) * HGRN_DIM)
        bcum, qf, kf = _hgrn_head_gates(lbl_ref[:, lanes], q_ref[:, lanes], f_ref[:, lanes], layer)
        rows = slice(hh * sub, (hh + 1) * sub)
        scale = jnp.tile(r_ref[rows, :], (1, o_ref.shape[1] // LANES))
        o_ref[rows, :] = (_dot(a_ref[rows, :], b) * scale).astype(o_ref.dtype)
        _hgrn_head_mix(bcum, qf, kf, nw_ref[...], i_ref[:, lanes], g_ref[:, lanes],
                       y_ref.at[:, lanes], st_ref.at[hh], p_ref.at[hh], r2_ref.at[hh], ol_ref.at[hh])


def matmul_cols_with_hgrn(a, ss, b, col_off, n, out_dtype, p_h, lb_logits, norm_w, layer, B, S,
                          tm=1024, tn=512, T=256):
    M, K = a.shape
    tm, tn, T = min(tm, M), min(tn, n), min(T, S)
    nt = S // T
    H, dk, NH = HGRN_HEADS, HGRN_DIM, HGRN_STEP_HEADS
    HG = H // NH
    ncb = n // tn
    if (M // tm) * ncb != B * HG * nt:
        return (matmul_cols(a, ss, b, col_off, n, out_dtype),
                hgrn_mixer(p_h, lb_logits, norm_w, layer, B, S))
    assert col_off % tn == 0 and n % tn == 0 and M % tm == 0
    off = col_off // tn
    row_blk = lambda s: (s // (HG * nt)) * nt + s % nt
    head_grp = lambda s: (s // nt) % HG
    col = lambda grp: pl.BlockSpec((T, NH * dk), lambda s: (row_blk(s), grp * HG + head_grp(s)))
    return pl.pallas_call(
        functools.partial(_mm_hgrn_kernel, layer=layer, n_col_blocks=ncb, n_time_blocks=nt),
        grid=((M // tm) * ncb,),
        in_specs=[pl.BlockSpec((tm, K), lambda s: (s // ncb, 0)),
                  pl.BlockSpec((tm, LANES), lambda s: (s // ncb, 0)),
                  pl.BlockSpec((K, tn), lambda s: (0, s % ncb + off)),
                  pl.BlockSpec((lb_logits.shape[0], NH * dk), lambda s: (0, head_grp(s))),
                  pl.BlockSpec((1, dk), lambda s: (0, 0)),
                  col(0), col(1), col(2), col(3)],
        out_specs=[pl.BlockSpec((tm, tn), lambda s: (s // ncb, s % ncb)),
                   pl.BlockSpec((T, NH * dk), lambda s: (row_blk(s), head_grp(s)))],
        out_shape=[jax.ShapeDtypeStruct((M, n), out_dtype),
                   jax.ShapeDtypeStruct((B * S, H * dk), BF16)],
        scratch_shapes=[pltpu.VMEM((tm, LANES), F32),
                        pltpu.VMEM((NH, dk, dk), F32),
                        pltpu.VMEM((NH, T // 2 * HGRN_SUB, 2 * dk), BF16),
                        pltpu.VMEM((NH, T // 2 * HGRN_SUB, 2 * dk), F32),
                        pltpu.VMEM((NH, T, dk), F32)],
        compiler_params=_params("arbitrary"),
        name="matmul_cols_hgrn",
    )(a, ss, b, lb_logits, norm_w.reshape(1, dk), p_h, p_h, p_h, p_h)


def _diff_kernel(lam_ref, nw_ref, q_ref, k_ref, v_ref, bias_ref, o_ref,
                 vt_ref, s0_ref, s1_ref, p0_ref, p1_ref, gm_ref, alpha_ref, m_ref, acc_ref, *, lam_init):
    QT = DIFF_QTILES
    T = q_ref.shape[0] // QT
    S = k_ref.shape[0]
    dh = DIFF_HEAD_DIM
    G = DIFF_GROUP
    n_groups = S // (G * T)
    qi = pl.program_id(2) * QT
    ng = (qi + QT - 1) // G + 1

    @pl.when(qi == 0)
    def _():
        def body(c, carry):
            start = pl.multiple_of(c * T, T)
            vt_ref[0:2 * dh, pl.ds(start, T)] = v_ref[pl.ds(start, T), :].astype(F32).T.astype(BF16)
            return carry
        lax.fori_loop(0, S // T, body, 0)
        pad_row = lax.broadcasted_iota(jnp.int32, (BF16_TILE_ROWS, S), 0)
        vt_ref[2 * dh:, :] = jnp.where(pad_row == 0, 1.0, 0.0).astype(BF16)

    row = lax.broadcasted_iota(jnp.int32, (2 * dh, T), 0)
    cols = []
    for t in range(QT):
        qt = q_ref[t * T:(t + 1) * T, :].astype(F32).T * (dh ** -0.5 * LOG2E)
        cols += [jnp.where(row < dh, qt, 0.0), jnp.where(row >= dh, qt, 0.0)]
    q2 = jnp.concatenate(cols, axis=1).astype(BF16)

    slots = (s0_ref, s1_ref)

    p_slots = (p0_ref, p1_ref)

    def values_group(g):
        pv = _dot(vt_ref[:, g * G * T:(g + 1) * G * T], p_slots[g % 2][...])
        acc_ref[...] = alpha_ref[g % 2] * acc_ref[...] + pv

    def block(g_scores, near, g_values, g_softmax):
        if g_values is not None:
            values_group(g_values)
        if g_softmax is not None:
            sm_slot, p_slot = slots[g_softmax % 2], p_slots[g_softmax % 2]
            m_old = m_ref[...]
            m_new = jnp.maximum(m_old, gm_ref[g_softmax % 2])
            alpha_ref[g_softmax % 2] = jnp.exp2(m_old - m_new)
        gm = None
        for u in range(G):
            if g_scores is not None:
                kt = g_scores * G + u
                s = _dot(k_ref[kt * T:(kt + 1) * T, :], q2)
                if near:
                    tiles = []
                    for t in range(QT):
                        d = qi + t - kt
                        tiles += [bias_ref[0, jnp.where(d < 0, 3, jnp.minimum(d, 2))]] * 2
                    s = s + jnp.concatenate(tiles, axis=1)
                slots[g_scores % 2][u] = s
                cm = jnp.max(s, axis=0, keepdims=True)
                gm = cm if gm is None else jnp.maximum(gm, cm)
            if g_softmax is not None:
                p_slot[u * T:(u + 1) * T, :] = jnp.exp2(sm_slot[u] - m_new).astype(BF16)
        if g_scores is not None:
            gm_ref[g_scores % 2] = gm
        if g_softmax is not None:
            m_ref[...] = m_new

    m_ref[...] = jnp.full(m_ref.shape, NEG, F32)
    acc_ref[...] = jnp.zeros(acc_ref.shape, F32)
    n_far_groups = jnp.maximum(qi - 1, 0) // G

    def stage(cond, g):
        for near in (False, True):
            is_near = g >= n_far_groups
            @pl.when(jnp.logical_and(cond, is_near if near else jnp.logical_not(is_near)))
            def _():
                block(g, near, None, g - 1 if g >= 1 else None)
                if g >= 1:
                    values_group(g - 1)

    stage(True, 0)
    for g in range(n_groups):
        if g + 1 < n_groups:
            stage(g < ng - 1, g + 1)

        @pl.when(g == ng - 1)
        def _():
            block(None, False, None, g)
            values_group(g)

    lp = lam_ref[...]
    lam = (jnp.exp(jnp.sum(lp[0:1] * lp[1:2], axis=-1, keepdims=True))
           - jnp.exp(jnp.sum(lp[2:3] * lp[3:4], axis=-1, keepdims=True)) + lam_init)
    acc = acc_ref[...]
    w = acc[:2 * dh] / acc[2 * dh:2 * dh + 1]
    for t in range(QT):
        c0 = 2 * t * T
        out = (w[:, c0:c0 + T] - lam * w[:, c0 + T:c0 + 2 * T]).T
        ms = jnp.mean(out * out, axis=-1, keepdims=True)
        y = out * lax.rsqrt(ms + RMS_EPS) * nw_ref[...] * (1.0 - lam_init)
        o_ref[t * T:(t + 1) * T, :] = y.astype(o_ref.dtype)


def diff_attention(p_d, bias, lam_params, norm_w, lam_init, B, S, side_casts=()):
    T = bias.shape[2]
    G = DIFF_GROUP
    QT = DIFF_QTILES
    assert T >= MAX_DISTANCE and S % (G * T) == 0 and G % QT == 0
    nq = S // (QT * T)
    lanes = 2 * QT * T
    H, hw = DIFF_HEADS, 2 * DIFF_HEAD_DIM
    grid = (B, H, nq)
    sides = [_side_cast_specs(w, layer, grid) for w, layer in side_casts]
    if any(side is None for side in sides):
        return (diff_attention(p_d, bias, lam_params, norm_w, lam_init, B, S),
                *[None] * len(side_casts))
    kernel_fn = functools.partial(_diff_kernel, lam_init=lam_init)
    if sides:
        kernel_fn = _with_side_cast(kernel_fn, 6, 1, len(sides))
    out_spec = pl.BlockSpec((QT * T, hw), lambda b, h, i: (b * nq + i, h))
    out_shape = jax.ShapeDtypeStruct((B * S, H * hw), BF16)
    return pl.pallas_call(
        kernel_fn,
        grid=grid,
        in_specs=[pl.BlockSpec(lam_params.shape, lambda b, h, i: (0, 0)),
                  pl.BlockSpec((1, hw), lambda b, h, i: (0, 0)),
                  pl.BlockSpec((QT * T, hw), lambda b, h, i: (b * nq + i, h)),
                  pl.BlockSpec((S, hw), lambda b, h, i: (b, H + h)),
                  pl.BlockSpec((S, hw), lambda b, h, i: (b, 2 * H + h)),
                  pl.BlockSpec((1, 4, T, T), lambda b, h, i: (h, 0, 0, 0))] + [s[0] for s in sides],
        out_specs=[out_spec] + [s[1] for s in sides] if sides else out_spec,
        out_shape=[out_shape] + [s[2] for s in sides] if sides else out_shape,
        scratch_shapes=[pltpu.VMEM((hw + BF16_TILE_ROWS, S), BF16),
                        pltpu.VMEM((G, T, lanes), F32), pltpu.VMEM((G, T, lanes), F32),
                        pltpu.VMEM((G * T, lanes), BF16), pltpu.VMEM((G * T, lanes), BF16),
                        pltpu.VMEM((2, 1, lanes), F32), pltpu.VMEM((2, 1, lanes), F32),
                        pltpu.VMEM((1, lanes), F32),
                        pltpu.VMEM((hw + BF16_TILE_ROWS, lanes), F32)],
        compiler_params=_params("parallel", "parallel", "arbitrary"),
        name="diff_attention",
    )(lam_params, norm_w.reshape(1, hw), p_d, p_d, p_d, bias, *[w for w, _ in side_casts])


def _ret_kernel(lg_ref, nw_ref, cos_ref, sin_ref, q_ref, k_ref, v_ref, g_ref, o_ref,
                st_ref, dec_ref, xi_ref, zeta_ref):
    C = q_ref.shape[0]
    dk, dv = RET_KDIM, RET_VDIM

    @pl.when(pl.program_id(2) == 0)
    def _():
        st_ref[...] = jnp.zeros_like(st_ref)
        rowf = lax.broadcasted_iota(jnp.int32, (C, dk), 0).astype(F32)
        r_i = lax.broadcasted_iota(jnp.int32, (C, C), 0)
        c_i = lax.broadcasted_iota(jnp.int32, (C, C), 1)
        for hh in range(RET_STEP_HEADS):
            lg = lg_ref[hh]
            xi_ref[hh] = jnp.exp((rowf + 1.0) * lg)
            zeta_ref[hh] = jnp.exp((C - 1.0 - rowf) * lg)
            dec_ref[hh] = jnp.exp(jnp.where(r_i >= c_i, (r_i - c_i).astype(F32) * lg[:, 0:1], NEG))

    cosf, sinf = cos_ref[...], sin_ref[...]

    def rot(x):
        return x * cosf + pltpu.roll(x, dk // 2, 1) * sinf

    for hh in range(RET_STEP_HEADS):
        qr = rot(q_ref[:, hh * dk:(hh + 1) * dk])
        kr = rot(k_ref[:, hh * dk:(hh + 1) * dk]) * (dk ** -0.5)
        v = v_ref[:, hh * dv:(hh + 1) * dv]
        state = st_ref[hh]
        scores = _dot_nt(qr.astype(BF16), kr.astype(BF16)) * dec_ref[hh]
        o = _dot(scores.astype(BF16), v) + _dot((qr * xi_ref[hh]).astype(BF16), state.astype(BF16))
        gamma_c = jnp.exp(C * lg_ref[hh][:, 0:1])
        st_ref[hh] = gamma_c * state + _dot((kr * zeta_ref[hh]).T.astype(BF16), v)

        ms = jnp.mean(o * o, axis=-1, keepdims=True)
        g = g_ref[:, hh * dv:(hh + 1) * dv].astype(F32)
        y = o * lax.rsqrt(ms + RMS_EPS) * nw_ref[...] * (g * _sigmoid(g))
        o_ref[:, hh * dv:(hh + 1) * dv] = y.astype(o_ref.dtype)


def retention_mixer(p_qk, p_vg, cos_t, sin_t, norm_w, B, S, C=256):
    C = min(C, S)
    nt = S // C
    H, dk, dv, NH = RET_HEADS, RET_KDIM, RET_VDIM, RET_STEP_HEADS
    HG = H // NH
    log_gamma = jnp.log(1.0 - 2.0 ** (-5.0 - jnp.arange(H, dtype=F32)))
    lg = jnp.broadcast_to(log_gamma[:, None, None], (H, 1, LANES))
    return pl.pallas_call(
        _ret_kernel,
        grid=(B, HG, nt),
        in_specs=[pl.BlockSpec((NH, 1, LANES), lambda b, h, t: (h, 0, 0)),
                  pl.BlockSpec((1, dv), lambda b, h, t: (0, 0)),
                  pl.BlockSpec((C, dk), lambda b, h, t: (t, 0)),
                  pl.BlockSpec((C, dk), lambda b, h, t: (t, 0)),
                  pl.BlockSpec((C, NH * dk), lambda b, h, t: (b * nt + t, h)),
                  pl.BlockSpec((C, NH * dk), lambda b, h, t: (b * nt + t, HG + h)),
                  pl.BlockSpec((C, NH * dv), lambda b, h, t: (b * nt + t, h)),
                  pl.BlockSpec((C, NH * dv), lambda b, h, t: (b * nt + t, HG + h))],
        out_specs=pl.BlockSpec((C, NH * dv), lambda b, h, t: (b * nt + t, h)),
        out_shape=jax.ShapeDtypeStruct((B * S, H * dv), BF16),
        scratch_shapes=[pltpu.VMEM((NH, dk, dv), F32), pltpu.VMEM((NH, C, C), F32),
                        pltpu.VMEM((NH, C, dk), F32), pltpu.VMEM((NH, C, dk), F32)],
        compiler_params=_params("parallel", "parallel", "arbitrary"),
        name="retention_mixer",
    )(lg, norm_w.reshape(1, dv), cos_t, sin_t, p_qk, p_qk, p_vg, p_vg)


def kernel(x, attn_norm_w, w_in, lb_logits, hgrn_norm_w, rel_bias, diff_lambda, diff_norm_w,
           ret_norm_w, w_gate_up, b_gate, w_br_hgrn, w_br_diff, w_br_ret, w_o, ffn_norm_w,
           w_ffn_gate, w_ffn_up, w_ffn_down, final_norm_w):
    B, S, D = x.shape
    depth = w_in.shape[0]
    M = B * S
    xs = x.reshape(M, D)

    cos_t, sin_t = rope_tables(S)
    bias = bias_tiles(rel_bias, min(256, S))

    off_d = 4 * HGRN_WIDTH
    off_rqk = off_d + 3 * DIFF_WIDTH
    off_rvg = off_rqk + 2 * RET_QK_WIDTH
    off_gd = off_rvg + 2 * RET_V_WIDTH

    h, ss = prenorm(xs, attn_norm_w[0])
    w_in_l = layer_weight_bf16(w_in, 0)
    for l in range(depth):
        def riding(call, w, **kw):
            out, w_bf = call(side_cast=(w, l), **kw)
            return out, (layer_weight_bf16(w, l) if w_bf is None else w_bf)

        p_h, w_ret = riding(functools.partial(matmul_cols, h, ss, w_in_l, 0, 4 * HGRN_WIDTH, F32),
                            w_br_ret)
        p_d = matmul_cols(h, ss, w_in_l, off_d, 3 * DIFF_WIDTH, BF16)
        p_rqk, w_hgrn = riding(functools.partial(matmul_cols, h, ss, w_in_l, off_rqk,
                                                 2 * RET_QK_WIDTH, F32), w_br_hgrn)
        p_rvg, y_h = matmul_cols_with_hgrn(h, ss, w_in_l, off_rvg, 2 * RET_V_WIDTH, BF16,
                                           p_h, lb_logits, hgrn_norm_w[l], l, B, S)
        gd, w_diff = riding(functools.partial(matmul_cols, h, ss, w_in_l, off_gd, GATE_RANK, BF16,
                                              tn=GATE_RANK), w_br_diff)

        lam_init = 0.8 - 0.6 * math.exp(-0.3 * l)
        y_d, w_gate, w_up = diff_attention(p_d, bias, diff_lambda[l], diff_norm_w[l], lam_init, B, S,
                                           side_casts=((w_ffn_gate, l), (w_ffn_up, l)))
        if w_gate is None:
            w_gate, w_up = w_ffn_gate, w_ffn_up
        y_r = retention_mixer(p_rqk, p_rvg, cos_t, sin_t, ret_norm_w[l], B, S)

        merged, w_o_l = riding(functools.partial(gated_merge, y_h, y_d, y_r, gd, w_hgrn, w_diff, w_ret,
                                                 layer_weight_bf16(w_gate_up, l), b_gate[l]), w_o)
        xs, h2, ss2 = matmul_residual(merged, w_o_l, xs, ffn_norm_w[l], in_place=l > 0)
        act, w_down = swiglu_up(h2, ss2, w_gate, w_up, l, (w_ffn_down, l))
        if w_down is None:
            w_down = layer_weight_bf16(w_ffn_down, l)
        if l + 1 < depth:
            xs, h, ss, w_in_l = matmul_residual(act, w_down, xs, attn_norm_w[l + 1], tm=512, tn=512,
                                                side_cast=(w_in, l + 1))
            if w_in_l is None:
                w_in_l = layer_weight_bf16(w_in, l + 1)
        else:
            xs = matmul_residual(act, w_down, xs, tm=512, tn=512)

    out = rms_norm_rows(xs, final_norm_w, x.dtype)
    return out.reshape(B, S, D)
```
